```python
import jax
import jax.numpy as jnp
from jax import lax
import numpy as np

D_MODEL = 2048
BATCH = 1
SEQ = 8192
DEPTH = 2

GRID_W = 64
CTX_LEN = 256
F32 = jnp.float32

N_HEADS = 8
N_KV_HEADS = 2
HEAD_DIM = 128
Q_GROUPS = N_HEADS // N_KV_HEADS
WINDOW = 128
BLOCK = 128
ROPE_THETA = 10000.0
Q_SCALE = HEAD_DIM ** -0.5

LRU_WIDTH = 1024
LRU_BLOCKS = 8
LRU_BLOCK_DIM = LRU_WIDTH // LRU_BLOCKS
LRU_CONV = 4
LRU_PAD = (LRU_CONV - 1) // 2
LRU_C = 8.0

SC_WIDTH = 1024
SC_CONV = 3
SC_PAD = (SC_CONV - 1) // 2

N_BRANCH = 3
BRANCH_W = 1024

Q_W = N_HEADS * HEAD_DIM
KV_W = N_KV_HEADS * HEAD_DIM
OFF_K = Q_W
OFF_V = OFF_K + KV_W
OFF_LX = OFF_V + KV_W
OFF_LG = OFF_LX + LRU_WIDTH
OFF_SB = OFF_LG + LRU_WIDTH
OFF_SC = OFF_SB + SC_WIDTH
OFF_SX = OFF_SC + SC_WIDTH
OFF_GATE = OFF_SX + SC_WIDTH
PROJ_W = OFF_GATE + N_BRANCH * D_MODEL
SPLIT_POINTS = (OFF_K, OFF_V, OFF_LX, OFF_LG, OFF_SB, OFF_SC, OFF_SX, OFF_GATE)

N_GROUPS = 4
EXPERTS_PER_GROUP = 8
N_EXPERTS = N_GROUPS * EXPERTS_PER_GROUP
TOP_K = 2
D_EXPERT = 1024

ALPHA = (2 * DEPTH) ** 0.25
BETA = (8 * DEPTH) ** -0.25
LN_EPS = 1e-5

kernel_name = 'hybrid_gated_branch_flow_block'


def layer_norm(x):
    xf = x.astype(F32)
    mu = jnp.mean(xf, axis=-1, keepdims=True)
    var = jnp.mean(jnp.square(xf - mu), axis=-1, keepdims=True)
    return (xf - mu) * lax.rsqrt(var + LN_EPS)


def modulate(x, shift, scale):
    return (layer_norm(x) * (1 + scale.astype(F32)) + shift.astype(F32)).astype(x.dtype)


def post_norm(x, g, b):
    return (layer_norm(x) * g.astype(F32) + b.astype(F32)).astype(x.dtype)


def rope_1d(x, pos):
    n = x.shape[-1]
    inv = ROPE_THETA ** (-jnp.arange(0, n, 2, dtype=F32) / n)
    ang = pos.astype(F32)[:, None] * inv[None, :]
    cos = jnp.cos(ang)[None, :, None, :]
    sin = jnp.sin(ang)[None, :, None, :]
    x1, x2 = x[..., : n // 2], x[..., n // 2:]
    return jnp.concatenate([x1 * cos - x2 * sin, x2 * cos + x1 * sin], axis=-1)


def rope_2d(x, rows, cols):
    xf = x.astype(F32)
    half = HEAD_DIM // 2
    out = jnp.concatenate([rope_1d(xf[..., :half], rows), rope_1d(xf[..., half:], cols)], axis=-1)
    return out.astype(x.dtype)


def dwconv(x, w, pad_left):
    k, ch = w.shape
    return lax.conv_general_dilated(
        x, w.reshape(k, 1, ch), window_strides=(1,),
        padding=((pad_left, k - 1 - pad_left),),
        dimension_numbers=('NWC', 'WIO', 'NWC'), feature_group_count=ch)


def window_attention(q, k, v, k_ctx, v_ctx, sink):
    b, s = q.shape[:2]
    nb = s // BLOCK
    qb = q.reshape(b, nb, BLOCK, N_KV_HEADS, Q_GROUPS, HEAD_DIM)

    def band(t):
        tp = jnp.pad(t, ((0, 0), (BLOCK, BLOCK), (0, 0), (0, 0)))
        tp = tp.reshape(b, nb + 2, BLOCK, N_KV_HEADS, HEAD_DIM)
        return jnp.concatenate([tp[:, :-2], tp[:, 1:-1], tp[:, 2:]], axis=2)

    kw, vw = band(k), band(v)
    s_loc = jnp.einsum('bnqkgd,bnskd->bnkgqs', qb, kw, preferred_element_type=F32)
    qi = jnp.arange(BLOCK)[:, None]
    kj = jnp.arange(3 * BLOCK)[None, :]
    rel = kj - BLOCK - qi
    kpos = jnp.arange(nb)[:, None] * BLOCK - BLOCK + jnp.arange(3 * BLOCK)[None, :]
    mask = (jnp.abs(rel) <= WINDOW)[None] & ((kpos >= 0) & (kpos < s))[:, None, :]
    s_loc = jnp.where(mask[None, :, None, None], s_loc, -jnp.inf)
    s_ctx = jnp.einsum('bnqkgd,bckd->bnkgqc', qb, k_ctx, preferred_element_type=F32)
    lc = k_ctx.shape[1]
    sink_b = jnp.broadcast_to(sink.astype(F32).reshape(1, 1, N_KV_HEADS, Q_GROUPS, 1, 1),
                              s_ctx.shape[:-1] + (1,))
    probs = jax.nn.softmax(jnp.concatenate([s_loc, s_ctx, sink_b], axis=-1), axis=-1)
    p_loc = probs[..., : 3 * BLOCK].astype(v.dtype)
    p_ctx = probs[..., 3 * BLOCK: 3 * BLOCK + lc].astype(v.dtype)
    o = (jnp.einsum('bnkgqs,bnskd->bnqkgd', p_loc, vw)
         + jnp.einsum('bnkgqc,bckd->bnqkgd', p_ctx, v_ctx))
    return o.reshape(b, s, Q_W)


def context_attention(q_ctx, k_ctx, v_ctx, sink):
    b, lc = q_ctx.shape[:2]
    qg = q_ctx.reshape(b, lc, N_KV_HEADS, Q_GROUPS, HEAD_DIM)
    sc = jnp.einsum('bqkgd,bckd->bkgqc', qg, k_ctx, preferred_element_type=F32)
    sink_b = jnp.broadcast_to(sink.astype(F32).reshape(1, N_KV_HEADS, Q_GROUPS, 1, 1),
                              sc.shape[:-1] + (1,))
    probs = jax.nn.softmax(jnp.concatenate([sc, sink_b], axis=-1), axis=-1)
    o = jnp.einsum('bkgqc,bckd->bqkgd', probs[..., :lc].astype(v_ctx.dtype), v_ctx)
    return o.reshape(b, lc, Q_W)


def _linear_combine(left, right):
    a_l, b_l = left
    a_r, b_r = right
    return a_l * a_r, a_r * b_l + b_r


def rg_lru_scan(xb, w_a, b_a, w_x, b_x, lam, h0, reverse):
    b, l, _ = xb.shape
    xh = xb.reshape(b, l, LRU_BLOCKS, LRU_BLOCK_DIM)
    r = jax.nn.sigmoid((jnp.einsum('blhi,hij->blhj', xh, w_a) + b_a).astype(F32)).reshape(b, l, LRU_WIDTH)
    i = jax.nn.sigmoid((jnp.einsum('blhi,hij->blhj', xh, w_x) + b_x).astype(F32)).reshape(b, l, LRU_WIDTH)
    log_a = -LRU_C * r * jax.nn.softplus(-lam.astype(F32))
    a = jnp.exp(log_a)
    gain = jnp.sqrt(-jnp.expm1(2.0 * log_a))
    u = gain * i * xb.astype(F32)
    edge = -1 if reverse else 0
    u = u.at[:, edge].add(a[:, edge] * h0)
    _, h = lax.associative_scan(_linear_combine, (a, u), axis=1, reverse=reverse)
    return h


def merge_branches(gate_logits, branches, w_branch, w_out):
    shp = gate_logits.shape[:-1]
    g = jax.nn.sigmoid(gate_logits.astype(F32)).astype(w_out.dtype).reshape(shp + (N_BRANCH, D_MODEL))
    z = g[..., 0, :] * (branches[0] @ w_branch[0])
    for j in range(1, N_BRANCH):
        z = z + g[..., j, :] * (branches[j] @ w_branch[j])
    return z @ w_out


def token_mixer(u, u_ctx, rows, cols, w_in, b_in, attn_sink, lru_conv_w, lru_conv_b,
                lru_w_a, lru_b_a, lru_w_x, lru_b_x, lru_lam, sc_conv_w, w_branch, w_out,
                with_ctx_out):
    b, s, _ = u.shape
    lc = u_ctx.shape[1]
    p = u @ w_in + b_in
    q, k, v, lx, lg, sb, sc, sx, gl = jnp.split(p, SPLIT_POINTS, axis=-1)
    pcs = u_ctx @ w_in[:, OFF_K:OFF_LG] + b_in[OFF_K:OFF_LG]
    kc, vc, lxc = jnp.split(pcs, (OFF_V - OFF_K, OFF_LX - OFF_K), axis=-1)
    kc = kc.reshape(b, lc, N_KV_HEADS, HEAD_DIM)
    vc = vc.reshape(b, lc, N_KV_HEADS, HEAD_DIM)

    q = rope_2d(q.reshape(b, s, N_HEADS, HEAD_DIM), rows, cols) * Q_SCALE
    k = rope_2d(k.reshape(b, s, N_KV_HEADS, HEAD_DIM), rows, cols)
    v = v.reshape(b, s, N_KV_HEADS, HEAD_DIM)
    y_att = window_attention(q, k, v, kc, vc, attn_sink)

    xb = dwconv(lx, lru_conv_w, LRU_PAD) + lru_conv_b
    xbc = dwconv(lxc, lru_conv_w, LRU_PAD) + lru_conv_b
    h_sum = None
    hc_sum = None
    for d, rev in enumerate((False, True)):
        hc = rg_lru_scan(xbc, lru_w_a[d], lru_b_a[d], lru_w_x[d], lru_b_x[d], lru_lam[d],
                         jnp.zeros((b, LRU_WIDTH), F32), rev)
        hc_last = hc[:, 0] if rev else hc[:, -1]
        h = rg_lru_scan(xb, lru_w_a[d], lru_b_a[d], lru_w_x[d], lru_b_x[d], lru_lam[d], hc_last, rev)
        h_sum = h if h_sum is None else h_sum + h
        if with_ctx_out:
            hc_sum = hc if hc_sum is None else hc_sum + hc
    y_lru = (h_sum * jax.nn.gelu(lg.astype(F32))).astype(u.dtype)

    y_sc = sb * dwconv(sc * sx, sc_conv_w, SC_PAD)

    y = merge_branches(gl, (y_att, y_lru, y_sc), w_branch, w_out)
    if not with_ctx_out:
        return y, None

    qc = (u_ctx @ w_in[:, :OFF_K] + b_in[:OFF_K]).reshape(b, lc, N_HEADS, HEAD_DIM) * Q_SCALE
    pco = u_ctx @ w_in[:, OFF_LG:] + b_in[OFF_LG:]
    lgc, sbc, scc, sxc, glc = jnp.split(
        pco, (OFF_SB - OFF_LG, OFF_SC - OFF_LG, OFF_SX - OFF_LG, OFF_GATE - OFF_LG), axis=-1)
    yc_att = context_attention(qc, kc, vc, attn_sink)
    yc_lru = (hc_sum * jax.nn.gelu(lgc.astype(F32))).astype(u_ctx.dtype)
    yc_sc = sbc * dwconv(scc * sxc, sc_conv_w, SC_PAD)
    y_ctx = merge_branches(glc, (yc_att, yc_lru, yc_sc), w_branch, w_out)
    return y, y_ctx


def hier_moe(u, w_grp, b_grp, w_exp, b_exp, w_gate, w_up, w_down):
    shp = u.shape
    xf = u.reshape(-1, D_MODEL)
    n = xf.shape[0]
    grp_logits = (xf @ w_grp + b_grp).astype(F32)
    grp_prob = jax.nn.softmax(grp_logits, axis=-1)
    g_idx = jnp.argmax(grp_logits, axis=-1)
    g_w = jnp.take_along_axis(grp_prob, g_idx[:, None], axis=-1)
    exp_logits = (xf @ w_exp + b_exp).astype(F32).reshape(n, N_GROUPS, EXPERTS_PER_GROUP)
    sel = jnp.take_along_axis(exp_logits, g_idx[:, None, None], axis=1)[:, 0]
    top_v, top_i = lax.top_k(sel, TOP_K)
    top_w = jax.nn.softmax(top_v, axis=-1) * g_w
    eid = g_idx[:, None] * EXPERTS_PER_GROUP + top_i
    combine = jnp.sum(jax.nn.one_hot(eid, N_EXPERTS, dtype=F32) * top_w[..., None], axis=1)
    combine = combine.astype(u.dtype)
    y = jnp.zeros_like(xf)
    for e in range(N_EXPERTS):
        h = jax.nn.silu(xf @ w_gate[e]) * (xf @ w_up[e])
        y = y + combine[:, e:e + 1] * (h @ w_down[e])
    return y.reshape(shp)


def setup_inputs(seed: int = 0) -> dict:
    key = jax.random.key(seed)
    keys = list(jax.random.split(key, 32))

    def nrm(shape, scale):
        return jax.random.normal(keys.pop(), shape, jnp.float32) * scale

    d = D_MODEL
    a_pow = jax.random.uniform(keys.pop(), (DEPTH, 2, LRU_WIDTH), jnp.float32, 0.9, 0.999)
    sig = a_pow ** (1.0 / LRU_C)
    return {
        'x': nrm((BATCH, SEQ, d), 1.0),
        'c': nrm((BATCH, d), 1.0),
        'ctx': nrm((BATCH, CTX_LEN, d), 1.0),
        'c_ctx': nrm((d,), 1.0),
        'w_mod': nrm((DEPTH, d, 6 * d), 0.5 * d ** -0.5),
        'b_mod': nrm((DEPTH, 6 * d), 0.02),
        'w_in': nrm((DEPTH, d, PROJ_W), d ** -0.5),
        'b_in': nrm((DEPTH, PROJ_W), 0.02),
        'attn_sink': nrm((DEPTH, N_HEADS), 0.5),
        'lru_conv_w': nrm((DEPTH, LRU_CONV, LRU_WIDTH), LRU_CONV ** -0.5),
        'lru_conv_b': nrm((DEPTH, LRU_WIDTH), 0.02),
        'lru_w_a': nrm((DEPTH, 2, LRU_BLOCKS, LRU_BLOCK_DIM, LRU_BLOCK_DIM), LRU_BLOCK_DIM ** -0.5),
        'lru_b_a': nrm((DEPTH, 2, LRU_BLOCKS, LRU_BLOCK_DIM), 0.02),
        'lru_w_x': nrm((DEPTH, 2, LRU_BLOCKS, LRU_BLOCK_DIM, LRU_BLOCK_DIM), LRU_BLOCK_DIM ** -0.5),
        'lru_b_x': nrm((DEPTH, 2, LRU_BLOCKS, LRU_BLOCK_DIM), 0.02),
        'lru_lam': jnp.log(sig) - jnp.log1p(-sig),
        'sc_conv_w': nrm((DEPTH, SC_CONV, SC_WIDTH), SC_CONV ** -0.5),
        'w_branch': nrm((DEPTH, N_BRANCH, BRANCH_W, d), BETA * BRANCH_W ** -0.5),
        'w_out': nrm((DEPTH, d, d), BETA * d ** -0.5),
        'ln1_g': 1.0 + nrm((DEPTH, d), 0.02),
        'ln1_b': nrm((DEPTH, d), 0.02),
        'ln2_g': 1.0 + nrm((DEPTH, d), 0.02),
        'ln2_b': nrm((DEPTH, d), 0.02),
        'w_grp': nrm((DEPTH, d, N_GROUPS), d ** -0.5),
        'b_grp': nrm((DEPTH, N_GROUPS), 0.01),
        'w_exp': nrm((DEPTH, d, N_EXPERTS), d ** -0.5),
        'b_exp': nrm((DEPTH, N_EXPERTS), 0.01),
        'w_gate': nrm((DEPTH, N_EXPERTS, d, D_EXPERT), d ** -0.5),
        'w_up': nrm((DEPTH, N_EXPERTS, d, D_EXPERT), d ** -0.5),
        'w_down': nrm((DEPTH, N_EXPERTS, D_EXPERT, d), BETA * D_EXPERT ** -0.5),
    }


def reference(x, c, ctx, c_ctx, w_mod, b_mod, w_in, b_in, attn_sink, lru_conv_w, lru_conv_b,
              lru_w_a, lru_b_a, lru_w_x, lru_b_x, lru_lam, sc_conv_w, w_branch, w_out,
              ln1_g, ln1_b, ln2_g, ln2_b, w_grp, b_grp, w_exp, b_exp, w_gate, w_up, w_down):
    b, s, _ = x.shape
    n_rows = s // GRID_W
    rows = jnp.repeat(jnp.arange(n_rows), GRID_W)
    cols = jnp.arange(n_rows * GRID_W) % GRID_W
    xc = ctx
    for l in range(DEPTH):
        last = l == DEPTH - 1
        n_cmod = 2 if last else 6
        mod = (jax.nn.silu(c) @ w_mod[l] + b_mod[l])[:, None, :]
        mod_c = (jax.nn.silu(c_ctx) @ w_mod[l, :, : n_cmod * D_MODEL]
                 + b_mod[l, : n_cmod * D_MODEL])[None, None, :]
        shift1, scale1, gate1, shift2, scale2, gate2 = jnp.split(mod, 6, axis=-1)
        cmod = jnp.split(mod_c, n_cmod, axis=-1)

        u = modulate(x, shift1, scale1)
        u_ctx = modulate(xc, cmod[0], cmod[1])
        y, y_ctx = token_mixer(u, u_ctx, rows, cols, w_in[l], b_in[l], attn_sink[l],
                               lru_conv_w[l], lru_conv_b[l], lru_w_a[l], lru_b_a[l],
                               lru_w_x[l], lru_b_x[l], lru_lam[l], sc_conv_w[l],
                               w_branch[l], w_out[l], not last)
        x = post_norm(ALPHA * x + gate1 * y, ln1_g[l], ln1_b[l])
        if last:
            m = hier_moe(modulate(x, shift2, scale2), w_grp[l], b_grp[l], w_exp[l], b_exp[l],
                         w_gate[l], w_up[l], w_down[l])
            x = post_norm(ALPHA * x + gate2 * m, ln2_g[l], ln2_b[l])
        else:
            xc = post_norm(ALPHA * xc + cmod[2] * y_ctx, ln1_g[l], ln1_b[l])
            both = jnp.concatenate([modulate(x, shift2, scale2), modulate(xc, cmod[3], cmod[4])], axis=1)
            m = hier_moe(both, w_grp[l], b_grp[l], w_exp[l], b_exp[l], w_gate[l], w_up[l], w_down[l])
            x = post_norm(ALPHA * x + gate2 * m[:, :s], ln2_g[l], ln2_b[l])
            xc = post_norm(ALPHA * xc + cmod[5] * m[:, s:], ln2_g[l], ln2_b[l])
    return x
```

```python
import functools

import jax
import jax.numpy as jnp
from jax import lax
from jax.experimental import pallas as pl
from jax.experimental.pallas import tpu as pltpu

F32 = jnp.float32
BF16 = jnp.bfloat16

D_MODEL = 2048
SEQ = 8192
CTX_LEN = 256
DEPTH = 2
GRID_W = 64

N_HEADS = 8
N_KV_HEADS = 2
HEAD_DIM = 128
Q_GROUPS = N_HEADS // N_KV_HEADS
WINDOW = 128
BLOCK = 128
ROPE_THETA = 10000.0
Q_SCALE = HEAD_DIM ** -0.5

LRU_WIDTH = 1024
LRU_BLOCKS = 8
LRU_BLOCK_DIM = 128
LRU_C = 8.0
SC_WIDTH = 1024
N_BRANCH = 3
BRANCH_W = 1024

Q_W = N_HEADS * HEAD_DIM
KV_W = N_KV_HEADS * HEAD_DIM
OFF_K = Q_W
OFF_V = OFF_K + KV_W
OFF_LX = OFF_V + KV_W
OFF_LG = OFF_LX + LRU_WIDTH
OFF_SB = OFF_LG + LRU_WIDTH
OFF_SC = OFF_SB + SC_WIDTH
OFF_SX = OFF_SC + SC_WIDTH
OFF_GATE = OFF_SX + SC_WIDTH
PROJ_W = OFF_GATE + N_BRANCH * D_MODEL

N_GROUPS = 4
EXPERTS_PER_GROUP = 8
N_EXPERTS = N_GROUPS * EXPERTS_PER_GROUP
D_EXPERT = 1024

ALPHA = (2 * DEPTH) ** 0.25
LN_EPS = 1e-5

VMEM_LIMIT_BYTES = 52 * 1024 * 1024
LANES = 128
SUBLANES = 8

HALF_W = 512
PROJ_TN = 1280
MOE_TM = 256


def _cparams(*sem):
    return pltpu.CompilerParams(dimension_semantics=sem, vmem_limit_bytes=VMEM_LIMIT_BYTES)


def _layer_norm(x):
    mu = jnp.mean(x, axis=-1, keepdims=True)
    xc = x - mu
    var = jnp.mean(xc * xc, axis=-1, keepdims=True)
    return xc * lax.rsqrt(var + LN_EPS)


def _mod_kernel(c_ref, w_ref, b_ref, o_ref):
    a = jax.nn.silu(c_ref[...]).astype(BF16)
    o_ref[...] = jnp.dot(a, w_ref[...].astype(BF16), preferred_element_type=F32) + b_ref[...]


def _mod_call(c8, w, b):
    n = w.shape[1]
    tn = 1024
    return pl.pallas_call(
        _mod_kernel,
        out_shape=jax.ShapeDtypeStruct((SUBLANES, n), F32),
        grid=(n // tn,),
        in_specs=[pl.BlockSpec((SUBLANES, D_MODEL), lambda j: (0, 0)),
                  pl.BlockSpec((D_MODEL, tn), lambda j: (0, j)),
                  pl.BlockSpec((1, tn), lambda j: (0, j))],
        out_specs=pl.BlockSpec((SUBLANES, tn), lambda j: (0, j)),
        compiler_params=_cparams("arbitrary"),
        name="mod_vectors",
    )(c8, w, b.reshape(1, n))


def _lnmod_kernel(x_ref, sh_ref, sc_ref, o_ref):
    y = _layer_norm(x_ref[...])
    o_ref[...] = (y * (1.0 + sc_ref[...]) + sh_ref[...]).astype(o_ref.dtype)


def _lnmod_call(x, mod, shift_chunk, scale_chunk, out_dtype):
    t = x.shape[0]
    tm = 256
    return pl.pallas_call(
        _lnmod_kernel,
        out_shape=jax.ShapeDtypeStruct((t, D_MODEL), out_dtype),
        grid=(t // tm,),
        in_specs=[pl.BlockSpec((tm, D_MODEL), lambda i: (i, 0)),
                  pl.BlockSpec((1, D_MODEL), lambda i: (0, shift_chunk)),
                  pl.BlockSpec((1, D_MODEL), lambda i: (0, scale_chunk))],
        out_specs=pl.BlockSpec((tm, D_MODEL), lambda i: (i, 0)),
        compiler_params=_cparams("arbitrary"),
        name="ln_modulate",
    )(x, mod, mod)


def _rope_head(xh, cos, sin_signed, first_half):
    partner = jnp.where(first_half, pltpu.roll(xh, LANES - 32, 1), pltpu.roll(xh, 32, 1))
    return xh * cos + partner * sin_signed


def _proj_kernel(u_ref, w_ref, b_ref, cos_ref, sin_ref, o_ref, *, rope):
    acc = jnp.dot(u_ref[...], w_ref[...], preferred_element_type=F32) + b_ref[...]
    if not rope:
        o_ref[...] = acc
        return
    j = pl.program_id(1)

    @pl.when(j != 0)
    def _():
        o_ref[...] = acc

    @pl.when(j == 0)
    def _():
        cos = cos_ref[...]
        sin_signed = sin_ref[...]
        lane = lax.broadcasted_iota(jnp.int32, cos.shape, 1)
        first_half = (lane & 63) < 32
        for h in range(N_HEADS + N_KV_HEADS):
            sl = slice(h * HEAD_DIM, (h + 1) * HEAD_DIM)
            r = _rope_head(acc[:, sl], cos, sin_signed, first_half)
            if h < N_HEADS:
                r = r * Q_SCALE
            o_ref[:, sl] = r


def _proj_call(u, w_bf, b, cos, sin_signed, *, rope, tm):
    t = u.shape[0]
    n = w_bf.shape[1]
    kern = functools.partial(_proj_kernel, rope=rope)
    return pl.pallas_call(
        kern,
        out_shape=jax.ShapeDtypeStruct((t, n), F32),
        grid=(t // tm, n // PROJ_TN),
        in_specs=[pl.BlockSpec((tm, D_MODEL), lambda i, j: (i, 0)),
                  pl.BlockSpec((D_MODEL, PROJ_TN), lambda i, j: (0, j)),
                  pl.BlockSpec((1, PROJ_TN), lambda i, j: (0, j)),
                  pl.BlockSpec((tm, HEAD_DIM), lambda i, j: (i, 0)),
                  pl.BlockSpec((tm, HEAD_DIM), lambda i, j: (i, 0))],
        out_specs=pl.BlockSpec((tm, PROJ_TN), lambda i, j: (i, j)),
        compiler_params=_cparams("arbitrary", "arbitrary"),
        name="in_proj_rope" if rope else "in_proj_ctx",
    )(u, w_bf, b.reshape(1, n), cos, sin_signed)


def _softmax_pv(q, k_all, v_all, bias, sink_col):
    s = lax.dot_general(q, k_all, (((1,), (1,)), ((), ())), preferred_element_type=F32)
    if bias is not None:
        s = s + bias
    m = jnp.maximum(jnp.max(s, axis=-1, keepdims=True), sink_col)
    e = jnp.exp(s - m)
    den = jnp.sum(e, axis=-1, keepdims=True) + jnp.exp(sink_col - m)
    p = (e / den).astype(BF16)
    return jnp.dot(p, v_all, preferred_element_type=F32)


def _sink_column(sink_ref, kv, rows):
    row = lax.broadcasted_iota(jnp.int32, (Q_GROUPS * rows, 1), 0)
    col = jnp.full((Q_GROUPS * rows, 1), sink_ref[kv * Q_GROUPS + Q_GROUPS - 1], F32)
    for g in range(Q_GROUPS - 2, -1, -1):
        col = jnp.where(row < (g + 1) * rows, sink_ref[kv * Q_GROUPS + g], col)
    return col


def _attn_kernel(sink_ref, q_ref, kp_ref, kc_ref, kn_ref, vp_ref, vc_ref, vn_ref,
                 kx_ref, vx_ref, o_ref, *, seq):
    n = pl.program_id(0)
    n_loc = 3 * BLOCK
    n_keys = n_loc + CTX_LEN
    qi = lax.broadcasted_iota(jnp.int32, (BLOCK, n_keys), 0)
    kj = lax.broadcasted_iota(jnp.int32, (BLOCK, n_keys), 1)
    rel = kj - BLOCK - qi
    kpos = n * BLOCK - BLOCK + kj
    bad = jnp.where(kj < n_loc,
                    jnp.where(jnp.abs(rel) > WINDOW, 1, 0) + jnp.where(kpos < 0, 1, 0)
                    + jnp.where(kpos >= seq, 1, 0), 0)
    bias1 = jnp.where(bad > 0, -jnp.inf, 0.0).astype(F32)
    bias = jnp.concatenate([bias1] * Q_GROUPS, axis=0)
    for kv in range(N_KV_HEADS):
        ks = slice(kv * HEAD_DIM, (kv + 1) * HEAD_DIM)
        k_all = jnp.concatenate([kp_ref[:, ks], kc_ref[:, ks], kn_ref[:, ks], kx_ref[:, ks]],
                                axis=0).astype(BF16)
        v_all = jnp.concatenate([vp_ref[:, ks], vc_ref[:, ks], vn_ref[:, ks], vx_ref[:, ks]],
                                axis=0).astype(BF16)
        q = jnp.concatenate(
            [q_ref[:, (kv * Q_GROUPS + g) * HEAD_DIM:(kv * Q_GROUPS + g + 1) * HEAD_DIM]
             for g in range(Q_GROUPS)], axis=0).astype(BF16)
        o = _softmax_pv(q, k_all, v_all, bias, _sink_column(sink_ref, kv, BLOCK))
        for g in range(Q_GROUPS):
            h = kv * Q_GROUPS + g
            o_ref[:, h * HEAD_DIM:(h + 1) * HEAD_DIM] = o[g * BLOCK:(g + 1) * BLOCK].astype(o_ref.dtype)


def _attn_call(p, p_ctx, sink):
    s = p.shape[0]
    nb = s // BLOCK
    kcol = OFF_K // KV_W
    vcol = OFF_V // KV_W
    kv_spec = lambda f, c: pl.BlockSpec((BLOCK, KV_W), lambda n: (f(n), c))
    prev = lambda n: jnp.maximum(n - 1, 0)
    cur = lambda n: n
    nxt = lambda n: jnp.minimum(n + 1, nb - 1)
    return pl.pallas_call(
        functools.partial(_attn_kernel, seq=s),
        out_shape=jax.ShapeDtypeStruct((s, Q_W), BF16),
        grid=(nb,),
        in_specs=[pl.BlockSpec(memory_space=pltpu.SMEM),
                  pl.BlockSpec((BLOCK, Q_W), lambda n: (n, 0)),
                  kv_spec(prev, kcol), kv_spec(cur, kcol), kv_spec(nxt, kcol),
                  kv_spec(prev, vcol), kv_spec(cur, vcol), kv_spec(nxt, vcol),
                  pl.BlockSpec((CTX_LEN, KV_W), lambda n: (0, kcol)),
                  pl.BlockSpec((CTX_LEN, KV_W), lambda n: (0, vcol))],
        out_specs=pl.BlockSpec((BLOCK, Q_W), lambda n: (n, 0)),
        compiler_params=_cparams("arbitrary"),
        name="window_attention",
    )(sink, p, p, p, p, p, p, p, p_ctx, p_ctx)


def _ctx_attn_kernel(sink_ref, q_ref, k_ref, v_ref, o_ref):
    rows = q_ref.shape[0]
    for kv in range(N_KV_HEADS):
        ks = slice(kv * HEAD_DIM, (kv + 1) * HEAD_DIM)
        q = jnp.concatenate(
            [q_ref[:, (kv * Q_GROUPS + g) * HEAD_DIM:(kv * Q_GROUPS + g + 1) * HEAD_DIM] * Q_SCALE
             for g in range(Q_GROUPS)], axis=0).astype(BF16)
        o = _softmax_pv(q, k_ref[:, ks].astype(BF16), v_ref[:, ks].astype(BF16), None,
                        _sink_column(sink_ref, kv, rows))
        for g in range(Q_GROUPS):
            h = kv * Q_GROUPS + g
            o_ref[:, h * HEAD_DIM:(h + 1) * HEAD_DIM] = o[g * rows:(g + 1) * rows].astype(o_ref.dtype)


def _ctx_attn_call(p_ctx, sink):
    lc = p_ctx.shape[0]
    return pl.pallas_call(
        _ctx_attn_kernel,
        out_shape=jax.ShapeDtypeStruct((lc, Q_W), BF16),
        grid=(1,),
        in_specs=[pl.BlockSpec(memory_space=pltpu.SMEM),
                  pl.BlockSpec((lc, Q_W), lambda n: (0, 0)),
                  pl.BlockSpec((lc, KV_W), lambda n: (0, OFF_K // KV_W)),
                  pl.BlockSpec((lc, KV_W), lambda n: (0, OFF_V // KV_W))],
        out_specs=pl.BlockSpec((lc, Q_W), lambda n: (0, 0)),
        compiler_params=_cparams("arbitrary"),
        name="context_attention",
    )(sink, p_ctx, p_ctx, p_ctx)


def _lru_conv(ext_scr, cur_ref, prev_ref, next_ref, is_first, is_last, cw_ref, cb_ref, tc):
    ext_scr[0:SUBLANES, :] = jnp.where(is_first, 0.0, prev_ref[...])
    ext_scr[SUBLANES:SUBLANES + tc, :] = cur_ref[...]
    ext_scr[SUBLANES + tc:, :] = jnp.where(is_last, 0.0, next_ref[...])
    out = cb_ref[...] + cw_ref[0:1, :] * ext_scr[SUBLANES - 1:SUBLANES - 1 + tc, :]
    for j in range(1, 4):
        out = out + cw_ref[j:j + 1, :] * ext_scr[SUBLANES - 1 + j:SUBLANES - 1 + j + tc, :]
    return out


def _lru_gates(xb, d, wa_ref, ba_ref, wx_ref, bx_ref, lam_ref, a_scr, u_scr):
    xb_bf = xb.astype(BF16)
    for b in range(HALF_W // LRU_BLOCK_DIM):
        sl = slice(b * LRU_BLOCK_DIM, (b + 1) * LRU_BLOCK_DIM)
        xs = xb_bf[:, sl]
        r = jax.nn.sigmoid(jnp.dot(xs, wa_ref[d, b].astype(BF16), preferred_element_type=F32)
                           + ba_ref[d:d + 1, sl])
        ig = jax.nn.sigmoid(jnp.dot(xs, wx_ref[d, b].astype(BF16), preferred_element_type=F32)
                            + bx_ref[d:d + 1, sl])
        log_a = -LRU_C * r * jax.nn.softplus(-lam_ref[d:d + 1, sl])
        a_scr[d, :, sl] = jnp.exp(log_a)
        u_scr[d, :, sl] = jnp.sqrt(1.0 - jnp.exp(2.0 * log_a)) * ig * xb[:, sl]


def _lru_kernel(fc_ref, fp_ref, fn_ref, rc_ref, rp_ref, rn_ref,
                cw_ref, cb_ref, wa_ref, ba_ref, wx_ref, bx_ref, lam_ref, h0_ref,
                hf_ref, hr_ref, ext_scr, a_scr, u_scr, cf_scr, cr_scr, *, tc, nchunks):
    i = pl.program_id(1)
    w = HALF_W

    @pl.when(i == 0)
    def _():
        cf_scr[...] = jnp.broadcast_to(h0_ref[0:1, :], (SUBLANES, w))
        cr_scr[...] = jnp.broadcast_to(h0_ref[1:2, :], (SUBLANES, w))

    xb_f = _lru_conv(ext_scr, fc_ref, fp_ref, fn_ref, i == 0, i == nchunks - 1, cw_ref, cb_ref, tc)
    _lru_gates(xb_f, 0, wa_ref, ba_ref, wx_ref, bx_ref, lam_ref, a_scr, u_scr)
    xb_r = _lru_conv(ext_scr, rc_ref, rp_ref, rn_ref, i == nchunks - 1, i == 0, cw_ref, cb_ref, tc)
    _lru_gates(xb_r, 1, wa_ref, ba_ref, wx_ref, bx_ref, lam_ref, a_scr, u_scr)

    row = lax.broadcasted_iota(jnp.int32, (SUBLANES, w), 0)
    ntile = tc // SUBLANES

    def body(j, carry):
        cf, cr = carry
        off = pl.multiple_of(j * SUBLANES, SUBLANES)
        a = a_scr[0, pl.ds(off, SUBLANES), :]
        u = u_scr[0, pl.ds(off, SUBLANES), :]
        for s in (1, 2, 4):
            m = row >= s
            u = jnp.where(m, u + a * pltpu.roll(u, s, 0), u)
            a = jnp.where(m, a * pltpu.roll(a, s, 0), a)
        h = u + a * cf
        hf_ref[pl.ds(off, SUBLANES), :] = h
        cf = jnp.broadcast_to(h[SUBLANES - 1:SUBLANES, :], (SUBLANES, w))
        offr = pl.multiple_of((ntile - 1 - j) * SUBLANES, SUBLANES)
        a = a_scr[1, pl.ds(offr, SUBLANES), :]
        u = u_scr[1, pl.ds(offr, SUBLANES), :]
        for s in (1, 2, 4):
            m = row < SUBLANES - s
            u = jnp.where(m, u + a * pltpu.roll(u, SUBLANES - s, 0), u)
            a = jnp.where(m, a * pltpu.roll(a, SUBLANES - s, 0), a)
        h = u + a * cr
        hr_ref[pl.ds(offr, SUBLANES), :] = h
        cr = jnp.broadcast_to(h[0:1, :], (SUBLANES, w))
        return cf, cr

    cf, cr = lax.fori_loop(0, ntile, body, (cf_scr[...], cr_scr[...]))
    cf_scr[...] = cf
    cr_scr[...] = cr


def _lru_call(p, h0, cw, cb, wa, ba, wx, bx, lam, *, tc):
    t = p.shape[0]
    nchunks = t // tc
    w = HALF_W
    c0 = OFF_LX // w
    r8 = tc // SUBLANES
    last8 = t // SUBLANES - 1
    fwd = lambda i: i
    rev = lambda i: nchunks - 1 - i
    cur_spec = lambda f: pl.BlockSpec((tc, w), lambda h, i: (f(i), c0 + h))
    prev_spec = lambda f: pl.BlockSpec((SUBLANES, w), lambda h, i: (jnp.maximum(f(i) * r8 - 1, 0), c0 + h))
    next_spec = lambda f: pl.BlockSpec((SUBLANES, w), lambda h, i: (jnp.minimum((f(i) + 1) * r8, last8), c0 + h))
    vec_spec = lambda rows: pl.BlockSpec((rows, w), lambda h, i: (0, h))
    gate_w_spec = pl.BlockSpec((2, w // LRU_BLOCK_DIM, LRU_BLOCK_DIM, LRU_BLOCK_DIM),
                               lambda h, i: (0, h, 0, 0))
    kern = functools.partial(_lru_kernel, tc=tc, nchunks=nchunks)
    return pl.pallas_call(
        kern,
        out_shape=(jax.ShapeDtypeStruct((t, LRU_WIDTH), F32), jax.ShapeDtypeStruct((t, LRU_WIDTH), F32)),
        grid=(LRU_WIDTH // w, nchunks),
        in_specs=[cur_spec(fwd), prev_spec(fwd), next_spec(fwd),
                  cur_spec(rev), prev_spec(rev), next_spec(rev),
                  vec_spec(4), vec_spec(1), gate_w_spec, vec_spec(2), gate_w_spec, vec_spec(2),
                  vec_spec(2), vec_spec(2)],
        out_specs=(pl.BlockSpec((tc, w), lambda h, i: (i, h)),
                   pl.BlockSpec((tc, w), lambda h, i: (nchunks - 1 - i, h))),
        scratch_shapes=[pltpu.VMEM((tc + 2 * SUBLANES, w), F32),
                        pltpu.VMEM((2, tc, w), F32), pltpu.VMEM((2, tc, w), F32),
                        pltpu.VMEM((SUBLANES, w), F32), pltpu.VMEM((SUBLANES, w), F32)],
        compiler_params=_cparams("arbitrary", "arbitrary"),
        name="rg_lru",
    )(p, p, p, p, p, p, cw, cb.reshape(1, LRU_WIDTH), wa, ba.reshape(2, LRU_WIDTH),
      wx, bx.reshape(2, LRU_WIDTH), lam, h0)


def _prep_kernel(lg_ref, hf_ref, hr_ref, sb_ref, scc_ref, scp_ref, scn_ref,
                 sxc_ref, sxp_ref, sxn_ref, w3_ref, ylru_ref, ysc_ref, ext_scr, *, tm, ntiles):
    i = pl.program_id(0)
    ylru_ref[...] = ((hf_ref[...] + hr_ref[...]) * jax.nn.gelu(lg_ref[...])).astype(ylru_ref.dtype)
    ext_scr[0:SUBLANES, :] = jnp.where(i == 0, 0.0, scp_ref[...] * sxp_ref[...])
    ext_scr[SUBLANES:SUBLANES + tm, :] = scc_ref[...] * sxc_ref[...]
    ext_scr[SUBLANES + tm:, :] = jnp.where(i == ntiles - 1, 0.0, scn_ref[...] * sxn_ref[...])
    conv = w3_ref[0:1, :] * ext_scr[SUBLANES - 1:SUBLANES - 1 + tm, :]
    for j in range(1, 3):
        conv = conv + w3_ref[j:j + 1, :] * ext_scr[SUBLANES - 1 + j:SUBLANES - 1 + j + tm, :]
    ysc_ref[...] = (sb_ref[...] * conv).astype(ysc_ref.dtype)


def _prep_call(p, hf, hr, w3, *, tm):
    t = p.shape[0]
    ntiles = t // tm
    w = HALF_W
    r8 = tm // SUBLANES
    last8 = t // SUBLANES - 1
    col = lambda off: pl.BlockSpec((tm, w), lambda i, h: (i, off // w + h))
    colp = lambda off: pl.BlockSpec((SUBLANES, w), lambda i, h: (jnp.maximum(i * r8 - 1, 0), off // w + h))
    coln = lambda off: pl.BlockSpec((SUBLANES, w), lambda i, h: (jnp.minimum((i + 1) * r8, last8), off // w + h))
    own = pl.BlockSpec((tm, w), lambda i, h: (i, h))
    kern = functools.partial(_prep_kernel, tm=tm, ntiles=ntiles)
    return pl.pallas_call(
        kern,
        out_shape=(jax.ShapeDtypeStruct((t, LRU_WIDTH), BF16), jax.ShapeDtypeStruct((t, SC_WIDTH), BF16)),
        grid=(ntiles, LRU_WIDTH // w),
        in_specs=[col(OFF_LG), own, own, col(OFF_SB),
                  col(OFF_SC), colp(OFF_SC), coln(OFF_SC),
                  col(OFF_SX), colp(OFF_SX), coln(OFF_SX),
                  pl.BlockSpec((3, w), lambda i, h: (0, h))],
        out_specs=(own, own),
        scratch_shapes=[pltpu.VMEM((tm + 2 * SUBLANES, w), F32)],
        compiler_params=_cparams("arbitrary", "arbitrary"),
        name="branch_prep",
    )(p, hf, hr, p, p, p, p, p, p, p, w3)


def _mergez_kernel(b0_ref, b1_ref, b2_ref, wb_ref, g0_ref, g1_ref, g2_ref, z_ref):
    acc = None
    for j, (b_ref, g_ref) in enumerate(((b0_ref, g0_ref), (b1_ref, g1_ref), (b2_ref, g2_ref))):
        t = jax.nn.sigmoid(g_ref[...]) * jnp.dot(b_ref[...], wb_ref[j], preferred_element_type=F32)
        acc = t if acc is None else acc + t
    z_ref[...] = acc.astype(z_ref.dtype)


def _mergez_call(y_att, y_lru, y_sc, wb_bf, p, *, tm):
    t = p.shape[0]
    tn = 512
    br = pl.BlockSpec((tm, BRANCH_W), lambda i, j: (i, 0))
    gate = lambda k: pl.BlockSpec((tm, tn), lambda i, j: (i, (OFF_GATE + k * D_MODEL) // tn + j))
    return pl.pallas_call(
        _mergez_kernel,
        out_shape=jax.ShapeDtypeStruct((t, D_MODEL), BF16),
        grid=(t // tm, D_MODEL // tn),
        in_specs=[br, br, br,
                  pl.BlockSpec((N_BRANCH, BRANCH_W, tn), lambda i, j: (0, 0, j)),
                  gate(0), gate(1), gate(2)],
        out_specs=pl.BlockSpec((tm, tn), lambda i, j: (i, j)),
        compiler_params=_cparams("arbitrary", "arbitrary"),
        name="branch_merge",
    )(y_att, y_lru, y_sc, wb_bf, p, p, p)


def _outnorm_kernel(z_ref, w_ref, x_ref, gate_ref, g_ref, b_ref, o_ref):
    y = jnp.dot(z_ref[...], w_ref[...], preferred_element_type=F32)
    r = ALPHA * x_ref[...] + gate_ref[...] * y
    o_ref[...] = _layer_norm(r) * g_ref[...] + b_ref[...]


def _outnorm_call(z, w_bf, x, mod, gate_chunk, g, b):
    t = z.shape[0]
    tm = 256
    vec = pl.BlockSpec((1, D_MODEL), lambda i: (0, 0))
    return pl.pallas_call(
        _outnorm_kernel,
        out_shape=jax.ShapeDtypeStruct((t, D_MODEL), F32),
        grid=(t // tm,),
        in_specs=[pl.BlockSpec((tm, D_MODEL), lambda i: (i, 0)),
                  pl.BlockSpec((D_MODEL, D_MODEL), lambda i: (0, 0)),
                  pl.BlockSpec((tm, D_MODEL), lambda i: (i, 0)),
                  pl.BlockSpec((1, D_MODEL), lambda i: (0, gate_chunk)),
                  vec, vec],
        out_specs=pl.BlockSpec((tm, D_MODEL), lambda i: (i, 0)),
        compiler_params=_cparams("arbitrary"),
        name="out_proj_norm",
    )(z, w_bf, x, mod, g.reshape(1, D_MODEL), b.reshape(1, D_MODEL))


def _router_kernel(x_ref, w_ref, b_ref, o_ref):
    logits = jnp.dot(x_ref[...].astype(BF16), w_ref[...], preferred_element_type=F32) + b_ref[...]
    lane = lax.broadcasted_iota(jnp.int32, logits.shape, 1)
    neg = -jnp.inf
    is_grp = lane < N_GROUPS
    gl = jnp.where(is_grp, logits, neg)
    gmax = jnp.max(gl, axis=-1, keepdims=True)
    g_idx = jnp.min(jnp.where(gl == gmax, lane, LANES), axis=-1, keepdims=True)
    gsum = jnp.sum(jnp.where(is_grp, jnp.exp(gl - gmax), 0.0), axis=-1, keepdims=True)
    g_w = 1.0 / gsum
    lo = N_GROUPS + g_idx * EXPERTS_PER_GROUP
    in_grp = jnp.where(lane >= lo, 1, 0) * jnp.where(lane < lo + EXPERTS_PER_GROUP, 1, 0)
    sel = jnp.where(in_grp > 0, logits, neg)
    v0 = jnp.max(sel, axis=-1, keepdims=True)
    i0 = jnp.min(jnp.where(sel == v0, lane, LANES), axis=-1, keepdims=True)
    sel1 = jnp.where(lane == i0, neg, sel)
    v1 = jnp.max(sel1, axis=-1, keepdims=True)
    i1 = jnp.min(jnp.where(sel1 == v1, lane, LANES), axis=-1, keepdims=True)
    e1 = jnp.exp(v1 - v0)
    den = 1.0 + e1
    w0 = (1.0 / den) * g_w
    w1 = (e1 / den) * g_w
    out = jnp.where(lane == 0, (i0 - N_GROUPS).astype(F32),
                    jnp.where(lane == 1, (i1 - N_GROUPS).astype(F32),
                              jnp.where(lane == 2, w0, jnp.where(lane == 3, w1, 0.0))))
    o_ref[...] = out


def _router_call(xf, wr_bf, br):
    t = xf.shape[0]
    tm = 256
    return pl.pallas_call(
        _router_kernel,
        out_shape=jax.ShapeDtypeStruct((t, LANES), F32),
        grid=(t // tm,),
        in_specs=[pl.BlockSpec((tm, D_MODEL), lambda i: (i, 0)),
                  pl.BlockSpec((D_MODEL, LANES), lambda i: (0, 0)),
                  pl.BlockSpec((1, LANES), lambda i: (0, 0))],
        out_specs=pl.BlockSpec((tm, LANES), lambda i: (i, 0)),
        compiler_params=_cparams("arbitrary"),
        name="moe_router",
    )(xf, wr_bf, br)


def _moe_kernel(te_ref, nt_ref, src_ref, x_hbm, wg_ref, wu_ref, wd_ref, y_hbm,
                xbuf, ybuf, gsem, ssem, *, tm):
    i = pl.program_id(0)
    nt = nt_ref[0]
    slot = i % 2

    def gather_copy(row, dst_slot, r):
        return pltpu.make_async_copy(x_hbm.at[pl.ds(row, 1), :], xbuf.at[dst_slot, pl.ds(r, 1), :],
                                     gsem.at[dst_slot])

    def scatter_copy(row, src_slot, r):
        return pltpu.make_async_copy(ybuf.at[src_slot, pl.ds(r, 1), :], y_hbm.at[pl.ds(row, 1), :],
                                     ssem.at[src_slot])

    def for_valid_rows(tile, fn):
        def body(r, c):
            pair = src_ref[tile * tm + r]

            @pl.when(pair >= 0)
            def _():
                fn(pair, r)
            return c
        lax.fori_loop(0, tm, body, 0)

    def start_gather(tile, s):
        for_valid_rows(tile, lambda pair, r: gather_copy(lax.shift_right_logical(pair, 1), s, r).start())

    def wait_gather(tile, s):
        for_valid_rows(tile, lambda pair, r: gather_copy(0, s, r).wait())

    def start_scatter(tile, s):
        for_valid_rows(tile, lambda pair, r: scatter_copy(pair, s, r).start())

    def wait_scatter(tile, s):
        for_valid_rows(tile, lambda pair, r: scatter_copy(0, s, r).wait())

    @pl.when(i == 0)
    def _():
        xbuf[...] = jnp.zeros(xbuf.shape, xbuf.dtype)
        start_gather(0, 0)

    @pl.when(i + 1 < nt)
    def _():
        start_gather(i + 1, 1 - slot)

    @pl.when(i < nt)
    def _():
        wait_gather(i, slot)

        @pl.when(i >= 2)
        def _():
            wait_scatter(i - 2, slot)

        x = xbuf[slot].astype(BF16)
        hg = jnp.dot(x, wg_ref[...], preferred_element_type=F32)
        hu = jnp.dot(x, wu_ref[...], preferred_element_type=F32)
        h = (jax.nn.silu(hg) * hu).astype(BF16)
        ybuf[slot] = jnp.dot(h, wd_ref[...], preferred_element_type=F32)
        start_scatter(i, slot)

        @pl.when(i == nt - 1)
        def _():
            @pl.when(i >= 1)
            def _():
                wait_scatter(i - 1, 1 - slot)
            wait_scatter(i, slot)


def _moe_call(te, nt, src, xf, wg_bf, wu_bf, wd_bf):
    n = xf.shape[0]
    ntiles = te.shape[0]
    kern = functools.partial(_moe_kernel, tm=MOE_TM)
    wspec = lambda shape: pl.BlockSpec((None,) + shape, lambda i, te, nt, src: (te[i], 0, 0))
    return pl.pallas_call(
        kern,
        out_shape=jax.ShapeDtypeStruct((2 * n, D_MODEL), F32),
        grid_spec=pltpu.PrefetchScalarGridSpec(
            num_scalar_prefetch=3,
            grid=(ntiles,),
            in_specs=[pl.BlockSpec(memory_space=pl.ANY),
                      wspec((D_MODEL, D_EXPERT)), wspec((D_MODEL, D_EXPERT)),
                      wspec((D_EXPERT, D_MODEL))],
            out_specs=pl.BlockSpec(memory_space=pl.ANY),
            scratch_shapes=[pltpu.VMEM((2, MOE_TM, D_MODEL), F32),
                            pltpu.VMEM((2, MOE_TM, D_MODEL), F32),
                            pltpu.SemaphoreType.DMA((2,)),
                            pltpu.SemaphoreType.DMA((2,))]),
        compiler_params=_cparams("arbitrary"),
        name="moe_experts",
    )(te, nt, src, xf, wg_bf, wu_bf, wd_bf)


def _moe_plan(eid, tm):
    npair = eid.size
    nslot = npair + N_EXPERTS * tm
    ntiles = nslot // tm
    flat = eid.reshape(-1)
    order = jnp.argsort(flat, stable=True).astype(jnp.int32)
    counts = jnp.sum((flat[:, None] == jnp.arange(N_EXPERTS, dtype=jnp.int32)[None, :]).astype(jnp.int32), axis=0)
    padded = ((counts + tm - 1) // tm) * tm
    pend = jnp.cumsum(padded)
    pstart = pend - padded
    start = jnp.cumsum(counts) - counts
    nt = pend[-1] // tm
    tile_start = jnp.arange(ntiles, dtype=jnp.int32) * tm
    te = jnp.sum((tile_start[:, None] >= pend[None, :]).astype(jnp.int32), axis=1)
    te = jnp.minimum(te, N_EXPERTS - 1)
    te = jnp.where(jnp.arange(ntiles) < nt, te, te[jnp.maximum(nt - 1, 0)])
    slot = jnp.arange(nslot, dtype=jnp.int32)
    es = te[slot // tm]
    j = slot - pstart[es]
    valid = (j < counts[es]) & (slot < pend[-1])
    pair = order[jnp.clip(start[es] + j, 0, npair - 1)]
    src = jnp.where(valid, pair, -1).astype(jnp.int32)
    return te.astype(jnp.int32), nt.astype(jnp.int32).reshape(1), src


def _combine_kernel(yp_ref, r_ref, x_ref, gate_ref, g_ref, b_ref, o_ref):
    w0 = r_ref[:, 2:3]
    w1 = r_ref[:, 3:4]
    m = w0 * yp_ref[:, 0:D_MODEL] + w1 * yp_ref[:, D_MODEL:2 * D_MODEL]
    r = ALPHA * x_ref[...] + gate_ref[...] * m
    o_ref[...] = _layer_norm(r) * g_ref[...] + b_ref[...]


def _combine_call(ypair2, route, x, mod, gate_chunk, g, b, row_block0):
    t = x.shape[0]
    tm = 256
    vec = pl.BlockSpec((1, D_MODEL), lambda i: (0, 0))
    return pl.pallas_call(
        _combine_kernel,
        out_shape=jax.ShapeDtypeStruct((t, D_MODEL), F32),
        grid=(t // tm,),
        in_specs=[pl.BlockSpec((tm, 2 * D_MODEL), lambda i: (i + row_block0, 0)),
                  pl.BlockSpec((tm, LANES), lambda i: (i + row_block0, 0)),
                  pl.BlockSpec((tm, D_MODEL), lambda i: (i, 0)),
                  pl.BlockSpec((1, D_MODEL), lambda i: (0, gate_chunk)),
                  vec, vec],
        out_specs=pl.BlockSpec((tm, D_MODEL), lambda i: (i, 0)),
        compiler_params=_cparams("arbitrary"),
        name="moe_combine_norm",
    )(ypair2, route, x, mod, g.reshape(1, D_MODEL), b.reshape(1, D_MODEL))


def _rope_tables(s):
    t = jnp.arange(s, dtype=jnp.int32)
    rows = (t // GRID_W).astype(F32)
    cols = (t % GRID_W).astype(F32)
    half = HEAD_DIM // 2
    inv = ROPE_THETA ** (-jnp.arange(0, half, 2, dtype=F32) / half)
    ar = rows[:, None] * inv[None, :]
    ac = cols[:, None] * inv[None, :]
    cos = jnp.concatenate([jnp.cos(ar), jnp.cos(ar), jnp.cos(ac), jnp.cos(ac)], axis=-1)
    sin_signed = jnp.concatenate([-jnp.sin(ar), jnp.sin(ar), -jnp.sin(ac), jnp.sin(ac)], axis=-1)
    return cos, sin_signed


def _moe(xf, w_grp, b_grp, w_exp, b_exp, wg_bf, wu_bf, wd_bf):
    n = xf.shape[0]
    pad = LANES - N_GROUPS - N_EXPERTS
    wr = jnp.concatenate([w_grp, w_exp, jnp.zeros((D_MODEL, pad), F32)], axis=1).astype(BF16)
    br = jnp.concatenate([b_grp, b_exp, jnp.zeros((pad,), F32)]).reshape(1, LANES)
    route = _router_call(xf, wr, br)
    eid = route[:, 0:2].astype(jnp.int32)
    te, nt, src = _moe_plan(eid, MOE_TM)
    ypair = _moe_call(te, nt, src, xf, wg_bf, wu_bf, wd_bf)
    return ypair.reshape(n, 2 * D_MODEL), route


def kernel(x, c, ctx, c_ctx, w_mod, b_mod, w_in, b_in, attn_sink, lru_conv_w, lru_conv_b,
           lru_w_a, lru_b_a, lru_w_x, lru_b_x, lru_lam, sc_conv_w, w_branch, w_out,
           ln1_g, ln1_b, ln2_g, ln2_b, w_grp, b_grp, w_exp, b_exp, w_gate, w_up, w_down):
    assert x.shape == (1, SEQ, D_MODEL) and ctx.shape == (1, CTX_LEN, D_MODEL)
    xs = x[0]
    xc = ctx[0]
    cos, sin_signed = _rope_tables(SEQ)
    c8 = jnp.concatenate([c, c_ctx[None, :], jnp.zeros((SUBLANES - 2, D_MODEL), F32)], axis=0)
    zero_h0 = jnp.zeros((2, LRU_WIDTH), F32)

    for l in range(DEPTH):
        last = l == DEPTH - 1
        w_in_bf = w_in[l].astype(BF16)
        wb_bf = w_branch[l].astype(BF16)
        wo_bf = w_out[l].astype(BF16)
        wg_bf = w_gate[l].astype(BF16)
        wu_bf = w_up[l].astype(BF16)
        wd_bf = w_down[l].astype(BF16)

        mods = _mod_call(c8, w_mod[l], b_mod[l])
        m_lat = mods[0:1]
        m_ctx = mods[1:2]

        u = _lnmod_call(xs, m_lat, 0, 1, BF16)
        u_ctx = _lnmod_call(xc, m_ctx, 0, 1, BF16)
        p = _proj_call(u, w_in_bf, b_in[l], cos, sin_signed, rope=True, tm=1024)
        p_ctx = _proj_call(u_ctx, w_in_bf, b_in[l], cos, sin_signed, rope=False, tm=CTX_LEN)

        y_att = _attn_call(p, p_ctx, attn_sink[l])
        lru_args = (lru_conv_w[l], lru_conv_b[l], lru_w_a[l], lru_b_a[l], lru_w_x[l], lru_b_x[l], lru_lam[l])
        hcf, hcr = _lru_call(p_ctx, zero_h0, *lru_args, tc=CTX_LEN)
        h0 = jnp.concatenate([hcf[CTX_LEN - 1:CTX_LEN], hcr[0:1]], axis=0)
        hf, hr = _lru_call(p, h0, *lru_args, tc=256)
        y_lru, y_sc = _prep_call(p, hf, hr, sc_conv_w[l], tm=256)
        z = _mergez_call(y_att, y_lru, y_sc, wb_bf, p, tm=1024)
        xs = _outnorm_call(z, wo_bf, xs, m_lat, 2, ln1_g[l], ln1_b[l])

        if last:
            xf = _lnmod_call(xs, m_lat, 3, 4, F32)
            ypair2, route = _moe(xf, w_grp[l], b_grp[l], w_exp[l], b_exp[l], wg_bf, wu_bf, wd_bf)
            xs = _combine_call(ypair2, route, xs, m_lat, 5, ln2_g[l], ln2_b[l], 0)
        else:
            yc_att = _ctx_attn_call(p_ctx, attn_sink[l])
            yc_lru, yc_sc = _prep_call(p_ctx, hcf, hcr, sc_conv_w[l], tm=CTX_LEN)
            zc = _mergez_call(yc_att, yc_lru, yc_sc, wb_bf, p_ctx, tm=CTX_LEN)
            xc = _outnorm_call(zc, wo_bf, xc, m_ctx, 2, ln1_g[l], ln1_b[l])
            xf = jnp.concatenate([_lnmod_call(xs, m_lat, 3, 4, F32),
                                  _lnmod_call(xc, m_ctx, 3, 4, F32)], axis=0)
            ypair2, route = _moe(xf, w_grp[l], b_grp[l], w_exp[l], b_exp[l], wg_bf, wu_bf, wd_bf)
            xs_new = _combine_call(ypair2, route, xs, m_lat, 5, ln2_g[l], ln2_b[l], 0)
            xc = _combine_call(ypair2, route, xc, m_ctx, 5, ln2_g[l], ln2_b[l], SEQ // 256)
            xs = xs_new
    return xs[None]
```

```python
import functools

import jax
import jax.numpy as jnp
from jax import lax
from jax.experimental import pallas as pl
from jax.experimental.pallas import tpu as pltpu

F32 = jnp.float32
BF16 = jnp.bfloat16

D_MODEL = 2048
SEQ = 8192
CTX_LEN = 256
DEPTH = 2
GRID_W = 64

N_HEADS = 8
N_KV_HEADS = 2
HEAD_DIM = 128
Q_GROUPS = N_HEADS // N_KV_HEADS
WINDOW = 128
BLOCK = 128
ROPE_THETA = 10000.0
Q_SCALE = HEAD_DIM ** -0.5

LRU_WIDTH = 1024
LRU_BLOCKS = 8
LRU_BLOCK_DIM = 128
LRU_C = 8.0
SC_WIDTH = 1024
N_BRANCH = 3
BRANCH_W = 1024

Q_W = N_HEADS * HEAD_DIM
KV_W = N_KV_HEADS * HEAD_DIM
OFF_K = Q_W
OFF_V = OFF_K + KV_W
OFF_LX = OFF_V + KV_W
OFF_LG = OFF_LX + LRU_WIDTH
OFF_SB = OFF_LG + LRU_WIDTH
OFF_SC = OFF_SB + SC_WIDTH
OFF_SX = OFF_SC + SC_WIDTH
OFF_GATE = OFF_SX + SC_WIDTH
PROJ_W = OFF_GATE + N_BRANCH * D_MODEL

N_GROUPS = 4
EXPERTS_PER_GROUP = 8
N_EXPERTS = N_GROUPS * EXPERTS_PER_GROUP
D_EXPERT = 1024

ALPHA = (2 * DEPTH) ** 0.25
LN_EPS = 1e-5

VMEM_LIMIT_BYTES = 52 * 1024 * 1024
MOE_UP_VMEM_LIMIT_BYTES = 58 * 1024 * 1024
LANES = 128
SUBLANES = 8

HALF_W = 512
PROJ_TN = 1280
MOE_TM = 256


def _cparams(*sem):
    return pltpu.CompilerParams(dimension_semantics=sem, vmem_limit_bytes=VMEM_LIMIT_BYTES)


def _layer_norm(x):
    mu = jnp.mean(x, axis=-1, keepdims=True)
    xc = x - mu
    var = jnp.mean(xc * xc, axis=-1, keepdims=True)
    return xc * lax.rsqrt(var + LN_EPS)


def _mod_kernel(c_ref, w_ref, b_ref, o_ref):
    a = jax.nn.silu(c_ref[...]).astype(BF16)
    o_ref[...] = jnp.dot(a, w_ref[...].astype(BF16), preferred_element_type=F32) + b_ref[...]


def _mod_call(c8, w_all, b, layer):
    n = w_all.shape[2]
    tn = 1024
    return pl.pallas_call(
        _mod_kernel,
        out_shape=jax.ShapeDtypeStruct((SUBLANES, n), F32),
        grid=(n // tn,),
        in_specs=[pl.BlockSpec((SUBLANES, D_MODEL), lambda j: (0, 0)),
                  pl.BlockSpec((None, D_MODEL, tn), lambda j: (layer, 0, j)),
                  pl.BlockSpec((1, tn), lambda j: (0, j))],
        out_specs=pl.BlockSpec((SUBLANES, tn), lambda j: (0, j)),
        compiler_params=_cparams("arbitrary"),
        name="mod_vectors",
    )(c8, w_all, b.reshape(1, n))


def _lnmod_kernel(x_ref, sh_ref, sc_ref, o_ref):
    y = _layer_norm(x_ref[...])
    o_ref[...] = (y * (1.0 + sc_ref[...]) + sh_ref[...]).astype(o_ref.dtype)


def _lnmod_call(x, mod, shift_chunk, scale_chunk, out_dtype):
    t = x.shape[0]
    tm = 256
    return pl.pallas_call(
        _lnmod_kernel,
        out_shape=jax.ShapeDtypeStruct((t, D_MODEL), out_dtype),
        grid=(t // tm,),
        in_specs=[pl.BlockSpec((tm, D_MODEL), lambda i: (i, 0)),
                  pl.BlockSpec((1, D_MODEL), lambda i: (0, shift_chunk)),
                  pl.BlockSpec((1, D_MODEL), lambda i: (0, scale_chunk))],
        out_specs=pl.BlockSpec((tm, D_MODEL), lambda i: (i, 0)),
        compiler_params=_cparams("arbitrary"),
        name="ln_modulate",
    )(x, mod, mod)


def _lnmod_pair_kernel(xa_ref, xb_ref, sha_ref, sca_ref, shb_ref, scb_ref, o_ref, *, na):
    i = pl.program_id(0)

    @pl.when(i < na)
    def _():
        _lnmod_kernel(xa_ref, sha_ref, sca_ref, o_ref)

    @pl.when(i >= na)
    def _():
        _lnmod_kernel(xb_ref, shb_ref, scb_ref, o_ref)


def _lnmod_pair_call(xa, mod_a, xb, mod_b, shift_chunk, scale_chunk, out_dtype):
    tm = 256
    na = xa.shape[0] // tm
    nb = xb.shape[0] // tm
    vec = lambda chunk: pl.BlockSpec((1, D_MODEL), lambda i: (0, chunk))
    return pl.pallas_call(
        functools.partial(_lnmod_pair_kernel, na=na),
        out_shape=jax.ShapeDtypeStruct(((na + nb) * tm, D_MODEL), out_dtype),
        grid=(na + nb,),
        in_specs=[pl.BlockSpec((tm, D_MODEL), lambda i: (jnp.minimum(i, na - 1), 0)),
                  pl.BlockSpec((tm, D_MODEL), lambda i: (jnp.maximum(i - na, 0), 0)),
                  vec(shift_chunk), vec(scale_chunk), vec(shift_chunk), vec(scale_chunk)],
        out_specs=pl.BlockSpec((tm, D_MODEL), lambda i: (i, 0)),
        compiler_params=_cparams("arbitrary"),
        name="ln_modulate_pair",
    )(xa, xb, mod_a, mod_a, mod_b, mod_b)


def _rope_head(xh, cos, sin_signed, first_half):
    partner = jnp.where(first_half, pltpu.roll(xh, LANES - 32, 1), pltpu.roll(xh, 32, 1))
    return xh * cos + partner * sin_signed


def _proj_kernel(u_ref, w_ref, b_ref, cos_ref, sin_ref, o_ref, w_scr, *, rope):
    j = pl.program_id(0)
    i = pl.program_id(1)

    @pl.when(i == 0)
    def _():
        w_scr[...] = w_ref[...].astype(BF16)

    acc = jnp.dot(u_ref[...], w_scr[...], preferred_element_type=F32) + b_ref[...]
    if not rope:
        o_ref[...] = acc
        return

    @pl.when(j != 0)
    def _():
        o_ref[...] = acc

    @pl.when(j == 0)
    def _():
        cos = cos_ref[...]
        sin_signed = sin_ref[...]
        lane = lax.broadcasted_iota(jnp.int32, cos.shape, 1)
        first_half = (lane & 63) < 32
        for h in range(N_HEADS + N_KV_HEADS):
            sl = slice(h * HEAD_DIM, (h + 1) * HEAD_DIM)
            r = _rope_head(acc[:, sl], cos, sin_signed, first_half)
            if h < N_HEADS:
                r = r * Q_SCALE
            o_ref[:, sl] = r


def _proj_call(u, w_all, b, cos, sin_signed, layer, *, rope, tm):
    t = u.shape[0]
    n = w_all.shape[2]
    kern = functools.partial(_proj_kernel, rope=rope)
    return pl.pallas_call(
        kern,
        out_shape=jax.ShapeDtypeStruct((t, n), F32),
        grid=(n // PROJ_TN, t // tm),
        in_specs=[pl.BlockSpec((tm, D_MODEL), lambda j, i: (i, 0)),
                  pl.BlockSpec((None, D_MODEL, PROJ_TN), lambda j, i: (layer, 0, j),
                               pipeline_mode=pl.Buffered(1)),
                  pl.BlockSpec((1, PROJ_TN), lambda j, i: (0, j)),
                  pl.BlockSpec((tm, HEAD_DIM), lambda j, i: (i, 0)),
                  pl.BlockSpec((tm, HEAD_DIM), lambda j, i: (i, 0))],
        out_specs=pl.BlockSpec((tm, PROJ_TN), lambda j, i: (i, j)),
        scratch_shapes=[pltpu.VMEM((D_MODEL, PROJ_TN), BF16)],
        compiler_params=_cparams("arbitrary", "arbitrary"),
        name="in_proj_rope" if rope else "in_proj_ctx",
    )(u, w_all, b.reshape(1, n), cos, sin_signed)


def _softmax_pv(q, k_all, v_all, bias, sink_col):
    s = lax.dot_general(q, k_all, (((1,), (1,)), ((), ())), preferred_element_type=F32)
    if bias is not None:
        s = s + bias
    m = jnp.maximum(jnp.max(s, axis=-1, keepdims=True), sink_col)
    e = jnp.exp(s - m)
    den = jnp.sum(e, axis=-1, keepdims=True) + jnp.exp(sink_col - m)
    p = (e / den).astype(BF16)
    return jnp.dot(p, v_all, preferred_element_type=F32)


def _sink_column(sink_ref, kv, rows):
    row = lax.broadcasted_iota(jnp.int32, (Q_GROUPS * rows, 1), 0)
    col = jnp.full((Q_GROUPS * rows, 1), sink_ref[kv * Q_GROUPS + Q_GROUPS - 1], F32)
    for g in range(Q_GROUPS - 2, -1, -1):
        col = jnp.where(row < (g + 1) * rows, sink_ref[kv * Q_GROUPS + g], col)
    return col


def _attn_kernel(sink_ref, q_ref, kp_ref, kc_ref, kn_ref, vp_ref, vc_ref, vn_ref,
                 kx_ref, vx_ref, o_ref, *, seq):
    n = pl.program_id(0)
    n_loc = 3 * BLOCK
    n_keys = n_loc + CTX_LEN
    qi = lax.broadcasted_iota(jnp.int32, (BLOCK, n_keys), 0)
    kj = lax.broadcasted_iota(jnp.int32, (BLOCK, n_keys), 1)
    rel = kj - BLOCK - qi
    kpos = n * BLOCK - BLOCK + kj
    bad = jnp.where(kj < n_loc,
                    jnp.where(jnp.abs(rel) > WINDOW, 1, 0) + jnp.where(kpos < 0, 1, 0)
                    + jnp.where(kpos >= seq, 1, 0), 0)
    bias1 = jnp.where(bad > 0, -jnp.inf, 0.0).astype(F32)
    bias = jnp.concatenate([bias1] * Q_GROUPS, axis=0)
    for kv in range(N_KV_HEADS):
        ks = slice(kv * HEAD_DIM, (kv + 1) * HEAD_DIM)
        k_all = jnp.concatenate([kp_ref[:, ks], kc_ref[:, ks], kn_ref[:, ks], kx_ref[:, ks]],
                                axis=0).astype(BF16)
        v_all = jnp.concatenate([vp_ref[:, ks], vc_ref[:, ks], vn_ref[:, ks], vx_ref[:, ks]],
                                axis=0).astype(BF16)
        q = jnp.concatenate(
            [q_ref[:, (kv * Q_GROUPS + g) * HEAD_DIM:(kv * Q_GROUPS + g + 1) * HEAD_DIM]
             for g in range(Q_GROUPS)], axis=0).astype(BF16)
        o = _softmax_pv(q, k_all, v_all, bias, _sink_column(sink_ref, kv, BLOCK))
        for g in range(Q_GROUPS):
            h = kv * Q_GROUPS + g
            o_ref[:, h * HEAD_DIM:(h + 1) * HEAD_DIM] = o[g * BLOCK:(g + 1) * BLOCK].astype(o_ref.dtype)


def _attn_call(p, p_ctx, sink):
    s = p.shape[0]
    nb = s // BLOCK
    kcol = OFF_K // KV_W
    vcol = OFF_V // KV_W
    kv_spec = lambda f, c: pl.BlockSpec((BLOCK, KV_W), lambda n: (f(n), c))
    prev = lambda n: jnp.maximum(n - 1, 0)
    cur = lambda n: n
    nxt = lambda n: jnp.minimum(n + 1, nb - 1)
    return pl.pallas_call(
        functools.partial(_attn_kernel, seq=s),
        out_shape=jax.ShapeDtypeStruct((s, Q_W), BF16),
        grid=(nb,),
        in_specs=[pl.BlockSpec(memory_space=pltpu.SMEM),
                  pl.BlockSpec((BLOCK, Q_W), lambda n: (n, 0)),
                  kv_spec(prev, kcol), kv_spec(cur, kcol), kv_spec(nxt, kcol),
                  kv_spec(prev, vcol), kv_spec(cur, vcol), kv_spec(nxt, vcol),
                  pl.BlockSpec((CTX_LEN, KV_W), lambda n: (0, kcol)),
                  pl.BlockSpec((CTX_LEN, KV_W), lambda n: (0, vcol))],
        out_specs=pl.BlockSpec((BLOCK, Q_W), lambda n: (n, 0)),
        compiler_params=_cparams("arbitrary"),
        name="window_attention",
    )(sink, p, p, p, p, p, p, p, p_ctx, p_ctx)


def _ctx_attn_kernel(sink_ref, q_ref, k_ref, v_ref, o_ref):
    rows = q_ref.shape[0]
    for kv in range(N_KV_HEADS):
        ks = slice(kv * HEAD_DIM, (kv + 1) * HEAD_DIM)
        q = jnp.concatenate(
            [q_ref[:, (kv * Q_GROUPS + g) * HEAD_DIM:(kv * Q_GROUPS + g + 1) * HEAD_DIM] * Q_SCALE
             for g in range(Q_GROUPS)], axis=0).astype(BF16)
        o = _softmax_pv(q, k_ref[:, ks].astype(BF16), v_ref[:, ks].astype(BF16), None,
                        _sink_column(sink_ref, kv, rows))
        for g in range(Q_GROUPS):
            h = kv * Q_GROUPS + g
            o_ref[:, h * HEAD_DIM:(h + 1) * HEAD_DIM] = o[g * rows:(g + 1) * rows].astype(o_ref.dtype)


def _ctx_attn_call(p_ctx, sink):
    lc = p_ctx.shape[0]
    return pl.pallas_call(
        _ctx_attn_kernel,
        out_shape=jax.ShapeDtypeStruct((lc, Q_W), BF16),
        grid=(1,),
        in_specs=[pl.BlockSpec(memory_space=pltpu.SMEM),
                  pl.BlockSpec((lc, Q_W), lambda n: (0, 0)),
                  pl.BlockSpec((lc, KV_W), lambda n: (0, OFF_K // KV_W)),
                  pl.BlockSpec((lc, KV_W), lambda n: (0, OFF_V // KV_W))],
        out_specs=pl.BlockSpec((lc, Q_W), lambda n: (0, 0)),
        compiler_params=_cparams("arbitrary"),
        name="context_attention",
    )(sink, p_ctx, p_ctx, p_ctx)


def _lru_conv(ext_scr, cur_ref, prev_ref, next_ref, is_first, is_last, cw_ref, cb_ref, tc):
    ext_scr[0:SUBLANES, :] = jnp.where(is_first, 0.0, prev_ref[...])
    ext_scr[SUBLANES:SUBLANES + tc, :] = cur_ref[...]
    ext_scr[SUBLANES + tc:, :] = jnp.where(is_last, 0.0, next_ref[...])
    out = cb_ref[...] + cw_ref[0:1, :] * ext_scr[SUBLANES - 1:SUBLANES - 1 + tc, :]
    for j in range(1, 4):
        out = out + cw_ref[j:j + 1, :] * ext_scr[SUBLANES - 1 + j:SUBLANES - 1 + j + tc, :]
    return out


def _lru_gates(xb, d, wa_ref, ba_ref, wx_ref, bx_ref, lam_ref, a_scr, u_scr):
    xb_bf = xb.astype(BF16)
    for b in range(HALF_W // LRU_BLOCK_DIM):
        sl = slice(b * LRU_BLOCK_DIM, (b + 1) * LRU_BLOCK_DIM)
        xs = xb_bf[:, sl]
        r = jax.nn.sigmoid(jnp.dot(xs, wa_ref[d, b].astype(BF16), preferred_element_type=F32)
                           + ba_ref[d:d + 1, sl])
        ig = jax.nn.sigmoid(jnp.dot(xs, wx_ref[d, b].astype(BF16), preferred_element_type=F32)
                            + bx_ref[d:d + 1, sl])
        log_a = -LRU_C * r * jax.nn.softplus(-lam_ref[d:d + 1, sl])
        a_scr[d, :, sl] = jnp.exp(log_a)
        u_scr[d, :, sl] = jnp.sqrt(1.0 - jnp.exp(2.0 * log_a)) * ig * xb[:, sl]


def _lru_kernel(fc_ref, fp_ref, fn_ref, rc_ref, rp_ref, rn_ref,
                cw_ref, cb_ref, wa_ref, ba_ref, wx_ref, bx_ref, lam_ref, h0_ref,
                hf_ref, hr_ref, ext_scr, a_scr, u_scr, cf_scr, cr_scr, *, tc, nchunks):
    i = pl.program_id(1)
    w = HALF_W

    @pl.when(i == 0)
    def _():
        cf_scr[...] = jnp.broadcast_to(h0_ref[0:1, :], (SUBLANES, w))
        cr_scr[...] = jnp.broadcast_to(h0_ref[1:2, :], (SUBLANES, w))

    xb_f = _lru_conv(ext_scr, fc_ref, fp_ref, fn_ref, i == 0, i == nchunks - 1, cw_ref, cb_ref, tc)
    _lru_gates(xb_f, 0, wa_ref, ba_ref, wx_ref, bx_ref, lam_ref, a_scr, u_scr)
    xb_r = _lru_conv(ext_scr, rc_ref, rp_ref, rn_ref, i == nchunks - 1, i == 0, cw_ref, cb_ref, tc)
    _lru_gates(xb_r, 1, wa_ref, ba_ref, wx_ref, bx_ref, lam_ref, a_scr, u_scr)

    row = lax.broadcasted_iota(jnp.int32, (SUBLANES, w), 0)
    ntile = tc // SUBLANES

    def body(j, carry):
        cf, cr = carry
        off = pl.multiple_of(j * SUBLANES, SUBLANES)
        a = a_scr[0, pl.ds(off, SUBLANES), :]
        u = u_scr[0, pl.ds(off, SUBLANES), :]
        for s in (1, 2, 4):
            m = row >= s
            u = jnp.where(m, u + a * pltpu.roll(u, s, 0), u)
            a = jnp.where(m, a * pltpu.roll(a, s, 0), a)
        h = u + a * cf
        hf_ref[pl.ds(off, SUBLANES), :] = h
        cf = jnp.broadcast_to(h[SUBLANES - 1:SUBLANES, :], (SUBLANES, w))
        offr = pl.multiple_of((ntile - 1 - j) * SUBLANES, SUBLANES)
        a = a_scr[1, pl.ds(offr, SUBLANES), :]
        u = u_scr[1, pl.ds(offr, SUBLANES), :]
        for s in (1, 2, 4):
            m = row < SUBLANES - s
            u = jnp.where(m, u + a * pltpu.roll(u, SUBLANES - s, 0), u)
            a = jnp.where(m, a * pltpu.roll(a, SUBLANES - s, 0), a)
        h = u + a * cr
        hr_ref[pl.ds(offr, SUBLANES), :] = h
        cr = jnp.broadcast_to(h[0:1, :], (SUBLANES, w))
        return cf, cr

    cf, cr = lax.fori_loop(0, ntile, body, (cf_scr[...], cr_scr[...]))
    cf_scr[...] = cf
    cr_scr[...] = cr


def _lru_call(p, h0, cw, cb, wa, ba, wx, bx, lam, *, tc):
    t = p.shape[0]
    nchunks = t // tc
    w = HALF_W
    c0 = OFF_LX // w
    r8 = tc // SUBLANES
    last8 = t // SUBLANES - 1
    fwd = lambda i: i
    rev = lambda i: nchunks - 1 - i
    cur_spec = lambda f: pl.BlockSpec((tc, w), lambda h, i: (f(i), c0 + h))
    prev_spec = lambda f: pl.BlockSpec((SUBLANES, w), lambda h, i: (jnp.maximum(f(i) * r8 - 1, 0), c0 + h))
    next_spec = lambda f: pl.BlockSpec((SUBLANES, w), lambda h, i: (jnp.minimum((f(i) + 1) * r8, last8), c0 + h))
    vec_spec = lambda rows: pl.BlockSpec((rows, w), lambda h, i: (0, h))
    gate_w_spec = pl.BlockSpec((2, w // LRU_BLOCK_DIM, LRU_BLOCK_DIM, LRU_BLOCK_DIM),
                               lambda h, i: (0, h, 0, 0))
    kern = functools.partial(_lru_kernel, tc=tc, nchunks=nchunks)
    return pl.pallas_call(
        kern,
        out_shape=(jax.ShapeDtypeStruct((t, LRU_WIDTH), F32), jax.ShapeDtypeStruct((t, LRU_WIDTH), F32)),
        grid=(LRU_WIDTH // w, nchunks),
        in_specs=[cur_spec(fwd), prev_spec(fwd), next_spec(fwd),
                  cur_spec(rev), prev_spec(rev), next_spec(rev),
                  vec_spec(4), vec_spec(1), gate_w_spec, vec_spec(2), gate_w_spec, vec_spec(2),
                  vec_spec(2), vec_spec(2)],
        out_specs=(pl.BlockSpec((tc, w), lambda h, i: (i, h)),
                   pl.BlockSpec((tc, w), lambda h, i: (nchunks - 1 - i, h))),
        scratch_shapes=[pltpu.VMEM((tc + 2 * SUBLANES, w), F32),
                        pltpu.VMEM((2, tc, w), F32), pltpu.VMEM((2, tc, w), F32),
                        pltpu.VMEM((SUBLANES, w), F32), pltpu.VMEM((SUBLANES, w), F32)],
        compiler_params=_cparams("arbitrary", "arbitrary"),
        name="rg_lru",
    )(p, p, p, p, p, p, cw, cb.reshape(1, LRU_WIDTH), wa, ba.reshape(2, LRU_WIDTH),
      wx, bx.reshape(2, LRU_WIDTH), lam, h0)


def _prep_kernel(lg_ref, hf_ref, hr_ref, sb_ref, scc_ref, scp_ref, scn_ref,
                 sxc_ref, sxp_ref, sxn_ref, w3_ref, ylru_ref, ysc_ref, ext_scr, *, tm, ntiles):
    i = pl.program_id(0)
    ylru_ref[...] = ((hf_ref[...] + hr_ref[...]) * jax.nn.gelu(lg_ref[...])).astype(ylru_ref.dtype)
    ext_scr[0:SUBLANES, :] = jnp.where(i == 0, 0.0, scp_ref[...] * sxp_ref[...])
    ext_scr[SUBLANES:SUBLANES + tm, :] = scc_ref[...] * sxc_ref[...]
    ext_scr[SUBLANES + tm:, :] = jnp.where(i == ntiles - 1, 0.0, scn_ref[...] * sxn_ref[...])
    conv = w3_ref[0:1, :] * ext_scr[SUBLANES - 1:SUBLANES - 1 + tm, :]
    for j in range(1, 3):
        conv = conv + w3_ref[j:j + 1, :] * ext_scr[SUBLANES - 1 + j:SUBLANES - 1 + j + tm, :]
    ysc_ref[...] = (sb_ref[...] * conv).astype(ysc_ref.dtype)


def _prep_call(p, hf, hr, w3, *, tm):
    t = p.shape[0]
    ntiles = t // tm
    w = HALF_W
    r8 = tm // SUBLANES
    last8 = t // SUBLANES - 1
    col = lambda off: pl.BlockSpec((tm, w), lambda i, h: (i, off // w + h))
    colp = lambda off: pl.BlockSpec((SUBLANES, w), lambda i, h: (jnp.maximum(i * r8 - 1, 0), off // w + h))
    coln = lambda off: pl.BlockSpec((SUBLANES, w), lambda i, h: (jnp.minimum((i + 1) * r8, last8), off // w + h))
    own = pl.BlockSpec((tm, w), lambda i, h: (i, h))
    kern = functools.partial(_prep_kernel, tm=tm, ntiles=ntiles)
    return pl.pallas_call(
        kern,
        out_shape=(jax.ShapeDtypeStruct((t, LRU_WIDTH), BF16), jax.ShapeDtypeStruct((t, SC_WIDTH), BF16)),
        grid=(ntiles, LRU_WIDTH // w),
        in_specs=[col(OFF_LG), own, own, col(OFF_SB),
                  col(OFF_SC), colp(OFF_SC), coln(OFF_SC),
                  col(OFF_SX), colp(OFF_SX), coln(OFF_SX),
                  pl.BlockSpec((3, w), lambda i, h: (0, h))],
        out_specs=(own, own),
        scratch_shapes=[pltpu.VMEM((tm + 2 * SUBLANES, w), F32)],
        compiler_params=_cparams("arbitrary", "arbitrary"),
        name="branch_prep",
    )(p, hf, hr, p, p, p, p, p, p, p, w3)


def _mergez_kernel(b0_ref, b1_ref, b2_ref, wb_ref, g0_ref, g1_ref, g2_ref, z_ref):
    acc = None
    for j, (b_ref, g_ref) in enumerate(((b0_ref, g0_ref), (b1_ref, g1_ref), (b2_ref, g2_ref))):
        t = jax.nn.sigmoid(g_ref[...]) * jnp.dot(b_ref[...], wb_ref[j], preferred_element_type=F32)
        acc = t if acc is None else acc + t
    z_ref[...] = acc.astype(z_ref.dtype)


def _mergez_call(y_att, y_lru, y_sc, wb_bf, p, *, tm):
    t = p.shape[0]
    tn = 512
    br = pl.BlockSpec((tm, BRANCH_W), lambda i, j: (i, 0))
    gate = lambda k: pl.BlockSpec((tm, tn), lambda i, j: (i, (OFF_GATE + k * D_MODEL) // tn + j))
    return pl.pallas_call(
        _mergez_kernel,
        out_shape=jax.ShapeDtypeStruct((t, D_MODEL), BF16),
        grid=(t // tm, D_MODEL // tn),
        in_specs=[br, br, br,
                  pl.BlockSpec((N_BRANCH, BRANCH_W, tn), lambda i, j: (0, 0, j)),
                  gate(0), gate(1), gate(2)],
        out_specs=pl.BlockSpec((tm, tn), lambda i, j: (i, j)),
        compiler_params=_cparams("arbitrary", "arbitrary"),
        name="branch_merge",
    )(y_att, y_lru, y_sc, wb_bf, p, p, p)


def _outnorm_kernel(z_ref, w_ref, x_ref, gate_ref, g_ref, b_ref, o_ref):
    y = jnp.dot(z_ref[...], w_ref[...], preferred_element_type=F32)
    r = ALPHA * x_ref[...] + gate_ref[...] * y
    o_ref[...] = _layer_norm(r) * g_ref[...] + b_ref[...]


def _outnorm_call(z, w_bf, x, mod, gate_chunk, g, b):
    t = z.shape[0]
    tm = 256
    vec = pl.BlockSpec((1, D_MODEL), lambda i: (0, 0))
    return pl.pallas_call(
        _outnorm_kernel,
        out_shape=jax.ShapeDtypeStruct((t, D_MODEL), F32),
        grid=(t // tm,),
        in_specs=[pl.BlockSpec((tm, D_MODEL), lambda i: (i, 0)),
                  pl.BlockSpec((D_MODEL, D_MODEL), lambda i: (0, 0)),
                  pl.BlockSpec((tm, D_MODEL), lambda i: (i, 0)),
                  pl.BlockSpec((1, D_MODEL), lambda i: (0, gate_chunk)),
                  vec, vec],
        out_specs=pl.BlockSpec((tm, D_MODEL), lambda i: (i, 0)),
        compiler_params=_cparams("arbitrary"),
        name="out_proj_norm",
    )(z, w_bf, x, mod, g.reshape(1, D_MODEL), b.reshape(1, D_MODEL))


def _router_kernel(x_ref, w_ref, b_ref, o_ref):
    logits = jnp.dot(x_ref[...].astype(BF16), w_ref[...], preferred_element_type=F32) + b_ref[...]
    lane = lax.broadcasted_iota(jnp.int32, logits.shape, 1)
    neg = -jnp.inf
    is_grp = lane < N_GROUPS
    gl = jnp.where(is_grp, logits, neg)
    gmax = jnp.max(gl, axis=-1, keepdims=True)
    g_idx = jnp.min(jnp.where(gl == gmax, lane, LANES), axis=-1, keepdims=True)
    gsum = jnp.sum(jnp.where(is_grp, jnp.exp(gl - gmax), 0.0), axis=-1, keepdims=True)
    g_w = 1.0 / gsum
    lo = N_GROUPS + g_idx * EXPERTS_PER_GROUP
    in_grp = jnp.where(lane >= lo, 1, 0) * jnp.where(lane < lo + EXPERTS_PER_GROUP, 1, 0)
    sel = jnp.where(in_grp > 0, logits, neg)
    v0 = jnp.max(sel, axis=-1, keepdims=True)
    i0 = jnp.min(jnp.where(sel == v0, lane, LANES), axis=-1, keepdims=True)
    sel1 = jnp.where(lane == i0, neg, sel)
    v1 = jnp.max(sel1, axis=-1, keepdims=True)
    i1 = jnp.min(jnp.where(sel1 == v1, lane, LANES), axis=-1, keepdims=True)
    e1 = jnp.exp(v1 - v0)
    den = 1.0 + e1
    w0 = (1.0 / den) * g_w
    w1 = (e1 / den) * g_w
    out = jnp.where(lane == 0, (i0 - N_GROUPS).astype(F32),
                    jnp.where(lane == 1, (i1 - N_GROUPS).astype(F32),
                              jnp.where(lane == 2, w0, jnp.where(lane == 3, w1, 0.0))))
    o_ref[...] = out


def _router_call(xf, wr_bf, br):
    t = xf.shape[0]
    tm = 256
    return pl.pallas_call(
        _router_kernel,
        out_shape=jax.ShapeDtypeStruct((t, LANES), F32),
        grid=(t // tm,),
        in_specs=[pl.BlockSpec((tm, D_MODEL), lambda i: (i, 0)),
                  pl.BlockSpec((D_MODEL, LANES), lambda i: (0, 0)),
                  pl.BlockSpec((1, LANES), lambda i: (0, 0))],
        out_specs=pl.BlockSpec((tm, LANES), lambda i: (i, 0)),
        compiler_params=_cparams("arbitrary"),
        name="moe_router",
    )(xf, wr_bf, br)


MOE_UNROLL = 8


def _tile_rows(q0_ref, cnt_ref, order_ref, tile, fn, static_rows):
    base = q0_ref[tile]
    cnt = cnt_ref[tile]
    if static_rows:
        for r in range(MOE_TM):
            fn(r, order_ref[base + r], r < cnt)
        return

    def body(blk, c):
        for k in range(MOE_UNROLL):
            r = blk * MOE_UNROLL + k
            fn(r, order_ref[base + r], r < cnt)
        return c
    lax.fori_loop(0, MOE_TM // MOE_UNROLL, body, 0)


def _is_new_expert(te_ref, i):
    return jnp.logical_or(i == 0, te_ref[i] != te_ref[jnp.maximum(i - 1, 0)])


def _moe_up_kernel(te_ref, q0_ref, cnt_ref, nt_ref, order_ref, x_hbm, wg_ref, wu_ref, h_ref,
                   xbuf, wg_scr, wu_scr, gsem):
    i = pl.program_id(0)
    nt = nt_ref[0]
    slot = i % 2

    def start_gather(tile, s, static_rows):
        def issue(r, pair, valid):
            tok = lax.shift_right_logical(pair, 1)
            pltpu.make_async_copy(x_hbm.at[pl.ds(tok, 1), :], xbuf.at[s, pl.ds(r, 1), :],
                                  gsem.at[s]).start()
        _tile_rows(q0_ref, cnt_ref, order_ref, tile, issue, static_rows)

    def wait_gather(s):
        pltpu.make_async_copy(x_hbm.at[pl.ds(0, MOE_TM), :], xbuf.at[s], gsem.at[s]).wait()

    @pl.when(i == 0)
    def _():
        start_gather(0, 0, False)

    @pl.when(_is_new_expert(te_ref, i))
    def _():
        wg_scr[...] = wg_ref[...].astype(BF16)
        wu_scr[...] = wu_ref[...].astype(BF16)

    @pl.when(i < nt)
    def _():
        wait_gather(slot)
        start_gather(jnp.minimum(i + 1, nt - 1), 1 - slot, True)
        x = xbuf[slot].astype(BF16)
        hg = jnp.dot(x, wg_scr[...], preferred_element_type=F32)
        hu = jnp.dot(x, wu_scr[...], preferred_element_type=F32)
        h_ref[...] = (jax.nn.silu(hg) * hu).astype(h_ref.dtype)

        @pl.when(i == nt - 1)
        def _():
            wait_gather(1 - slot)

    @pl.when(i >= nt)
    def _():
        h_ref[...] = jnp.zeros(h_ref.shape, h_ref.dtype)


def _moe_down_kernel(te_ref, q0_ref, cnt_ref, nt_ref, order_ref, h_ref, wd_ref, y_hbm,
                     ybuf, wd_scr, ssem, *, ntok):
    i = pl.program_id(0)
    nt = nt_ref[0]
    slot = i % 2

    def wait_scatter(s):
        pltpu.make_async_copy(ybuf.at[s], y_hbm.at[pl.ds(0, MOE_TM), pl.ds(0, D_MODEL)], ssem.at[s]).wait()

    def start_scatter(tile, s):
        def issue(r, pair, valid):
            dst = jnp.where(valid, pair, 2 * ntok + s * MOE_TM + r)
            tok = lax.shift_right_logical(dst, 1)
            col = pl.multiple_of((dst & 1) * D_MODEL, D_MODEL)
            pltpu.make_async_copy(ybuf.at[s, pl.ds(r, 1), :],
                                  y_hbm.at[pl.ds(tok, 1), pl.ds(col, D_MODEL)], ssem.at[s]).start()
        _tile_rows(q0_ref, cnt_ref, order_ref, tile, issue, True)

    @pl.when(i == 0)
    def _():
        ybuf[0] = jnp.zeros(ybuf.shape[1:], ybuf.dtype)
        for half in range(2):
            spare = pltpu.make_async_copy(
                ybuf.at[0], y_hbm.at[pl.ds(ntok, MOE_TM), pl.ds(half * D_MODEL, D_MODEL)], ssem.at[0])
            spare.start()
            spare.wait()

    @pl.when(_is_new_expert(te_ref, i))
    def _():
        wd_scr[...] = wd_ref[...].astype(BF16)

    @pl.when(i < nt)
    def _():
        @pl.when(i >= 2)
        def _():
            wait_scatter(slot)

        ybuf[slot] = jnp.dot(h_ref[...], wd_scr[...], preferred_element_type=F32)
        start_scatter(i, slot)

        @pl.when(i == nt - 1)
        def _():
            @pl.when(i >= 1)
            def _():
                wait_scatter(1 - slot)
            wait_scatter(slot)


def _moe_experts(plan, xf, w_gate, w_up, w_down, layer):
    te, q0, cnt, nt, order = plan
    n = xf.shape[0]
    ntiles = te.shape[0]
    wspec = lambda shape: pl.BlockSpec((None, None) + shape,
                                       lambda i, te, q0, cnt, nt, order: (layer, te[i], 0, 0))
    tile_spec = pl.BlockSpec((MOE_TM, D_EXPERT), lambda i, te, q0, cnt, nt, order: (i, 0))
    big_vmem = pltpu.CompilerParams(dimension_semantics=("arbitrary",),
                                    vmem_limit_bytes=MOE_UP_VMEM_LIMIT_BYTES)
    h = pl.pallas_call(
        _moe_up_kernel,
        out_shape=jax.ShapeDtypeStruct((ntiles * MOE_TM, D_EXPERT), BF16),
        grid_spec=pltpu.PrefetchScalarGridSpec(
            num_scalar_prefetch=5,
            grid=(ntiles,),
            in_specs=[pl.BlockSpec(memory_space=pl.ANY),
                      wspec((D_MODEL, D_EXPERT)), wspec((D_MODEL, D_EXPERT))],
            out_specs=tile_spec,
            scratch_shapes=[pltpu.VMEM((2, MOE_TM, D_MODEL), F32),
                            pltpu.VMEM((D_MODEL, D_EXPERT), BF16),
                            pltpu.VMEM((D_MODEL, D_EXPERT), BF16),
                            pltpu.SemaphoreType.DMA((2,))]),
        compiler_params=big_vmem,
        name="moe_gate_up",
    )(te, q0, cnt, nt, order, xf, w_gate, w_up)
    return pl.pallas_call(
        functools.partial(_moe_down_kernel, ntok=n),
        out_shape=jax.ShapeDtypeStruct((n + MOE_TM, 2 * D_MODEL), F32),
        grid_spec=pltpu.PrefetchScalarGridSpec(
            num_scalar_prefetch=5,
            grid=(ntiles,),
            in_specs=[tile_spec, wspec((D_EXPERT, D_MODEL))],
            out_specs=pl.BlockSpec(memory_space=pl.ANY),
            scratch_shapes=[pltpu.VMEM((2, MOE_TM, D_MODEL), F32),
                            pltpu.VMEM((D_EXPERT, D_MODEL), BF16),
                            pltpu.SemaphoreType.DMA((2,))]),
        compiler_params=_cparams("arbitrary"),
        name="moe_down",
    )(te, q0, cnt, nt, order, h, w_down)


def _lookup(table, idx):
    onehot = idx[:, None] == jnp.arange(table.shape[0], dtype=jnp.int32)[None, :]
    return jnp.sum(jnp.where(onehot, table[None, :], 0), axis=1)


def _moe_plan(eid, tm):
    npair = eid.size
    ntiles = npair // tm + N_EXPERTS
    flat = eid.reshape(-1)
    order = jnp.argsort(flat, stable=True).astype(jnp.int32)
    experts = jnp.arange(N_EXPERTS, dtype=jnp.int32)
    counts = jnp.sum((flat[:, None] == experts[None, :]).astype(jnp.int32), axis=0)
    padded = ((counts + tm - 1) // tm) * tm
    pend = jnp.cumsum(padded)
    pstart = pend - padded
    start = jnp.cumsum(counts) - counts
    nt = pend[-1] // tm
    tiles = jnp.arange(ntiles, dtype=jnp.int32)
    tile_start = tiles * tm
    te = jnp.minimum(jnp.sum((tile_start[:, None] >= pend[None, :]).astype(jnp.int32), axis=1), N_EXPERTS - 1)
    in_use = tiles < nt
    off = tile_start - _lookup(pstart, te)
    q0 = jnp.where(in_use, _lookup(start, te) + off, 0)
    cnt = jnp.where(in_use, jnp.clip(_lookup(counts, te) - off, 0, tm), 0)
    te_last = jnp.sum(jnp.where(tiles == nt - 1, te, 0))
    te = jnp.where(in_use, te, te_last)
    i32 = lambda a: a.astype(jnp.int32)
    order = jnp.concatenate([order, jnp.zeros((tm,), jnp.int32)])
    return i32(te), i32(q0), i32(cnt), i32(nt).reshape(1), order


def _combine_kernel(yp_ref, r_ref, x_ref, gate_ref, g_ref, b_ref, o_ref):
    w0 = r_ref[:, 2:3]
    w1 = r_ref[:, 3:4]
    m = w0 * yp_ref[:, 0:D_MODEL] + w1 * yp_ref[:, D_MODEL:2 * D_MODEL]
    r = ALPHA * x_ref[...] + gate_ref[...] * m
    o_ref[...] = _layer_norm(r) * g_ref[...] + b_ref[...]


def _combine_call(ypair2, route, x, mod, gate_chunk, g, b, row_block0):
    t = x.shape[0]
    tm = 256
    vec = pl.BlockSpec((1, D_MODEL), lambda i: (0, 0))
    return pl.pallas_call(
        _combine_kernel,
        out_shape=jax.ShapeDtypeStruct((t, D_MODEL), F32),
        grid=(t // tm,),
        in_specs=[pl.BlockSpec((tm, 2 * D_MODEL), lambda i: (i + row_block0, 0)),
                  pl.BlockSpec((tm, LANES), lambda i: (i + row_block0, 0)),
                  pl.BlockSpec((tm, D_MODEL), lambda i: (i, 0)),
                  pl.BlockSpec((1, D_MODEL), lambda i: (0, gate_chunk)),
                  vec, vec],
        out_specs=pl.BlockSpec((tm, D_MODEL), lambda i: (i, 0)),
        compiler_params=_cparams("arbitrary"),
        name="moe_combine_norm",
    )(ypair2, route, x, mod, g.reshape(1, D_MODEL), b.reshape(1, D_MODEL))


def _rope_tables(s):
    t = jnp.arange(s, dtype=jnp.int32)
    rows = (t // GRID_W).astype(F32)
    cols = (t % GRID_W).astype(F32)
    half = HEAD_DIM // 2
    inv = ROPE_THETA ** (-jnp.arange(0, half, 2, dtype=F32) / half)
    ar = rows[:, None] * inv[None, :]
    ac = cols[:, None] * inv[None, :]
    cos = jnp.concatenate([jnp.cos(ar), jnp.cos(ar), jnp.cos(ac), jnp.cos(ac)], axis=-1)
    sin_signed = jnp.concatenate([-jnp.sin(ar), jnp.sin(ar), -jnp.sin(ac), jnp.sin(ac)], axis=-1)
    return cos, sin_signed


def _moe(xf, w_grp, b_grp, w_exp, b_exp, w_gate, w_up, w_down, layer):
    pad = LANES - N_GROUPS - N_EXPERTS
    wr = jnp.concatenate([w_grp, w_exp, jnp.zeros((D_MODEL, pad), F32)], axis=1).astype(BF16)
    br = jnp.concatenate([b_grp, b_exp, jnp.zeros((pad,), F32)]).reshape(1, LANES)
    route = _router_call(xf, wr, br)
    eid = route[:, 0:2].astype(jnp.int32)
    y2 = _moe_experts(_moe_plan(eid, MOE_TM), xf, w_gate, w_up, w_down, layer)
    return y2, route


def kernel(x, c, ctx, c_ctx, w_mod, b_mod, w_in, b_in, attn_sink, lru_conv_w, lru_conv_b,
           lru_w_a, lru_b_a, lru_w_x, lru_b_x, lru_lam, sc_conv_w, w_branch, w_out,
           ln1_g, ln1_b, ln2_g, ln2_b, w_grp, b_grp, w_exp, b_exp, w_gate, w_up, w_down):
    assert x.shape == (1, SEQ, D_MODEL) and ctx.shape == (1, CTX_LEN, D_MODEL)
    xs = x[0]
    xc = ctx[0]
    cos, sin_signed = _rope_tables(SEQ)
    c8 = jnp.concatenate([c, c_ctx[None, :], jnp.zeros((SUBLANES - 2, D_MODEL), F32)], axis=0)
    zero_h0 = jnp.zeros((2, LRU_WIDTH), F32)

    for l in range(DEPTH):
        last = l == DEPTH - 1
        wb_bf = w_branch[l].astype(BF16)
        wo_bf = w_out[l].astype(BF16)

        mods = _mod_call(c8, w_mod, b_mod[l], l)
        m_lat = mods[0:1]
        m_ctx = mods[1:2]

        u = _lnmod_call(xs, m_lat, 0, 1, BF16)
        u_ctx = _lnmod_call(xc, m_ctx, 0, 1, BF16)
        p = _proj_call(u, w_in, b_in[l], cos, sin_signed, l, rope=True, tm=1024)
        p_ctx = _proj_call(u_ctx, w_in, b_in[l], cos, sin_signed, l, rope=False, tm=CTX_LEN)

        y_att = _attn_call(p, p_ctx, attn_sink[l])
        lru_args = (lru_conv_w[l], lru_conv_b[l], lru_w_a[l], lru_b_a[l], lru_w_x[l], lru_b_x[l], lru_lam[l])
        hcf, hcr = _lru_call(p_ctx, zero_h0, *lru_args, tc=CTX_LEN)
        h0 = jnp.concatenate([hcf[CTX_LEN - 1:CTX_LEN], hcr[0:1]], axis=0)
        hf, hr = _lru_call(p, h0, *lru_args, tc=256)
        y_lru, y_sc = _prep_call(p, hf, hr, sc_conv_w[l], tm=256)
        z = _mergez_call(y_att, y_lru, y_sc, wb_bf, p, tm=1024)
        xs = _outnorm_call(z, wo_bf, xs, m_lat, 2, ln1_g[l], ln1_b[l])

        if last:
            xf = _lnmod_call(xs, m_lat, 3, 4, F32)
            ypair2, route = _moe(xf, w_grp[l], b_grp[l], w_exp[l], b_exp[l], w_gate, w_up, w_down, l)
            xs = _combine_call(ypair2, route, xs, m_lat, 5, ln2_g[l], ln2_b[l], 0)
        else:
            yc_att = _ctx_attn_call(p_ctx, attn_sink[l])
            yc_lru, yc_sc = _prep_call(p_ctx, hcf, hcr, sc_conv_w[l], tm=CTX_LEN)
            zc = _mergez_call(yc_att, yc_lru, yc_sc, wb_bf, p_ctx, tm=CTX_LEN)
            xc = _outnorm_call(zc, wo_bf, xc, m_ctx, 2, ln1_g[l], ln1_b[l])
            xf = _lnmod_pair_call(xs, m_lat, xc, m_ctx, 3, 4, F32)
            ypair2, route = _moe(xf, w_grp[l], b_grp[l], w_exp[l], b_exp[l], w_gate, w_up, w_down, l)
            xs_new =_combine_call(ypair2, route, xs, m_lat, 5, ln2_g[l], ln2_b[l], 0)
            xc = _combine_call(ypair2, route, xc, m_ctx, 5, ln2_g[l], ln2_b[l], SEQ // 256)
            xs = xs_new
    return xs[None]
```

```python
import functools

import jax
import jax.numpy as jnp
from jax import lax
from jax.experimental import pallas as pl
from jax.experimental.pallas import tpu as pltpu

F32 = jnp.float32
BF16 = jnp.bfloat16

D_MODEL = 2048
SEQ = 8192
CTX_LEN = 256
DEPTH = 2
GRID_W = 64

N_HEADS = 8
N_KV_HEADS = 2
HEAD_DIM = 128
Q_GROUPS = N_HEADS // N_KV_HEADS
WINDOW = 128
BLOCK = 128
ROPE_THETA = 10000.0
Q_SCALE = HEAD_DIM ** -0.5

LRU_WIDTH = 1024
LRU_BLOCKS = 8
LRU_BLOCK_DIM = 128
LRU_C = 8.0
SC_WIDTH = 1024
N_BRANCH = 3
BRANCH_W = 1024

Q_W = N_HEADS * HEAD_DIM
KV_W = N_KV_HEADS * HEAD_DIM
OFF_K = Q_W
OFF_V = OFF_K + KV_W
OFF_LX = OFF_V + KV_W
OFF_LG = OFF_LX + LRU_WIDTH
OFF_SB = OFF_LG + LRU_WIDTH
OFF_SC = OFF_SB + SC_WIDTH
OFF_SX = OFF_SC + SC_WIDTH
OFF_GATE = OFF_SX + SC_WIDTH
PROJ_W = OFF_GATE + N_BRANCH * D_MODEL

N_GROUPS = 4
EXPERTS_PER_GROUP = 8
N_EXPERTS = N_GROUPS * EXPERTS_PER_GROUP
D_EXPERT = 1024

ALPHA = (2 * DEPTH) ** 0.25
LN_EPS = 1e-5

VMEM_LIMIT_BYTES = 52 * 1024 * 1024
BIG_VMEM_LIMIT_BYTES = 58 * 1024 * 1024
LANES = 128
SUBLANES = 8

HALF_W = 512
PROJ_TN = 1280
MOE_TM = 256


def _cparams(*sem):
    return pltpu.CompilerParams(dimension_semantics=sem, vmem_limit_bytes=VMEM_LIMIT_BYTES)


def _layer_norm(x):
    mu = jnp.mean(x, axis=-1, keepdims=True)
    xc = x - mu
    var = jnp.mean(xc * xc, axis=-1, keepdims=True)
    return xc * lax.rsqrt(var + LN_EPS)


def _mod_kernel(c_ref, w_ref, b_ref, o_ref):
    a = jax.nn.silu(c_ref[...]).astype(BF16)
    o_ref[...] = jnp.dot(a, w_ref[...].astype(BF16), preferred_element_type=F32) + b_ref[...]


def _mod_call(c8, w_all, b, layer):
    n = w_all.shape[2]
    tn = 1024
    return pl.pallas_call(
        _mod_kernel,
        out_shape=jax.ShapeDtypeStruct((SUBLANES, n), F32),
        grid=(n // tn,),
        in_specs=[pl.BlockSpec((SUBLANES, D_MODEL), lambda j: (0, 0)),
                  pl.BlockSpec((None, D_MODEL, tn), lambda j: (layer, 0, j)),
                  pl.BlockSpec((1, tn), lambda j: (0, j))],
        out_specs=pl.BlockSpec((SUBLANES, tn), lambda j: (0, j)),
        compiler_params=_cparams("arbitrary"),
        name="mod_vectors",
    )(c8, w_all, b.reshape(1, n))


def _lnmod_kernel(x_ref, sh_ref, sc_ref, o_ref):
    y = _layer_norm(x_ref[...])
    o_ref[...] = (y * (1.0 + sc_ref[...]) + sh_ref[...]).astype(o_ref.dtype)


def _lnmod_call(x, mod, shift_chunk, scale_chunk, out_dtype):
    t = x.shape[0]
    tm = 256
    return pl.pallas_call(
        _lnmod_kernel,
        out_shape=jax.ShapeDtypeStruct((t, D_MODEL), out_dtype),
        grid=(t // tm,),
        in_specs=[pl.BlockSpec((tm, D_MODEL), lambda i: (i, 0)),
                  pl.BlockSpec((1, D_MODEL), lambda i: (0, shift_chunk)),
                  pl.BlockSpec((1, D_MODEL), lambda i: (0, scale_chunk))],
        out_specs=pl.BlockSpec((tm, D_MODEL), lambda i: (i, 0)),
        compiler_params=_cparams("arbitrary"),
        name="ln_modulate",
    )(x, mod, mod)


def _lnmod_pair_kernel(xa_ref, xb_ref, sha_ref, sca_ref, shb_ref, scb_ref, o_ref, *, na):
    i = pl.program_id(0)

    @pl.when(i < na)
    def _():
        _lnmod_kernel(xa_ref, sha_ref, sca_ref, o_ref)

    @pl.when(i >= na)
    def _():
        _lnmod_kernel(xb_ref, shb_ref, scb_ref, o_ref)


def _lnmod_pair_call(xa, mod_a, xb, mod_b, shift_chunk, scale_chunk, out_dtype):
    tm = 256
    na = xa.shape[0] // tm
    nb = xb.shape[0] // tm
    vec = lambda chunk: pl.BlockSpec((1, D_MODEL), lambda i: (0, chunk))
    return pl.pallas_call(
        functools.partial(_lnmod_pair_kernel, na=na),
        out_shape=jax.ShapeDtypeStruct(((na + nb) * tm, D_MODEL), out_dtype),
        grid=(na + nb,),
        in_specs=[pl.BlockSpec((tm, D_MODEL), lambda i: (jnp.minimum(i, na - 1), 0)),
                  pl.BlockSpec((tm, D_MODEL), lambda i: (jnp.maximum(i - na, 0), 0)),
                  vec(shift_chunk), vec(scale_chunk), vec(shift_chunk), vec(scale_chunk)],
        out_specs=pl.BlockSpec((tm, D_MODEL), lambda i: (i, 0)),
        compiler_params=_cparams("arbitrary"),
        name="ln_modulate_pair",
    )(xa, xb, mod_a, mod_a, mod_b, mod_b)


def _rope_head(xh, cos, sin_signed, first_half):
    partner = jnp.where(first_half, pltpu.roll(xh, LANES - 32, 1), pltpu.roll(xh, 32, 1))
    return xh * cos + partner * sin_signed


def _proj_kernel(u_ref, w_ref, b_ref, cos_ref, sin_ref, o_ref, w_scr, *, rope):
    j = pl.program_id(0)
    i = pl.program_id(1)

    @pl.when(i == 0)
    def _():
        w_scr[...] = w_ref[...].astype(BF16)

    acc = jnp.dot(u_ref[...], w_scr[...], preferred_element_type=F32) + b_ref[...]
    if not rope:
        o_ref[...] = acc
        return

    @pl.when(j != 0)
    def _():
        o_ref[...] = acc

    @pl.when(j == 0)
    def _():
        cos = cos_ref[...]
        sin_signed = sin_ref[...]
        lane = lax.broadcasted_iota(jnp.int32, cos.shape, 1)
        first_half = (lane & 63) < 32
        for h in range(N_HEADS + N_KV_HEADS):
            sl = slice(h * HEAD_DIM, (h + 1) * HEAD_DIM)
            r = _rope_head(acc[:, sl], cos, sin_signed, first_half)
            if h < N_HEADS:
                r = r * Q_SCALE
            o_ref[:, sl] = r


def _proj_call(u, w_all, b, cos, sin_signed, layer, *, rope, tm):
    t = u.shape[0]
    n = w_all.shape[2]
    kern = functools.partial(_proj_kernel, rope=rope)
    return pl.pallas_call(
        kern,
        out_shape=jax.ShapeDtypeStruct((t, n), F32),
        grid=(n // PROJ_TN, t // tm),
        in_specs=[pl.BlockSpec((tm, D_MODEL), lambda j, i: (i, 0)),
                  pl.BlockSpec((None, D_MODEL, PROJ_TN), lambda j, i: (layer, 0, j)),
                  pl.BlockSpec((1, PROJ_TN), lambda j, i: (0, j)),
                  pl.BlockSpec((tm, HEAD_DIM), lambda j, i: (i, 0)),
                  pl.BlockSpec((tm, HEAD_DIM), lambda j, i: (i, 0))],
        out_specs=pl.BlockSpec((tm, PROJ_TN), lambda j, i: (i, j)),
        scratch_shapes=[pltpu.VMEM((D_MODEL, PROJ_TN), BF16)],
        compiler_params=pltpu.CompilerParams(dimension_semantics=("arbitrary", "arbitrary"),
                                             vmem_limit_bytes=BIG_VMEM_LIMIT_BYTES),
        name="in_proj_rope" if rope else "in_proj_ctx",
    )(u, w_all, b.reshape(1, n), cos, sin_signed)


def _softmax_pv(q, k_all, v_all, bias, sink_col):
    s = lax.dot_general(q, k_all, (((1,), (1,)), ((), ())), preferred_element_type=F32)
    if bias is not None:
        s = s + bias
    m = jnp.maximum(jnp.max(s, axis=-1, keepdims=True), sink_col)
    e = jnp.exp(s - m)
    den = jnp.sum(e, axis=-1, keepdims=True) + jnp.exp(sink_col - m)
    p = (e / den).astype(BF16)
    return jnp.dot(p, v_all, preferred_element_type=F32)


def _sink_column(sink_ref, kv, rows):
    row = lax.broadcasted_iota(jnp.int32, (Q_GROUPS * rows, 1), 0)
    col = jnp.full((Q_GROUPS * rows, 1), sink_ref[kv * Q_GROUPS + Q_GROUPS - 1], F32)
    for g in range(Q_GROUPS - 2, -1, -1):
        col = jnp.where(row < (g + 1) * rows, sink_ref[kv * Q_GROUPS + g], col)
    return col


def _attn_kernel(sink_ref, q_ref, kp_ref, kc_ref, kn_ref, vp_ref, vc_ref, vn_ref,
                 kx_ref, vx_ref, o_ref, *, seq):
    n = pl.program_id(0)
    n_loc = 3 * BLOCK
    n_keys = n_loc + CTX_LEN
    qi = lax.broadcasted_iota(jnp.int32, (BLOCK, n_keys), 0)
    kj = lax.broadcasted_iota(jnp.int32, (BLOCK, n_keys), 1)
    rel = kj - BLOCK - qi
    kpos = n * BLOCK - BLOCK + kj
    bad = jnp.where(kj < n_loc,
                    jnp.where(jnp.abs(rel) > WINDOW, 1, 0) + jnp.where(kpos < 0, 1, 0)
                    + jnp.where(kpos >= seq, 1, 0), 0)
    bias1 = jnp.where(bad > 0, -jnp.inf, 0.0).astype(F32)
    bias = jnp.concatenate([bias1] * Q_GROUPS, axis=0)
    for kv in range(N_KV_HEADS):
        ks = slice(kv * HEAD_DIM, (kv + 1) * HEAD_DIM)
        k_all = jnp.concatenate([kp_ref[:, ks], kc_ref[:, ks], kn_ref[:, ks], kx_ref[:, ks]],
                                axis=0).astype(BF16)
        v_all = jnp.concatenate([vp_ref[:, ks], vc_ref[:, ks], vn_ref[:, ks], vx_ref[:, ks]],
                                axis=0).astype(BF16)
        q = jnp.concatenate(
            [q_ref[:, (kv * Q_GROUPS + g) * HEAD_DIM:(kv * Q_GROUPS + g + 1) * HEAD_DIM]
             for g in range(Q_GROUPS)], axis=0).astype(BF16)
        o = _softmax_pv(q, k_all, v_all, bias, _sink_column(sink_ref, kv, BLOCK))
        for g in range(Q_GROUPS):
            h = kv * Q_GROUPS + g
            o_ref[:, h * HEAD_DIM:(h + 1) * HEAD_DIM] = o[g * BLOCK:(g + 1) * BLOCK].astype(o_ref.dtype)


def _attn_call(p, p_ctx, sink):
    s = p.shape[0]
    nb = s // BLOCK
    kcol = OFF_K // KV_W
    vcol = OFF_V // KV_W
    kv_spec = lambda f, c: pl.BlockSpec((BLOCK, KV_W), lambda n: (f(n), c))
    prev = lambda n: jnp.maximum(n - 1, 0)
    cur = lambda n: n
    nxt = lambda n: jnp.minimum(n + 1, nb - 1)
    return pl.pallas_call(
        functools.partial(_attn_kernel, seq=s),
        out_shape=jax.ShapeDtypeStruct((s, Q_W), BF16),
        grid=(nb,),
        in_specs=[pl.BlockSpec(memory_space=pltpu.SMEM),
                  pl.BlockSpec((BLOCK, Q_W), lambda n: (n, 0)),
                  kv_spec(prev, kcol), kv_spec(cur, kcol), kv_spec(nxt, kcol),
                  kv_spec(prev, vcol), kv_spec(cur, vcol), kv_spec(nxt, vcol),
                  pl.BlockSpec((CTX_LEN, KV_W), lambda n: (0, kcol)),
                  pl.BlockSpec((CTX_LEN, KV_W), lambda n: (0, vcol))],
        out_specs=pl.BlockSpec((BLOCK, Q_W), lambda n: (n, 0)),
        compiler_params=_cparams("arbitrary"),
        name="window_attention",
    )(sink, p, p, p, p, p, p, p, p_ctx, p_ctx)


def _ctx_attn_kernel(sink_ref, q_ref, k_ref, v_ref, o_ref):
    rows = q_ref.shape[0]
    for kv in range(N_KV_HEADS):
        ks = slice(kv * HEAD_DIM, (kv + 1) * HEAD_DIM)
        q = jnp.concatenate(
            [q_ref[:, (kv * Q_GROUPS + g) * HEAD_DIM:(kv * Q_GROUPS + g + 1) * HEAD_DIM] * Q_SCALE
             for g in range(Q_GROUPS)], axis=0).astype(BF16)
        o = _softmax_pv(q, k_ref[:, ks].astype(BF16), v_ref[:, ks].astype(BF16), None,
                        _sink_column(sink_ref, kv, rows))
        for g in range(Q_GROUPS):
            h = kv * Q_GROUPS + g
            o_ref[:, h * HEAD_DIM:(h + 1) * HEAD_DIM] = o[g * rows:(g + 1) * rows].astype(o_ref.dtype)


def _ctx_attn_call(p_ctx, sink):
    lc = p_ctx.shape[0]
    return pl.pallas_call(
        _ctx_attn_kernel,
        out_shape=jax.ShapeDtypeStruct((lc, Q_W), BF16),
        grid=(1,),
        in_specs=[pl.BlockSpec(memory_space=pltpu.SMEM),
                  pl.BlockSpec((lc, Q_W), lambda n: (0, 0)),
                  pl.BlockSpec((lc, KV_W), lambda n: (0, OFF_K // KV_W)),
                  pl.BlockSpec((lc, KV_W), lambda n: (0, OFF_V // KV_W))],
        out_specs=pl.BlockSpec((lc, Q_W), lambda n: (0, 0)),
        compiler_params=_cparams("arbitrary"),
        name="context_attention",
    )(sink, p_ctx, p_ctx, p_ctx)


def _lru_conv(ext_scr, cur_ref, prev_ref, next_ref, is_first, is_last, cw_ref, cb_ref, tc):
    ext_scr[0:SUBLANES, :] = jnp.where(is_first, 0.0, prev_ref[...])
    ext_scr[SUBLANES:SUBLANES + tc, :] = cur_ref[...]
    ext_scr[SUBLANES + tc:, :] = jnp.where(is_last, 0.0, next_ref[...])
    out = cb_ref[...] + cw_ref[0:1, :] * ext_scr[SUBLANES - 1:SUBLANES - 1 + tc, :]
    for j in range(1, 4):
        out = out + cw_ref[j:j + 1, :] * ext_scr[SUBLANES - 1 + j:SUBLANES - 1 + j + tc, :]
    return out


def _lru_gates(xb, d, wa_ref, ba_ref, wx_ref, bx_ref, lam_ref, a_scr, u_scr):
    xb_bf = xb.astype(BF16)
    for b in range(HALF_W // LRU_BLOCK_DIM):
        sl = slice(b * LRU_BLOCK_DIM, (b + 1) * LRU_BLOCK_DIM)
        xs = xb_bf[:, sl]
        r = jax.nn.sigmoid(jnp.dot(xs, wa_ref[d, b].astype(BF16), preferred_element_type=F32)
                           + ba_ref[d:d + 1, sl])
        ig = jax.nn.sigmoid(jnp.dot(xs, wx_ref[d, b].astype(BF16), preferred_element_type=F32)
                            + bx_ref[d:d + 1, sl])
        log_a = -LRU_C * r * jax.nn.softplus(-lam_ref[d:d + 1, sl])
        a_scr[d, :, sl] = jnp.exp(log_a)
        u_scr[d, :, sl] = jnp.sqrt(1.0 - jnp.exp(2.0 * log_a)) * ig * xb[:, sl]


def _lru_kernel(fc_ref, fp_ref, fn_ref, rc_ref, rp_ref, rn_ref,
                cw_ref, cb_ref, wa_ref, ba_ref, wx_ref, bx_ref, lam_ref, h0_ref,
                hf_ref, hr_ref, ext_scr, a_scr, u_scr, cf_scr, cr_scr, *, tc, nchunks):
    i = pl.program_id(1)
    w = HALF_W

    @pl.when(i == 0)
    def _():
        cf_scr[...] = jnp.broadcast_to(h0_ref[0:1, :], (SUBLANES, w))
        cr_scr[...] = jnp.broadcast_to(h0_ref[1:2, :], (SUBLANES, w))

    xb_f = _lru_conv(ext_scr, fc_ref, fp_ref, fn_ref, i == 0, i == nchunks - 1, cw_ref, cb_ref, tc)
    _lru_gates(xb_f, 0, wa_ref, ba_ref, wx_ref, bx_ref, lam_ref, a_scr, u_scr)
    xb_r = _lru_conv(ext_scr, rc_ref, rp_ref, rn_ref, i == nchunks - 1, i == 0, cw_ref, cb_ref, tc)
    _lru_gates(xb_r, 1, wa_ref, ba_ref, wx_ref, bx_ref, lam_ref, a_scr, u_scr)

    row = lax.broadcasted_iota(jnp.int32, (SUBLANES, w), 0)
    ntile = tc // SUBLANES

    def body(j, carry):
        cf, cr = carry
        off = pl.multiple_of(j * SUBLANES, SUBLANES)
        a = a_scr[0, pl.ds(off, SUBLANES), :]
        u = u_scr[0, pl.ds(off, SUBLANES), :]
        for s in (1, 2, 4):
            m = row >= s
            u = jnp.where(m, u + a * pltpu.roll(u, s, 0), u)
            a = jnp.where(m, a * pltpu.roll(a, s, 0), a)
        h = u + a * cf
        hf_ref[pl.ds(off, SUBLANES), :] = h
        cf = jnp.broadcast_to(h[SUBLANES - 1:SUBLANES, :], (SUBLANES, w))
        offr = pl.multiple_of((ntile - 1 - j) * SUBLANES, SUBLANES)
        a = a_scr[1, pl.ds(offr, SUBLANES), :]
        u = u_scr[1, pl.ds(offr, SUBLANES), :]
        for s in (1, 2, 4):
            m = row < SUBLANES - s
            u = jnp.where(m, u + a * pltpu.roll(u, SUBLANES - s, 0), u)
            a = jnp.where(m, a * pltpu.roll(a, SUBLANES - s, 0), a)
        h = u + a * cr
        hr_ref[pl.ds(offr, SUBLANES), :] = h
        cr = jnp.broadcast_to(h[0:1, :], (SUBLANES, w))
        return cf, cr

    cf, cr = lax.fori_loop(0, ntile, body, (cf_scr[...], cr_scr[...]))
    cf_scr[...] = cf
    cr_scr[...] = cr


def _lru_call(p, h0, cw, cb, wa, ba, wx, bx, lam, *, tc):
    t = p.shape[0]
    nchunks = t // tc
    w = HALF_W
    c0 = OFF_LX // w
    r8 = tc // SUBLANES
    last8 = t // SUBLANES - 1
    fwd = lambda i: i
    rev = lambda i: nchunks - 1 - i
    cur_spec = lambda f: pl.BlockSpec((tc, w), lambda h, i: (f(i), c0 + h))
    prev_spec = lambda f: pl.BlockSpec((SUBLANES, w), lambda h, i: (jnp.maximum(f(i) * r8 - 1, 0), c0 + h))
    next_spec = lambda f: pl.BlockSpec((SUBLANES, w), lambda h, i: (jnp.minimum((f(i) + 1) * r8, last8), c0 + h))
    vec_spec = lambda rows: pl.BlockSpec((rows, w), lambda h, i: (0, h))
    gate_w_spec = pl.BlockSpec((2, w // LRU_BLOCK_DIM, LRU_BLOCK_DIM, LRU_BLOCK_DIM),
                               lambda h, i: (0, h, 0, 0))
    kern = functools.partial(_lru_kernel, tc=tc, nchunks=nchunks)
    return pl.pallas_call(
        kern,
        out_shape=(jax.ShapeDtypeStruct((t, LRU_WIDTH), F32), jax.ShapeDtypeStruct((t, LRU_WIDTH), F32)),
        grid=(LRU_WIDTH // w, nchunks),
        in_specs=[cur_spec(fwd), prev_spec(fwd), next_spec(fwd),
                  cur_spec(rev), prev_spec(rev), next_spec(rev),
                  vec_spec(4), vec_spec(1), gate_w_spec, vec_spec(2), gate_w_spec, vec_spec(2),
                  vec_spec(2), vec_spec(2)],
        out_specs=(pl.BlockSpec((tc, w), lambda h, i: (i, h)),
                   pl.BlockSpec((tc, w), lambda h, i: (nchunks - 1 - i, h))),
        scratch_shapes=[pltpu.VMEM((tc + 2 * SUBLANES, w), F32),
                        pltpu.VMEM((2, tc, w), F32), pltpu.VMEM((2, tc, w), F32),
                        pltpu.VMEM((SUBLANES, w), F32), pltpu.VMEM((SUBLANES, w), F32)],
        compiler_params=_cparams("arbitrary", "arbitrary"),
        name="rg_lru",
    )(p, p, p, p, p, p, cw, cb.reshape(1, LRU_WIDTH), wa, ba.reshape(2, LRU_WIDTH),
      wx, bx.reshape(2, LRU_WIDTH), lam, h0)


def _prep_kernel(lg_ref, hf_ref, hr_ref, sb_ref, scc_ref, scp_ref, scn_ref,
                 sxc_ref, sxp_ref, sxn_ref, w3_ref, ylru_ref, ysc_ref, ext_scr, *, tm, ntiles):
    i = pl.program_id(0)
    ylru_ref[...] = ((hf_ref[...] + hr_ref[...]) * jax.nn.gelu(lg_ref[...])).astype(ylru_ref.dtype)
    ext_scr[0:SUBLANES, :] = jnp.where(i == 0, 0.0, scp_ref[...] * sxp_ref[...])
    ext_scr[SUBLANES:SUBLANES + tm, :] = scc_ref[...] * sxc_ref[...]
    ext_scr[SUBLANES + tm:, :] = jnp.where(i == ntiles - 1, 0.0, scn_ref[...] * sxn_ref[...])
    conv = w3_ref[0:1, :] * ext_scr[SUBLANES - 1:SUBLANES - 1 + tm, :]
    for j in range(1, 3):
        conv = conv + w3_ref[j:j + 1, :] * ext_scr[SUBLANES - 1 + j:SUBLANES - 1 + j + tm, :]
    ysc_ref[...] = (sb_ref[...] * conv).astype(ysc_ref.dtype)


def _prep_call(p, hf, hr, w3, *, tm):
    t = p.shape[0]
    ntiles = t // tm
    w = HALF_W
    r8 = tm // SUBLANES
    last8 = t // SUBLANES - 1
    col = lambda off: pl.BlockSpec((tm, w), lambda i, h: (i, off // w + h))
    colp = lambda off: pl.BlockSpec((SUBLANES, w), lambda i, h: (jnp.maximum(i * r8 - 1, 0), off // w + h))
    coln = lambda off: pl.BlockSpec((SUBLANES, w), lambda i, h: (jnp.minimum((i + 1) * r8, last8), off // w + h))
    own = pl.BlockSpec((tm, w), lambda i, h: (i, h))
    kern = functools.partial(_prep_kernel, tm=tm, ntiles=ntiles)
    return pl.pallas_call(
        kern,
        out_shape=(jax.ShapeDtypeStruct((t, LRU_WIDTH), BF16), jax.ShapeDtypeStruct((t, SC_WIDTH), BF16)),
        grid=(ntiles, LRU_WIDTH // w),
        in_specs=[col(OFF_LG), own, own, col(OFF_SB),
                  col(OFF_SC), colp(OFF_SC), coln(OFF_SC),
                  col(OFF_SX), colp(OFF_SX), coln(OFF_SX),
                  pl.BlockSpec((3, w), lambda i, h: (0, h))],
        out_specs=(own, own),
        scratch_shapes=[pltpu.VMEM((tm + 2 * SUBLANES, w), F32)],
        compiler_params=_cparams("arbitrary", "arbitrary"),
        name="branch_prep",
    )(p, hf, hr, p, p, p, p, p, p, p, w3)


def _mergez_kernel(b0_ref, b1_ref, b2_ref, wb_ref, g0_ref, g1_ref, g2_ref, z_ref):
    acc = None
    for j, (b_ref, g_ref) in enumerate(((b0_ref, g0_ref), (b1_ref, g1_ref), (b2_ref, g2_ref))):
        t = jax.nn.sigmoid(g_ref[...]) * jnp.dot(b_ref[...], wb_ref[j].astype(BF16),
                                                  preferred_element_type=F32)
        acc = t if acc is None else acc + t
    z_ref[...] = acc.astype(z_ref.dtype)


def _mergez_call(y_att, y_lru, y_sc, wb_all, p, layer, *, tm):
    t = p.shape[0]
    tn = 512
    br = pl.BlockSpec((tm, BRANCH_W), lambda i, j: (i, 0))
    gate = lambda k: pl.BlockSpec((tm, tn), lambda i, j: (i, (OFF_GATE + k * D_MODEL) // tn + j))
    return pl.pallas_call(
        _mergez_kernel,
        out_shape=jax.ShapeDtypeStruct((t, D_MODEL), BF16),
        grid=(t // tm, D_MODEL // tn),
        in_specs=[br, br, br,
                  pl.BlockSpec((None, N_BRANCH, BRANCH_W, tn), lambda i, j: (layer, 0, 0, j)),
                  gate(0), gate(1), gate(2)],
        out_specs=pl.BlockSpec((tm, tn), lambda i, j: (i, j)),
        compiler_params=_cparams("arbitrary", "arbitrary"),
        name="branch_merge",
    )(y_att, y_lru, y_sc, wb_all, p, p, p)


def _post_norm_store(r, g_ref, b_ref, nsh_ref, nsc_ref, o_ref, on_ref):
    o = _layer_norm(r) * g_ref[...] + b_ref[...]
    o_ref[...] = o
    if on_ref is not None:
        on_ref[...] = (_layer_norm(o) * (1.0 + nsc_ref[...]) + nsh_ref[...]).astype(on_ref.dtype)


def _outnorm_kernel(z_ref, w_ref, x_ref, gate_ref, g_ref, b_ref, *rest, fused):
    if fused:
        nsh_ref, nsc_ref, o_ref, on_ref, w_scr = rest
    else:
        (o_ref, w_scr), nsh_ref, nsc_ref, on_ref = rest, None, None, None

    @pl.when(pl.program_id(0) == 0)
    def _():
        w_scr[...] = w_ref[...].astype(BF16)

    y = jnp.dot(z_ref[...], w_scr[...], preferred_element_type=F32)
    r = ALPHA * x_ref[...] + gate_ref[...] * y
    _post_norm_store(r, g_ref, b_ref, nsh_ref, nsc_ref, o_ref, on_ref)


def _outnorm_call(z, w_all, layer, x, mod, gate_chunk, g, b, next_mod=None, next_dtype=None):
    t = z.shape[0]
    tm = 256
    row = pl.BlockSpec((tm, D_MODEL), lambda i: (i, 0))
    vec = pl.BlockSpec((1, D_MODEL), lambda i: (0, 0))
    chunk = lambda c: pl.BlockSpec((1, D_MODEL), lambda i: (0, c))
    in_specs = [row,
                pl.BlockSpec((None, D_MODEL, D_MODEL), lambda i: (layer, 0, 0),
                             pipeline_mode=pl.Buffered(1)),
                row, chunk(gate_chunk), vec, vec]
    args = [z, w_all, x, mod, g.reshape(1, D_MODEL), b.reshape(1, D_MODEL)]
    out_shape = jax.ShapeDtypeStruct((t, D_MODEL), F32)
    out_specs = row
    if next_mod is not None:
        nmod, sh, sc = next_mod
        in_specs += [chunk(sh), chunk(sc)]
        args += [nmod, nmod]
        out_shape = (out_shape, jax.ShapeDtypeStruct((t, D_MODEL), next_dtype))
        out_specs = (row, row)
    return pl.pallas_call(
        functools.partial(_outnorm_kernel, fused=next_mod is not None),
        out_shape=out_shape,
        grid=(t // tm,),
        in_specs=in_specs,
        out_specs=out_specs,
        scratch_shapes=[pltpu.VMEM((D_MODEL, D_MODEL), BF16)],
        compiler_params=_cparams("arbitrary"),
        name="out_proj_norm",
    )(*args)


def _router_kernel(x_ref, w_ref, b_ref, o_ref):
    logits = jnp.dot(x_ref[...].astype(BF16), w_ref[...], preferred_element_type=F32) + b_ref[...]
    lane = lax.broadcasted_iota(jnp.int32, logits.shape, 1)
    neg = -jnp.inf
    is_grp = lane < N_GROUPS
    gl = jnp.where(is_grp, logits, neg)
    gmax = jnp.max(gl, axis=-1, keepdims=True)
    g_idx = jnp.min(jnp.where(gl == gmax, lane, LANES), axis=-1, keepdims=True)
    gsum = jnp.sum(jnp.where(is_grp, jnp.exp(gl - gmax), 0.0), axis=-1, keepdims=True)
    g_w = 1.0 / gsum
    lo = N_GROUPS + g_idx * EXPERTS_PER_GROUP
    in_grp = jnp.where(lane >= lo, 1, 0) * jnp.where(lane < lo + EXPERTS_PER_GROUP, 1, 0)
    sel = jnp.where(in_grp > 0, logits, neg)
    v0 = jnp.max(sel, axis=-1, keepdims=True)
    i0 = jnp.min(jnp.where(sel == v0, lane, LANES), axis=-1, keepdims=True)
    sel1 = jnp.where(lane == i0, neg, sel)
    v1 = jnp.max(sel1, axis=-1, keepdims=True)
    i1 = jnp.min(jnp.where(sel1 == v1, lane, LANES), axis=-1, keepdims=True)
    e1 = jnp.exp(v1 - v0)
    den = 1.0 + e1
    w0 = (1.0 / den) * g_w
    w1 = (e1 / den) * g_w
    out = jnp.where(lane == 0, (i0 - N_GROUPS).astype(F32),
                    jnp.where(lane == 1, (i1 - N_GROUPS).astype(F32),
                              jnp.where(lane == 2, w0, jnp.where(lane == 3, w1, 0.0))))
    o_ref[...] = out


def _router_call(xf, wr_bf, br):
    t = xf.shape[0]
    tm = 256
    return pl.pallas_call(
        _router_kernel,
        out_shape=jax.ShapeDtypeStruct((t, LANES), F32),
        grid=(t // tm,),
        in_specs=[pl.BlockSpec((tm, D_MODEL), lambda i: (i, 0)),
                  pl.BlockSpec((D_MODEL, LANES), lambda i: (0, 0)),
                  pl.BlockSpec((1, LANES), lambda i: (0, 0))],
        out_specs=pl.BlockSpec((tm, LANES), lambda i: (i, 0)),
        compiler_params=_cparams("arbitrary"),
        name="moe_router",
    )(xf, wr_bf, br)


MOE_UNROLL = 8


def _tile_rows(q0_ref, cnt_ref, order_ref, tile, fn, static_rows):
    base = q0_ref[tile]
    cnt = cnt_ref[tile]
    if static_rows:
        for r in range(MOE_TM):
            fn(r, order_ref[base + r], r < cnt)
        return

    def body(blk, c):
        for k in range(MOE_UNROLL):
            r = blk * MOE_UNROLL + k
            fn(r, order_ref[base + r], r < cnt)
        return c
    lax.fori_loop(0, MOE_TM // MOE_UNROLL, body, 0)


def _is_new_expert(te_ref, i):
    return jnp.logical_or(i == 0, te_ref[i] != te_ref[jnp.maximum(i - 1, 0)])


def _moe_up_kernel(te_ref, q0_ref, cnt_ref, nt_ref, tok_ref, x_hbm, wg_ref, wu_ref, h_ref,
                   xbuf, wg_scr, wu_scr, gsem):
    i = pl.program_id(0)
    nt = nt_ref[0]

    def start_gather(tile, s, static_rows):
        def issue(r, tok, valid):
            pltpu.make_async_copy(x_hbm.at[pl.ds(tok, 1), :], xbuf.at[s, pl.ds(r, 1), :],
                                  gsem.at[s]).start()
        _tile_rows(q0_ref, cnt_ref, tok_ref, tile, issue, static_rows)

    def wait_gather(s):
        pltpu.make_async_copy(x_hbm.at[pl.ds(0, MOE_TM), :], xbuf.at[s], gsem.at[s]).wait()

    @pl.when(i == 0)
    def _():
        start_gather(0, 0, False)

    @pl.when(_is_new_expert(te_ref, i))
    def _():
        wg_scr[...] = wg_ref[...].astype(BF16)
        wu_scr[...] = wu_ref[...].astype(BF16)

    for s in range(2):
        @pl.when(jnp.logical_and(i < nt, i % 2 == s))
        def _(s=s):
            wait_gather(s)
            start_gather(jnp.minimum(i + 1, nt - 1), 1 - s, True)
            x = xbuf[s].astype(BF16)
            hg = jnp.dot(x, wg_scr[...], preferred_element_type=F32)
            hu = jnp.dot(x, wu_scr[...], preferred_element_type=F32)
            h_ref[...] = (jax.nn.silu(hg) * hu).astype(h_ref.dtype)

            @pl.when(i == nt - 1)
            def _():
                wait_gather(1 - s)

    @pl.when(i >= nt)
    def _():
        h_ref[...] = jnp.zeros(h_ref.shape, h_ref.dtype)


def _moe_down_kernel(te_ref, q0_ref, cnt_ref, nt_ref, order_ref, h_ref, wd_ref, y_hbm,
                     ybuf, wd_scr, ssem, *, ntok):
    i = pl.program_id(0)
    nt = nt_ref[0]

    def wait_scatter(s):
        pltpu.make_async_copy(ybuf.at[s], y_hbm.at[pl.ds(0, MOE_TM), pl.ds(0, D_MODEL)], ssem.at[s]).wait()

    def start_scatter(tile, s):
        def issue(r, pair, valid):
            dst = jnp.where(valid, pair, 2 * ntok + s * MOE_TM + r)
            tok = lax.shift_right_logical(dst, 1)
            col = pl.multiple_of((dst & 1) * D_MODEL, D_MODEL)
            pltpu.make_async_copy(ybuf.at[s, pl.ds(r, 1), :],
                                  y_hbm.at[pl.ds(tok, 1), pl.ds(col, D_MODEL)], ssem.at[s]).start()
        _tile_rows(q0_ref, cnt_ref, order_ref, tile, issue, True)

    @pl.when(i == 0)
    def _():
        ybuf[0] = jnp.zeros(ybuf.shape[1:], ybuf.dtype)
        for half in range(2):
            spare = pltpu.make_async_copy(
                ybuf.at[0], y_hbm.at[pl.ds(ntok, MOE_TM), pl.ds(half * D_MODEL, D_MODEL)], ssem.at[0])
            spare.start()
            spare.wait()

    @pl.when(_is_new_expert(te_ref, i))
    def _():
        wd_scr[...] = wd_ref[...].astype(BF16)

    for s in range(2):
        @pl.when(jnp.logical_and(i < nt, i % 2 == s))
        def _(s=s):
            @pl.when(i >= 2)
            def _():
                wait_scatter(s)

            ybuf[s] = jnp.dot(h_ref[...], wd_scr[...], preferred_element_type=F32)
            start_scatter(i, s)

            @pl.when(i == nt - 1)
            def _():
                @pl.when(i >= 1)
                def _():
                    wait_scatter(1 - s)
                wait_scatter(s)


def _moe_experts(plan, xf, w_gate, w_up, w_down, layer):
    te, q0, cnt, nt, order, tok = plan
    n = xf.shape[0]
    ntiles = te.shape[0]
    wspec = lambda shape: pl.BlockSpec((None, None) + shape,
                                       lambda i, te, q0, cnt, nt, order: (layer, te[i], 0, 0))
    tile_spec = pl.BlockSpec((MOE_TM, D_EXPERT), lambda i, te, q0, cnt, nt, order: (i, 0))
    big_vmem = pltpu.CompilerParams(dimension_semantics=("arbitrary",),
                                    vmem_limit_bytes=BIG_VMEM_LIMIT_BYTES)
    h = pl.pallas_call(
        _moe_up_kernel,
        out_shape=jax.ShapeDtypeStruct((ntiles * MOE_TM, D_EXPERT), BF16),
        grid_spec=pltpu.PrefetchScalarGridSpec(
            num_scalar_prefetch=5,
            grid=(ntiles,),
            in_specs=[pl.BlockSpec(memory_space=pl.ANY),
                      wspec((D_MODEL, D_EXPERT)), wspec((D_MODEL, D_EXPERT))],
            out_specs=tile_spec,
            scratch_shapes=[pltpu.VMEM((2, MOE_TM, D_MODEL), F32),
                            pltpu.VMEM((D_MODEL, D_EXPERT), BF16),
                            pltpu.VMEM((D_MODEL, D_EXPERT), BF16),
                            pltpu.SemaphoreType.DMA((2,))]),
        compiler_params=big_vmem,
        name="moe_gate_up",
    )(te, q0, cnt, nt, tok, xf, w_gate, w_up)
    return pl.pallas_call(
        functools.partial(_moe_down_kernel, ntok=n),
        out_shape=jax.ShapeDtypeStruct((n + MOE_TM, 2 * D_MODEL), F32),
        grid_spec=pltpu.PrefetchScalarGridSpec(
            num_scalar_prefetch=5,
            grid=(ntiles,),
            in_specs=[tile_spec, wspec((D_EXPERT, D_MODEL))],
            out_specs=pl.BlockSpec(memory_space=pl.ANY),
            scratch_shapes=[pltpu.VMEM((2, MOE_TM, D_MODEL), F32),
                            pltpu.VMEM((D_EXPERT, D_MODEL), BF16),
                            pltpu.SemaphoreType.DMA((2,))]),
        compiler_params=_cparams("arbitrary"),
        name="moe_down",
    )(te, q0, cnt, nt, order, h, w_down)


def _lookup(table, idx):
    onehot = idx[:, None] == jnp.arange(table.shape[0], dtype=jnp.int32)[None, :]
    return jnp.sum(jnp.where(onehot, table[None, :], 0), axis=1)


def _moe_plan(eid, tm):
    npair = eid.size
    ntiles = npair // tm + N_EXPERTS
    flat = eid.reshape(-1)
    order = jnp.argsort(flat, stable=True).astype(jnp.int32)
    experts = jnp.arange(N_EXPERTS, dtype=jnp.int32)
    counts = jnp.sum((flat[:, None] == experts[None, :]).astype(jnp.int32), axis=0)
    padded = ((counts + tm - 1) // tm) * tm
    pend = jnp.cumsum(padded)
    pstart = pend - padded
    start = jnp.cumsum(counts) - counts
    nt = pend[-1] // tm
    tiles = jnp.arange(ntiles, dtype=jnp.int32)
    tile_start = tiles * tm
    te = jnp.minimum(jnp.sum((tile_start[:, None] >= pend[None, :]).astype(jnp.int32), axis=1), N_EXPERTS - 1)
    in_use = tiles < nt
    off = tile_start - _lookup(pstart, te)
    q0 = jnp.where(in_use, _lookup(start, te) + off, 0)
    cnt = jnp.where(in_use, jnp.clip(_lookup(counts, te) - off, 0, tm), 0)
    te_last = jnp.sum(jnp.where(tiles == nt - 1, te, 0))
    te = jnp.where(in_use, te, te_last)
    i32 = lambda a: a.astype(jnp.int32)
    order = jnp.concatenate([order, jnp.zeros((tm,), jnp.int32)])
    return i32(te), i32(q0), i32(cnt), i32(nt).reshape(1), order, order // 2


def _combine_kernel(yp_ref, r_ref, x_ref, gate_ref, g_ref, b_ref, *rest, fused):
    if fused:
        nsh_ref, nsc_ref, o_ref, on_ref = rest
    else:
        (o_ref,), nsh_ref, nsc_ref, on_ref = rest, None, None, None
    w0 = r_ref[:, 2:3]
    w1 = r_ref[:, 3:4]
    m = w0 * yp_ref[:, 0:D_MODEL] + w1 * yp_ref[:, D_MODEL:2 * D_MODEL]
    r = ALPHA * x_ref[...] + gate_ref[...] * m
    _post_norm_store(r, g_ref, b_ref, nsh_ref, nsc_ref, o_ref, on_ref)


def _combine_call(ypair2, route, x, mod, gate_chunk, g, b, row_block0, next_mod=None, next_dtype=None):
    t = x.shape[0]
    tm = 256
    row = pl.BlockSpec((tm, D_MODEL), lambda i: (i, 0))
    vec = pl.BlockSpec((1, D_MODEL), lambda i: (0, 0))
    chunk = lambda c: pl.BlockSpec((1, D_MODEL), lambda i: (0, c))
    in_specs = [pl.BlockSpec((tm, 2 * D_MODEL), lambda i: (i + row_block0, 0)),
                pl.BlockSpec((tm, LANES), lambda i: (i + row_block0, 0)),
                row, chunk(gate_chunk), vec, vec]
    args = [ypair2, route, x, mod, g.reshape(1, D_MODEL), b.reshape(1, D_MODEL)]
    out_shape = jax.ShapeDtypeStruct((t, D_MODEL), F32)
    out_specs = row
    if next_mod is not None:
        nmod, sh, sc = next_mod
        in_specs += [chunk(sh), chunk(sc)]
        args += [nmod, nmod]
        out_shape = (out_shape, jax.ShapeDtypeStruct((t, D_MODEL), next_dtype))
        out_specs = (row, row)
    return pl.pallas_call(
        functools.partial(_combine_kernel, fused=next_mod is not None),
        out_shape=out_shape,
        grid=(t // tm,),
        in_specs=in_specs,
        out_specs=out_specs,
        compiler_params=_cparams("arbitrary"),
        name="moe_combine_norm",
    )(*args)


def _rope_tables(s):
    t = jnp.arange(s, dtype=jnp.int32)
    rows = (t // GRID_W).astype(F32)
    cols = (t % GRID_W).astype(F32)
    half = HEAD_DIM // 2
    inv = ROPE_THETA ** (-jnp.arange(0, half, 2, dtype=F32) / half)
    ar = rows[:, None] * inv[None, :]
    ac = cols[:, None] * inv[None, :]
    cos = jnp.concatenate([jnp.cos(ar), jnp.cos(ar), jnp.cos(ac), jnp.cos(ac)], axis=-1)
    sin_signed = jnp.concatenate([-jnp.sin(ar), jnp.sin(ar), -jnp.sin(ac), jnp.sin(ac)], axis=-1)
    return cos, sin_signed


def _moe(xf, w_grp, b_grp, w_exp, b_exp, w_gate, w_up, w_down, layer):
    pad = LANES - N_GROUPS - N_EXPERTS
    wr = jnp.concatenate([w_grp, w_exp, jnp.zeros((D_MODEL, pad), F32)], axis=1).astype(BF16)
    br = jnp.concatenate([b_grp, b_exp, jnp.zeros((pad,), F32)]).reshape(1, LANES)
    route = _router_call(xf, wr, br)
    eid = route[:, 0:2].astype(jnp.int32)
    y2 = _moe_experts(_moe_plan(eid, MOE_TM), xf, w_gate, w_up, w_down, layer)
    return y2, route


def kernel(x, c, ctx, c_ctx, w_mod, b_mod, w_in, b_in, attn_sink, lru_conv_w, lru_conv_b,
           lru_w_a, lru_b_a, lru_w_x, lru_b_x, lru_lam, sc_conv_w, w_branch, w_out,
           ln1_g, ln1_b, ln2_g, ln2_b, w_grp, b_grp, w_exp, b_exp, w_gate, w_up, w_down):
    assert x.shape == (1, SEQ, D_MODEL) and ctx.shape == (1, CTX_LEN, D_MODEL)
    xs = x[0]
    xc = ctx[0]
    cos, sin_signed = _rope_tables(SEQ)
    c8 = jnp.concatenate([c, c_ctx[None, :], jnp.zeros((SUBLANES - 2, D_MODEL), F32)], axis=0)
    zero_h0 = jnp.zeros((2, LRU_WIDTH), F32)

    mods = [_mod_call(c8, w_mod, b_mod[l], l) for l in range(DEPTH)]
    u = u_ctx = None
    for l in range(DEPTH):
        last = l == DEPTH - 1
        m_lat = mods[l][0:1]
        m_ctx = mods[l][1:2]

        if l == 0:
            u = _lnmod_call(xs, m_lat, 0, 1, BF16)
            u_ctx = _lnmod_call(xc, m_ctx, 0, 1, BF16)
        p = _proj_call(u, w_in, b_in[l], cos, sin_signed, l, rope=True, tm=1024)
        p_ctx = _proj_call(u_ctx, w_in, b_in[l], cos, sin_signed, l, rope=False, tm=CTX_LEN)

        y_att = _attn_call(p, p_ctx, attn_sink[l])
        lru_args = (lru_conv_w[l], lru_conv_b[l], lru_w_a[l], lru_b_a[l], lru_w_x[l], lru_b_x[l], lru_lam[l])
        hcf, hcr = _lru_call(p_ctx, zero_h0, *lru_args, tc=CTX_LEN)
        h0 = jnp.concatenate([hcf[CTX_LEN - 1:CTX_LEN], hcr[0:1]], axis=0)
        hf, hr = _lru_call(p, h0, *lru_args, tc=256)
        y_lru, y_sc = _prep_call(p, hf, hr, sc_conv_w[l], tm=256)
        z = _mergez_call(y_att, y_lru, y_sc, w_branch, p, l, tm=1024)

        if last:
            xs, xf = _outnorm_call(z, w_out, l, xs, m_lat, 2, ln1_g[l], ln1_b[l],
                                   next_mod=(m_lat, 3, 4), next_dtype=F32)
            ypair2, route = _moe(xf, w_grp[l], b_grp[l], w_exp[l], b_exp[l], w_gate, w_up, w_down, l)
            xs = _combine_call(ypair2, route, xs, m_lat, 5, ln2_g[l], ln2_b[l], 0)
        else:
            xs = _outnorm_call(z, w_out, l, xs, m_lat, 2, ln1_g[l], ln1_b[l])
            yc_att = _ctx_attn_call(p_ctx, attn_sink[l])
            yc_lru, yc_sc = _prep_call(p_ctx, hcf, hcr, sc_conv_w[l], tm=CTX_LEN)
            zc = _mergez_call(yc_att, yc_lru, yc_sc, w_branch, p_ctx, l, tm=CTX_LEN)
            xc = _outnorm_call(zc, w_out, l, xc, m_ctx, 2, ln1_g[l], ln1_b[l])
            xf = _lnmod_pair_call(xs, m_lat, xc, m_ctx, 3, 4, F32)
            ypair2, route = _moe(xf, w_grp[l], b_grp[l], w_exp[l], b_exp[l], w_gate, w_up, w_down, l)
            n_lat = mods[l + 1][0:1]
            n_ctx = mods[l + 1][1:2]
            xs, u = _combine_call(ypair2, route, xs, m_lat, 5, ln2_g[l], ln2_b[l], 0,
                                  next_mod=(n_lat, 0, 1), next_dtype=BF16)
            xc, u_ctx = _combine_call(ypair2, route, xc, m_ctx, 5, ln2_g[l], ln2_b[l], SEQ // 256,
                                      next_mod=(n_ctx, 0, 1), next_dtype=BF16)
    return xs[None]
```

```python
import functools

import jax
import jax.numpy as jnp
from jax import lax
from jax.experimental import pallas as pl
from jax.experimental.pallas import tpu as pltpu

F32 = jnp.float32
BF16 = jnp.bfloat16

D_MODEL = 2048
SEQ = 8192
CTX_LEN = 256
DEPTH = 2
GRID_W = 64

N_HEADS = 8
N_KV_HEADS = 2
HEAD_DIM = 128
Q_GROUPS = N_HEADS // N_KV_HEADS
WINDOW = 128
BLOCK = 128
ROPE_THETA = 10000.0
Q_SCALE = HEAD_DIM ** -0.5

LRU_WIDTH = 1024
LRU_BLOCKS = 8
LRU_BLOCK_DIM = 128
LRU_C = 8.0
SC_WIDTH = 1024
N_BRANCH = 3
BRANCH_W = 1024

Q_W = N_HEADS * HEAD_DIM
KV_W = N_KV_HEADS * HEAD_DIM
OFF_K = Q_W
OFF_V = OFF_K + KV_W
OFF_LX = OFF_V + KV_W
OFF_LG = OFF_LX + LRU_WIDTH
OFF_SB = OFF_LG + LRU_WIDTH
OFF_SC = OFF_SB + SC_WIDTH
OFF_SX = OFF_SC + SC_WIDTH
OFF_GATE = OFF_SX + SC_WIDTH
PROJ_W = OFF_GATE + N_BRANCH * D_MODEL

N_GROUPS = 4
EXPERTS_PER_GROUP = 8
N_EXPERTS = N_GROUPS * EXPERTS_PER_GROUP
D_EXPERT = 1024

ALPHA = (2 * DEPTH) ** 0.25
LN_EPS = 1e-5

VMEM_LIMIT_BYTES = 52 * 1024 * 1024
BIG_VMEM_LIMIT_BYTES = 58 * 1024 * 1024
LANES = 128
SUBLANES = 8

HALF_W = 512
PROJ_TN = 1280
MOE_TM = 256


def _cparams(*sem):
    return pltpu.CompilerParams(dimension_semantics=sem, vmem_limit_bytes=VMEM_LIMIT_BYTES)


def _layer_norm(x):
    mu = jnp.mean(x, axis=-1, keepdims=True)
    xc = x - mu
    var = jnp.mean(xc * xc, axis=-1, keepdims=True)
    return xc * lax.rsqrt(var + LN_EPS)


def _mod_kernel(c_ref, w_ref, b_ref, o_ref):
    a = jax.nn.silu(c_ref[...]).astype(BF16)
    o_ref[...] = jnp.dot(a, w_ref[...].astype(BF16), preferred_element_type=F32) + b_ref[...]


def _mod_call(c8, w_all, b, layer):
    n = w_all.shape[2]
    tn = 1024
    return pl.pallas_call(
        _mod_kernel,
        out_shape=jax.ShapeDtypeStruct((SUBLANES, n), F32),
        grid=(n // tn,),
        in_specs=[pl.BlockSpec((SUBLANES, D_MODEL), lambda j: (0, 0)),
                  pl.BlockSpec((None, D_MODEL, tn), lambda j: (layer, 0, j)),
                  pl.BlockSpec((1, tn), lambda j: (0, j))],
        out_specs=pl.BlockSpec((SUBLANES, tn), lambda j: (0, j)),
        compiler_params=_cparams("arbitrary"),
        name="mod_vectors",
    )(c8, w_all, b.reshape(1, n))


def _lnmod_kernel(x_ref, sh_ref, sc_ref, o_ref):
    y = _layer_norm(x_ref[...])
    o_ref[...] = (y * (1.0 + sc_ref[...]) + sh_ref[...]).astype(o_ref.dtype)


def _lnmod_call(x, mod, shift_chunk, scale_chunk, out_dtype):
    t = x.shape[0]
    tm = 256
    return pl.pallas_call(
        _lnmod_kernel,
        out_shape=jax.ShapeDtypeStruct((t, D_MODEL), out_dtype),
        grid=(t // tm,),
        in_specs=[pl.BlockSpec((tm, D_MODEL), lambda i: (i, 0)),
                  pl.BlockSpec((1, D_MODEL), lambda i: (0, shift_chunk)),
                  pl.BlockSpec((1, D_MODEL), lambda i: (0, scale_chunk))],
        out_specs=pl.BlockSpec((tm, D_MODEL), lambda i: (i, 0)),
        compiler_params=_cparams("arbitrary"),
        name="ln_modulate",
    )(x, mod, mod)


def _lnmod_pair_kernel(xa_ref, xb_ref, sha_ref, sca_ref, shb_ref, scb_ref, o_ref, *, na):
    i = pl.program_id(0)

    @pl.when(i < na)
    def _():
        _lnmod_kernel(xa_ref, sha_ref, sca_ref, o_ref)

    @pl.when(i >= na)
    def _():
        _lnmod_kernel(xb_ref, shb_ref, scb_ref, o_ref)


def _lnmod_pair_call(xa, mod_a, xb, mod_b, shift_chunk, scale_chunk, out_dtype):
    tm = 256
    na = xa.shape[0] // tm
    nb = xb.shape[0] // tm
    vec = lambda chunk: pl.BlockSpec((1, D_MODEL), lambda i: (0, chunk))
    return pl.pallas_call(
        functools.partial(_lnmod_pair_kernel, na=na),
        out_shape=jax.ShapeDtypeStruct(((na + nb) * tm, D_MODEL), out_dtype),
        grid=(na + nb,),
        in_specs=[pl.BlockSpec((tm, D_MODEL), lambda i: (jnp.minimum(i, na - 1), 0)),
                  pl.BlockSpec((tm, D_MODEL), lambda i: (jnp.maximum(i - na, 0), 0)),
                  vec(shift_chunk), vec(scale_chunk), vec(shift_chunk), vec(scale_chunk)],
        out_specs=pl.BlockSpec((tm, D_MODEL), lambda i: (i, 0)),
        compiler_params=_cparams("arbitrary"),
        name="ln_modulate_pair",
    )(xa, xb, mod_a, mod_a, mod_b, mod_b)


def _rope_head(xh, cos, sin_signed, first_half):
    partner = jnp.where(first_half, pltpu.roll(xh, LANES - 32, 1), pltpu.roll(xh, 32, 1))
    return xh * cos + partner * sin_signed


def _proj_kernel(u_ref, w_ref, b_ref, cos_ref, sin_ref, o_ref, w_scr, *, rope):
    j = pl.program_id(0)
    i = pl.program_id(1)

    @pl.when(i == 0)
    def _():
        w_scr[...] = w_ref[...].astype(BF16)

    acc = jnp.dot(u_ref[...], w_scr[...], preferred_element_type=F32) + b_ref[...]
    if not rope:
        o_ref[...] = acc
        return

    @pl.when(j != 0)
    def _():
        o_ref[...] = acc

    @pl.when(j == 0)
    def _():
        cos = cos_ref[...]
        sin_signed = sin_ref[...]
        lane = lax.broadcasted_iota(jnp.int32, cos.shape, 1)
        first_half = (lane & 63) < 32
        for h in range(N_HEADS + N_KV_HEADS):
            sl = slice(h * HEAD_DIM, (h + 1) * HEAD_DIM)
            r = _rope_head(acc[:, sl], cos, sin_signed, first_half)
            if h < N_HEADS:
                r = r * Q_SCALE
            o_ref[:, sl] = r


def _proj_call(u, w_all, b, cos, sin_signed, layer, *, rope, tm):
    t = u.shape[0]
    n = w_all.shape[2]
    kern = functools.partial(_proj_kernel, rope=rope)
    return pl.pallas_call(
        kern,
        out_shape=jax.ShapeDtypeStruct((t, n), F32),
        grid=(n // PROJ_TN, t // tm),
        in_specs=[pl.BlockSpec((tm, D_MODEL), lambda j, i: (i, 0)),
                  pl.BlockSpec((None, D_MODEL, PROJ_TN), lambda j, i: (layer, 0, j)),
                  pl.BlockSpec((1, PROJ_TN), lambda j, i: (0, j)),
                  pl.BlockSpec((tm, HEAD_DIM), lambda j, i: (i, 0)),
                  pl.BlockSpec((tm, HEAD_DIM), lambda j, i: (i, 0))],
        out_specs=pl.BlockSpec((tm, PROJ_TN), lambda j, i: (i, j)),
        scratch_shapes=[pltpu.VMEM((D_MODEL, PROJ_TN), BF16)],
        compiler_params=pltpu.CompilerParams(dimension_semantics=("arbitrary", "arbitrary"),
                                             vmem_limit_bytes=BIG_VMEM_LIMIT_BYTES),
        name="in_proj_rope" if rope else "in_proj_ctx",
    )(u, w_all, b.reshape(1, n), cos, sin_signed)


def _softmax_pv(q, k_all, v_all, bias, sink_col):
    s = lax.dot_general(q, k_all, (((1,), (1,)), ((), ())), preferred_element_type=F32)
    if bias is not None:
        s = s + bias
    m = jnp.maximum(jnp.max(s, axis=-1, keepdims=True), sink_col)
    e = jnp.exp(s - m)
    den = jnp.sum(e, axis=-1, keepdims=True) + jnp.exp(sink_col - m)
    p = (e / den).astype(BF16)
    return jnp.dot(p, v_all, preferred_element_type=F32)


def _sink_column(sink_ref, kv, rows):
    row = lax.broadcasted_iota(jnp.int32, (Q_GROUPS * rows, 1), 0)
    col = jnp.full((Q_GROUPS * rows, 1), sink_ref[kv * Q_GROUPS + Q_GROUPS - 1], F32)
    for g in range(Q_GROUPS - 2, -1, -1):
        col = jnp.where(row < (g + 1) * rows, sink_ref[kv * Q_GROUPS + g], col)
    return col


def _attn_kernel(sink_ref, q_ref, kp_ref, kc_ref, kn_ref, vp_ref, vc_ref, vn_ref,
                 kx_ref, vx_ref, o_ref, *, seq):
    n = pl.program_id(0)
    n_loc = 3 * BLOCK
    n_keys = n_loc + CTX_LEN
    qi = lax.broadcasted_iota(jnp.int32, (BLOCK, n_keys), 0)
    kj = lax.broadcasted_iota(jnp.int32, (BLOCK, n_keys), 1)
    rel = kj - BLOCK - qi
    kpos = n * BLOCK - BLOCK + kj
    bad = jnp.where(kj < n_loc,
                    jnp.where(jnp.abs(rel) > WINDOW, 1, 0) + jnp.where(kpos < 0, 1, 0)
                    + jnp.where(kpos >= seq, 1, 0), 0)
    bias1 = jnp.where(bad > 0, -jnp.inf, 0.0).astype(F32)
    bias = jnp.concatenate([bias1] * Q_GROUPS, axis=0)
    for kv in range(N_KV_HEADS):
        ks = slice(kv * HEAD_DIM, (kv + 1) * HEAD_DIM)
        k_all = jnp.concatenate([kp_ref[:, ks], kc_ref[:, ks], kn_ref[:, ks], kx_ref[:, ks]],
                                axis=0).astype(BF16)
        v_all = jnp.concatenate([vp_ref[:, ks], vc_ref[:, ks], vn_ref[:, ks], vx_ref[:, ks]],
                                axis=0).astype(BF16)
        q = jnp.concatenate(
            [q_ref[:, (kv * Q_GROUPS + g) * HEAD_DIM:(kv * Q_GROUPS + g + 1) * HEAD_DIM]
             for g in range(Q_GROUPS)], axis=0).astype(BF16)
        o = _softmax_pv(q, k_all, v_all, bias, _sink_column(sink_ref, kv, BLOCK))
        for g in range(Q_GROUPS):
            h = kv * Q_GROUPS + g
            o_ref[:, h * HEAD_DIM:(h + 1) * HEAD_DIM] = o[g * BLOCK:(g + 1) * BLOCK].astype(o_ref.dtype)


def _attn_call(p, p_ctx, sink):
    s = p.shape[0]
    nb = s // BLOCK
    kcol = OFF_K // KV_W
    vcol = OFF_V // KV_W
    kv_spec = lambda f, c: pl.BlockSpec((BLOCK, KV_W), lambda n: (f(n), c))
    prev = lambda n: jnp.maximum(n - 1, 0)
    cur = lambda n: n
    nxt = lambda n: jnp.minimum(n + 1, nb - 1)
    return pl.pallas_call(
        functools.partial(_attn_kernel, seq=s),
        out_shape=jax.ShapeDtypeStruct((s, Q_W), BF16),
        grid=(nb,),
        in_specs=[pl.BlockSpec(memory_space=pltpu.SMEM),
                  pl.BlockSpec((BLOCK, Q_W), lambda n: (n, 0)),
                  kv_spec(prev, kcol), kv_spec(cur, kcol), kv_spec(nxt, kcol),
                  kv_spec(prev, vcol), kv_spec(cur, vcol), kv_spec(nxt, vcol),
                  pl.BlockSpec((CTX_LEN, KV_W), lambda n: (0, kcol)),
                  pl.BlockSpec((CTX_LEN, KV_W), lambda n: (0, vcol))],
        out_specs=pl.BlockSpec((BLOCK, Q_W), lambda n: (n, 0)),
        compiler_params=_cparams("arbitrary"),
        name="window_attention",
    )(sink, p, p, p, p, p, p, p, p_ctx, p_ctx)


def _ctx_attn_kernel(sink_ref, q_ref, k_ref, v_ref, o_ref):
    rows = q_ref.shape[0]
    for kv in range(N_KV_HEADS):
        ks = slice(kv * HEAD_DIM, (kv + 1) * HEAD_DIM)
        q = jnp.concatenate(
            [q_ref[:, (kv * Q_GROUPS + g) * HEAD_DIM:(kv * Q_GROUPS + g + 1) * HEAD_DIM] * Q_SCALE
             for g in range(Q_GROUPS)], axis=0).astype(BF16)
        o = _softmax_pv(q, k_ref[:, ks].astype(BF16), v_ref[:, ks].astype(BF16), None,
                        _sink_column(sink_ref, kv, rows))
        for g in range(Q_GROUPS):
            h = kv * Q_GROUPS + g
            o_ref[:, h * HEAD_DIM:(h + 1) * HEAD_DIM] = o[g * rows:(g + 1) * rows].astype(o_ref.dtype)


def _ctx_attn_call(p_ctx, sink):
    lc = p_ctx.shape[0]
    return pl.pallas_call(
        _ctx_attn_kernel,
        out_shape=jax.ShapeDtypeStruct((lc, Q_W), BF16),
        grid=(1,),
        in_specs=[pl.BlockSpec(memory_space=pltpu.SMEM),
                  pl.BlockSpec((lc, Q_W), lambda n: (0, 0)),
                  pl.BlockSpec((lc, KV_W), lambda n: (0, OFF_K // KV_W)),
                  pl.BlockSpec((lc, KV_W), lambda n: (0, OFF_V // KV_W))],
        out_specs=pl.BlockSpec((lc, Q_W), lambda n: (0, 0)),
        compiler_params=_cparams("arbitrary"),
        name="context_attention",
    )(sink, p_ctx, p_ctx, p_ctx)


def _lru_conv(ext_scr, cur_ref, prev_ref, next_ref, is_first, is_last, cw_ref, cb_ref, tc):
    ext_scr[0:SUBLANES, :] = jnp.where(is_first, 0.0, prev_ref[...])
    ext_scr[SUBLANES:SUBLANES + tc, :] = cur_ref[...]
    ext_scr[SUBLANES + tc:, :] = jnp.where(is_last, 0.0, next_ref[...])
    out = cb_ref[...] + cw_ref[0:1, :] * ext_scr[SUBLANES - 1:SUBLANES - 1 + tc, :]
    for j in range(1, 4):
        out = out + cw_ref[j:j + 1, :] * ext_scr[SUBLANES - 1 + j:SUBLANES - 1 + j + tc, :]
    return out


def _lru_gates(xb, d, wa_ref, ba_ref, wx_ref, bx_ref, lam_ref, a_scr, u_scr):
    xb_bf = xb.astype(BF16)
    for b in range(HALF_W // LRU_BLOCK_DIM):
        sl = slice(b * LRU_BLOCK_DIM, (b + 1) * LRU_BLOCK_DIM)
        xs = xb_bf[:, sl]
        r = jax.nn.sigmoid(jnp.dot(xs, wa_ref[d, b].astype(BF16), preferred_element_type=F32)
                           + ba_ref[d:d + 1, sl])
        ig = jax.nn.sigmoid(jnp.dot(xs, wx_ref[d, b].astype(BF16), preferred_element_type=F32)
                            + bx_ref[d:d + 1, sl])
        log_a = -LRU_C * r * jax.nn.softplus(-lam_ref[d:d + 1, sl])
        a_scr[d, :, sl] = jnp.exp(log_a)
        u_scr[d, :, sl] = jnp.sqrt(1.0 - jnp.exp(2.0 * log_a)) * ig * xb[:, sl]


def _lru_kernel(fc_ref, fp_ref, fn_ref, rc_ref, rp_ref, rn_ref,
                cw_ref, cb_ref, wa_ref, ba_ref, wx_ref, bx_ref, lam_ref, h0_ref,
                hf_ref, hr_ref, ext_scr, a_scr, u_scr, cf_scr, cr_scr, *, tc, nchunks):
    i = pl.program_id(1)
    w = HALF_W

    @pl.when(i == 0)
    def _():
        cf_scr[...] = jnp.broadcast_to(h0_ref[0:1, :], (SUBLANES, w))
        cr_scr[...] = jnp.broadcast_to(h0_ref[1:2, :], (SUBLANES, w))

    xb_f = _lru_conv(ext_scr, fc_ref, fp_ref, fn_ref, i == 0, i == nchunks - 1, cw_ref, cb_ref, tc)
    _lru_gates(xb_f, 0, wa_ref, ba_ref, wx_ref, bx_ref, lam_ref, a_scr, u_scr)
    xb_r = _lru_conv(ext_scr, rc_ref, rp_ref, rn_ref, i == nchunks - 1, i == 0, cw_ref, cb_ref, tc)
    _lru_gates(xb_r, 1, wa_ref, ba_ref, wx_ref, bx_ref, lam_ref, a_scr, u_scr)

    row = lax.broadcasted_iota(jnp.int32, (SUBLANES, w), 0)
    ntile = tc // SUBLANES

    def body(j, carry):
        cf, cr = carry
        off = pl.multiple_of(j * SUBLANES, SUBLANES)
        a = a_scr[0, pl.ds(off, SUBLANES), :]
        u = u_scr[0, pl.ds(off, SUBLANES), :]
        for s in (1, 2, 4):
            m = row >= s
            u = jnp.where(m, u + a * pltpu.roll(u, s, 0), u)
            a = jnp.where(m, a * pltpu.roll(a, s, 0), a)
        h = u + a * cf
        hf_ref[pl.ds(off, SUBLANES), :] = h
        cf = jnp.broadcast_to(h[SUBLANES - 1:SUBLANES, :], (SUBLANES, w))
        offr = pl.multiple_of((ntile - 1 - j) * SUBLANES, SUBLANES)
        a = a_scr[1, pl.ds(offr, SUBLANES), :]
        u = u_scr[1, pl.ds(offr, SUBLANES), :]
        for s in (1, 2, 4):
            m = row < SUBLANES - s
            u = jnp.where(m, u + a * pltpu.roll(u, SUBLANES - s, 0), u)
            a = jnp.where(m, a * pltpu.roll(a, SUBLANES - s, 0), a)
        h = u + a * cr
        hr_ref[pl.ds(offr, SUBLANES), :] = h
        cr = jnp.broadcast_to(h[0:1, :], (SUBLANES, w))
        return cf, cr

    cf, cr = lax.fori_loop(0, ntile, body, (cf_scr[...], cr_scr[...]))
    cf_scr[...] = cf
    cr_scr[...] = cr


def _lru_call(p, h0, cw, cb, wa, ba, wx, bx, lam, *, tc):
    t = p.shape[0]
    nchunks = t // tc
    w = HALF_W
    c0 = OFF_LX // w
    r8 = tc // SUBLANES
    last8 = t // SUBLANES - 1
    fwd = lambda i: i
    rev = lambda i: nchunks - 1 - i
    cur_spec = lambda f: pl.BlockSpec((tc, w), lambda h, i: (f(i), c0 + h))
    prev_spec = lambda f: pl.BlockSpec((SUBLANES, w), lambda h, i: (jnp.maximum(f(i) * r8 - 1, 0), c0 + h))
    next_spec = lambda f: pl.BlockSpec((SUBLANES, w), lambda h, i: (jnp.minimum((f(i) + 1) * r8, last8), c0 + h))
    vec_spec = lambda rows: pl.BlockSpec((rows, w), lambda h, i: (0, h))
    gate_w_spec = pl.BlockSpec((2, w // LRU_BLOCK_DIM, LRU_BLOCK_DIM, LRU_BLOCK_DIM),
                               lambda h, i: (0, h, 0, 0))
    kern = functools.partial(_lru_kernel, tc=tc, nchunks=nchunks)
    return pl.pallas_call(
        kern,
        out_shape=(jax.ShapeDtypeStruct((t, LRU_WIDTH), F32), jax.ShapeDtypeStruct((t, LRU_WIDTH), F32)),
        grid=(LRU_WIDTH // w, nchunks),
        in_specs=[cur_spec(fwd), prev_spec(fwd), next_spec(fwd),
                  cur_spec(rev), prev_spec(rev), next_spec(rev),
                  vec_spec(4), vec_spec(1), gate_w_spec, vec_spec(2), gate_w_spec, vec_spec(2),
                  vec_spec(2), vec_spec(2)],
        out_specs=(pl.BlockSpec((tc, w), lambda h, i: (i, h)),
                   pl.BlockSpec((tc, w), lambda h, i: (nchunks - 1 - i, h))),
        scratch_shapes=[pltpu.VMEM((tc + 2 * SUBLANES, w), F32),
                        pltpu.VMEM((2, tc, w), F32), pltpu.VMEM((2, tc, w), F32),
                        pltpu.VMEM((SUBLANES, w), F32), pltpu.VMEM((SUBLANES, w), F32)],
        compiler_params=_cparams("arbitrary", "arbitrary"),
        name="rg_lru",
    )(p, p, p, p, p, p, cw, cb.reshape(1, LRU_WIDTH), wa, ba.reshape(2, LRU_WIDTH),
      wx, bx.reshape(2, LRU_WIDTH), lam, h0)


def _prep_kernel(lg_ref, hf_ref, hr_ref, sb_ref, scc_ref, scp_ref, scn_ref,
                 sxc_ref, sxp_ref, sxn_ref, w3_ref, ylru_ref, ysc_ref, ext_scr, *, tm, ntiles):
    i = pl.program_id(0)
    ylru_ref[...] = ((hf_ref[...] + hr_ref[...]) * jax.nn.gelu(lg_ref[...])).astype(ylru_ref.dtype)
    ext_scr[0:SUBLANES, :] = jnp.where(i == 0, 0.0, scp_ref[...] * sxp_ref[...])
    ext_scr[SUBLANES:SUBLANES + tm, :] = scc_ref[...] * sxc_ref[...]
    ext_scr[SUBLANES + tm:, :] = jnp.where(i == ntiles - 1, 0.0, scn_ref[...] * sxn_ref[...])
    conv = w3_ref[0:1, :] * ext_scr[SUBLANES - 1:SUBLANES - 1 + tm, :]
    for j in range(1, 3):
        conv = conv + w3_ref[j:j + 1, :] * ext_scr[SUBLANES - 1 + j:SUBLANES - 1 + j + tm, :]
    ysc_ref[...] = (sb_ref[...] * conv).astype(ysc_ref.dtype)


def _prep_call(p, hf, hr, w3, *, tm):
    t = p.shape[0]
    ntiles = t // tm
    w = HALF_W
    r8 = tm // SUBLANES
    last8 = t // SUBLANES - 1
    col = lambda off: pl.BlockSpec((tm, w), lambda i, h: (i, off // w + h))
    colp = lambda off: pl.BlockSpec((SUBLANES, w), lambda i, h: (jnp.maximum(i * r8 - 1, 0), off // w + h))
    coln = lambda off: pl.BlockSpec((SUBLANES, w), lambda i, h: (jnp.minimum((i + 1) * r8, last8), off // w + h))
    own = pl.BlockSpec((tm, w), lambda i, h: (i, h))
    kern = functools.partial(_prep_kernel, tm=tm, ntiles=ntiles)
    return pl.pallas_call(
        kern,
        out_shape=(jax.ShapeDtypeStruct((t, LRU_WIDTH), BF16), jax.ShapeDtypeStruct((t, SC_WIDTH), BF16)),
        grid=(ntiles, LRU_WIDTH // w),
        in_specs=[col(OFF_LG), own, own, col(OFF_SB),
                  col(OFF_SC), colp(OFF_SC), coln(OFF_SC),
                  col(OFF_SX), colp(OFF_SX), coln(OFF_SX),
                  pl.BlockSpec((3, w), lambda i, h: (0, h))],
        out_specs=(own, own),
        scratch_shapes=[pltpu.VMEM((tm + 2 * SUBLANES, w), F32)],
        compiler_params=_cparams("arbitrary", "arbitrary"),
        name="branch_prep",
    )(p, hf, hr, p, p, p, p, p, p, p, w3)


def _mergez_kernel(b0_ref, b1_ref, b2_ref, wb_ref, g0_ref, g1_ref, g2_ref, z_ref):
    acc = None
    for j, (b_ref, g_ref) in enumerate(((b0_ref, g0_ref), (b1_ref, g1_ref), (b2_ref, g2_ref))):
        t = jax.nn.sigmoid(g_ref[...]) * jnp.dot(b_ref[...], wb_ref[j].astype(BF16),
                                                  preferred_element_type=F32)
        acc = t if acc is None else acc + t
    z_ref[...] = acc.astype(z_ref.dtype)


def _mergez_call(y_att, y_lru, y_sc, wb_all, p, layer, *, tm):
    t = p.shape[0]
    tn = 512
    br = pl.BlockSpec((tm, BRANCH_W), lambda i, j: (i, 0))
    gate = lambda k: pl.BlockSpec((tm, tn), lambda i, j: (i, (OFF_GATE + k * D_MODEL) // tn + j))
    return pl.pallas_call(
        _mergez_kernel,
        out_shape=jax.ShapeDtypeStruct((t, D_MODEL), BF16),
        grid=(t // tm, D_MODEL // tn),
        in_specs=[br, br, br,
                  pl.BlockSpec((None, N_BRANCH, BRANCH_W, tn), lambda i, j: (layer, 0, 0, j)),
                  gate(0), gate(1), gate(2)],
        out_specs=pl.BlockSpec((tm, tn), lambda i, j: (i, j)),
        compiler_params=_cparams("arbitrary", "arbitrary"),
        name="branch_merge",
    )(y_att, y_lru, y_sc, wb_all, p, p, p)


def _post_norm_store(r, g_ref, b_ref, nsh_ref, nsc_ref, o_ref, on_ref):
    o = _layer_norm(r) * g_ref[...] + b_ref[...]
    o_ref[...] = o
    if on_ref is not None:
        on_ref[...] = (_layer_norm(o) * (1.0 + nsc_ref[...]) + nsh_ref[...]).astype(on_ref.dtype)


def _outnorm_kernel(z_ref, w_ref, x_ref, gate_ref, g_ref, b_ref, *rest, fused):
    if fused:
        nsh_ref, nsc_ref, o_ref, on_ref, w_scr = rest
    else:
        (o_ref, w_scr), nsh_ref, nsc_ref, on_ref = rest, None, None, None

    @pl.when(pl.program_id(0) == 0)
    def _():
        w_scr[...] = w_ref[...].astype(BF16)

    y = jnp.dot(z_ref[...], w_scr[...], preferred_element_type=F32)
    r = ALPHA * x_ref[...] + gate_ref[...] * y
    _post_norm_store(r, g_ref, b_ref, nsh_ref, nsc_ref, o_ref, on_ref)


def _outnorm_call(z, w_all, layer, x, mod, gate_chunk, g, b, next_mod=None, next_dtype=None):
    t = z.shape[0]
    tm = 256
    row = pl.BlockSpec((tm, D_MODEL), lambda i: (i, 0))
    vec = pl.BlockSpec((1, D_MODEL), lambda i: (0, 0))
    chunk = lambda c: pl.BlockSpec((1, D_MODEL), lambda i: (0, c))
    in_specs = [row,
                pl.BlockSpec((None, D_MODEL, D_MODEL), lambda i: (layer, 0, 0),
                             pipeline_mode=pl.Buffered(1)),
                row, chunk(gate_chunk), vec, vec]
    args = [z, w_all, x, mod, g.reshape(1, D_MODEL), b.reshape(1, D_MODEL)]
    out_shape = jax.ShapeDtypeStruct((t, D_MODEL), F32)
    out_specs = row
    if next_mod is not None:
        nmod, sh, sc = next_mod
        in_specs += [chunk(sh), chunk(sc)]
        args += [nmod, nmod]
        out_shape = (out_shape, jax.ShapeDtypeStruct((t, D_MODEL), next_dtype))
        out_specs = (row, row)
    return pl.pallas_call(
        functools.partial(_outnorm_kernel, fused=next_mod is not None),
        out_shape=out_shape,
        grid=(t // tm,),
        in_specs=in_specs,
        out_specs=out_specs,
        scratch_shapes=[pltpu.VMEM((D_MODEL, D_MODEL), BF16)],
        compiler_params=_cparams("arbitrary"),
        name="out_proj_norm",
    )(*args)


def _router_kernel(x_ref, w_ref, b_ref, o_ref):
    logits = jnp.dot(x_ref[...].astype(BF16), w_ref[...], preferred_element_type=F32) + b_ref[...]
    lane = lax.broadcasted_iota(jnp.int32, logits.shape, 1)
    neg = -jnp.inf
    is_grp = lane < N_GROUPS
    gl = jnp.where(is_grp, logits, neg)
    gmax = jnp.max(gl, axis=-1, keepdims=True)
    g_idx = jnp.min(jnp.where(gl == gmax, lane, LANES), axis=-1, keepdims=True)
    gsum = jnp.sum(jnp.where(is_grp, jnp.exp(gl - gmax), 0.0), axis=-1, keepdims=True)
    g_w = 1.0 / gsum
    lo = N_GROUPS + g_idx * EXPERTS_PER_GROUP
    in_grp = jnp.where(lane >= lo, 1, 0) * jnp.where(lane < lo + EXPERTS_PER_GROUP, 1, 0)
    sel = jnp.where(in_grp > 0, logits, neg)
    v0 = jnp.max(sel, axis=-1, keepdims=True)
    i0 = jnp.min(jnp.where(sel == v0, lane, LANES), axis=-1, keepdims=True)
    sel1 = jnp.where(lane == i0, neg, sel)
    v1 = jnp.max(sel1, axis=-1, keepdims=True)
    i1 = jnp.min(jnp.where(sel1 == v1, lane, LANES), axis=-1, keepdims=True)
    e1 = jnp.exp(v1 - v0)
    den = 1.0 + e1
    w0 = (1.0 / den) * g_w
    w1 = (e1 / den) * g_w
    out = jnp.where(lane == 0, (i0 - N_GROUPS).astype(F32),
                    jnp.where(lane == 1, (i1 - N_GROUPS).astype(F32),
                              jnp.where(lane == 2, w0, jnp.where(lane == 3, w1, 0.0))))
    o_ref[...] = out


def _router_call(xf, wr_bf, br):
    t = xf.shape[0]
    tm = 256
    return pl.pallas_call(
        _router_kernel,
        out_shape=jax.ShapeDtypeStruct((t, LANES), F32),
        grid=(t // tm,),
        in_specs=[pl.BlockSpec((tm, D_MODEL), lambda i: (i, 0)),
                  pl.BlockSpec((D_MODEL, LANES), lambda i: (0, 0)),
                  pl.BlockSpec((1, LANES), lambda i: (0, 0))],
        out_specs=pl.BlockSpec((tm, LANES), lambda i: (i, 0)),
        compiler_params=_cparams("arbitrary"),
        name="moe_router",
    )(xf, wr_bf, br)


MOE_UNROLL = 8


def _tile_rows(q0_ref, cnt_ref, order_ref, tile, fn, static_rows):
    base = q0_ref[tile]
    cnt = cnt_ref[tile]
    if static_rows:
        for r in range(MOE_TM):
            fn(r, order_ref[base + r], r < cnt)
        return

    def body(blk, c):
        for k in range(MOE_UNROLL):
            r = blk * MOE_UNROLL + k
            fn(r, order_ref[base + r], r < cnt)
        return c
    lax.fori_loop(0, MOE_TM // MOE_UNROLL, body, 0)


def _is_new_expert(te_ref, i):
    return jnp.logical_or(i == 0, te_ref[i] != te_ref[jnp.maximum(i - 1, 0)])


def _moe_kernel(te_ref, nx_ref, q0_ref, cnt_ref, nt_ref, order_ref, tok_ref,
                x_hbm, wg_hbm, wu_hbm, wd_hbm, y_hbm,
                xbuf, ybuf, wg_stage, wu_stage, wd_stage, wg_scr, wu_scr, wd_scr,
                gsem, ssem, wsem, *, layer, ntok):
    i = pl.program_id(0)
    nt = nt_ref[0]

    def weight_copies(e):
        return (pltpu.make_async_copy(wg_hbm.at[layer, e], wg_stage, wsem.at[0]),
                pltpu.make_async_copy(wu_hbm.at[layer, e], wu_stage, wsem.at[1]),
                pltpu.make_async_copy(wd_hbm.at[layer, e], wd_stage, wsem.at[2]))

    def start_gather(tile, s, static_rows):
        def issue(r, tok, valid):
            pltpu.make_async_copy(x_hbm.at[pl.ds(tok, 1), :], xbuf.at[s, pl.ds(r, 1), :],
                                  gsem.at[s]).start()
        _tile_rows(q0_ref, cnt_ref, tok_ref, tile, issue, static_rows)

    def wait_gather(s):
        pltpu.make_async_copy(x_hbm.at[pl.ds(0, MOE_TM), :], xbuf.at[s], gsem.at[s]).wait()

    def wait_scatter(s):
        pltpu.make_async_copy(ybuf.at[s], y_hbm.at[pl.ds(0, MOE_TM), pl.ds(0, D_MODEL)], ssem.at[s]).wait()

    def start_scatter(tile, s):
        def issue(r, pair, valid):
            dst = jnp.where(valid, pair, 2 * ntok + s * MOE_TM + r)
            tok = lax.shift_right_logical(dst, 1)
            col = pl.multiple_of((dst & 1) * D_MODEL, D_MODEL)
            pltpu.make_async_copy(ybuf.at[s, pl.ds(r, 1), :],
                                  y_hbm.at[pl.ds(tok, 1), pl.ds(col, D_MODEL)], ssem.at[s]).start()
        _tile_rows(q0_ref, cnt_ref, order_ref, tile, issue, True)

    @pl.when(i == 0)
    def _():
        start_gather(0, 0, False)
        for copy in weight_copies(te_ref[0]):
            copy.start()
        ybuf[0] = jnp.zeros(ybuf.shape[1:], ybuf.dtype)
        for half in range(2):
            spare = pltpu.make_async_copy(
                ybuf.at[0], y_hbm.at[pl.ds(ntok, MOE_TM), pl.ds(half * D_MODEL, D_MODEL)], ssem.at[0])
            spare.start()
            spare.wait()

    @pl.when(_is_new_expert(te_ref, i))
    def _():
        for copy in weight_copies(te_ref[i]):
            copy.wait()
        wg_scr[...] = wg_stage[...].astype(BF16)
        wu_scr[...] = wu_stage[...].astype(BF16)
        wd_scr[...] = wd_stage[...].astype(BF16)

        @pl.when(nx_ref[i] >= 0)
        def _():
            for copy in weight_copies(nx_ref[i]):
                copy.start()

    for s in range(2):
        @pl.when(jnp.logical_and(i < nt, i % 2 == s))
        def _(s=s):
            wait_gather(s)

            @pl.when(i >= 2)
            def _():
                wait_scatter(s)

            start_gather(jnp.minimum(i + 1, nt - 1), 1 - s, True)
            x = xbuf[s].astype(BF16)
            hg = jnp.dot(x, wg_scr[...], preferred_element_type=F32)
            hu = jnp.dot(x, wu_scr[...], preferred_element_type=F32)
            h = (jax.nn.silu(hg) * hu).astype(BF16)
            ybuf[s] = jnp.dot(h, wd_scr[...], preferred_element_type=F32)
            start_scatter(i, s)

            @pl.when(i == nt - 1)
            def _():
                wait_gather(1 - s)

                @pl.when(i >= 1)
                def _():
                    wait_scatter(1 - s)
                wait_scatter(s)


def _moe_experts(plan, xf, w_gate, w_up, w_down, layer):
    te, nxt, q0, cnt, nt, order, tok = plan
    n = xf.shape[0]
    ntiles = te.shape[0]
    hbm = pl.BlockSpec(memory_space=pl.ANY)
    return pl.pallas_call(
        functools.partial(_moe_kernel, layer=layer, ntok=n),
        out_shape=jax.ShapeDtypeStruct((n + MOE_TM, 2 * D_MODEL), F32),
        grid_spec=pltpu.PrefetchScalarGridSpec(
            num_scalar_prefetch=7,
            grid=(ntiles,),
            in_specs=[hbm, hbm, hbm, hbm],
            out_specs=hbm,
            scratch_shapes=[pltpu.VMEM((2, MOE_TM, D_MODEL), F32),
                            pltpu.VMEM((2, MOE_TM, D_MODEL), F32),
                            pltpu.VMEM((D_MODEL, D_EXPERT), F32),
                            pltpu.VMEM((D_MODEL, D_EXPERT), F32),
                            pltpu.VMEM((D_EXPERT, D_MODEL), F32),
                            pltpu.VMEM((D_MODEL, D_EXPERT), BF16),
                            pltpu.VMEM((D_MODEL, D_EXPERT), BF16),
                            pltpu.VMEM((D_EXPERT, D_MODEL), BF16),
                            pltpu.SemaphoreType.DMA((2,)),
                            pltpu.SemaphoreType.DMA((2,)),
                            pltpu.SemaphoreType.DMA((3,))]),
        compiler_params=pltpu.CompilerParams(dimension_semantics=("arbitrary",),
                                             vmem_limit_bytes=BIG_VMEM_LIMIT_BYTES),
        name="moe_experts",
    )(te, nxt, q0, cnt, nt, order, tok, xf, w_gate, w_up, w_down)


def _lookup(table, idx):
    onehot = idx[:, None] == jnp.arange(table.shape[0], dtype=jnp.int32)[None, :]
    return jnp.sum(jnp.where(onehot, table[None, :], 0), axis=1)


def _moe_plan(eid, tm):
    npair = eid.size
    ntiles = npair // tm + N_EXPERTS
    flat = eid.reshape(-1)
    order = jnp.argsort(flat, stable=True).astype(jnp.int32)
    experts = jnp.arange(N_EXPERTS, dtype=jnp.int32)
    counts = jnp.sum((flat[:, None] == experts[None, :]).astype(jnp.int32), axis=0)
    padded = ((counts + tm - 1) // tm) * tm
    pend = jnp.cumsum(padded)
    pstart = pend - padded
    start = jnp.cumsum(counts) - counts
    nt = pend[-1] // tm
    tiles = jnp.arange(ntiles, dtype=jnp.int32)
    tile_start = tiles * tm
    te = jnp.minimum(jnp.sum((tile_start[:, None] >= pend[None, :]).astype(jnp.int32), axis=1), N_EXPERTS - 1)
    in_use = tiles < nt
    off = tile_start - _lookup(pstart, te)
    q0 = jnp.where(in_use, _lookup(start, te) + off, 0)
    cnt = jnp.where(in_use, jnp.clip(_lookup(counts, te) - off, 0, tm), 0)
    te_last = jnp.sum(jnp.where(tiles == nt - 1, te, 0))
    te = jnp.where(in_use, te, te_last)
    later = (experts[None, :] > experts[:, None]) & (counts[None, :] > 0)
    next_expert = jnp.min(jnp.where(later, experts[None, :], N_EXPERTS), axis=1)
    next_expert = jnp.where(next_expert < N_EXPERTS, next_expert, -1)
    nxt = _lookup(next_expert, te)
    i32 = lambda a: a.astype(jnp.int32)
    order = jnp.concatenate([order, jnp.zeros((tm,), jnp.int32)])
    return i32(te), i32(nxt), i32(q0), i32(cnt), i32(nt).reshape(1), order, order // 2


def _combine_kernel(yp_ref, r_ref, x_ref, gate_ref, g_ref, b_ref, *rest, fused):
    if fused:
        nsh_ref, nsc_ref, o_ref, on_ref = rest
    else:
        (o_ref,), nsh_ref, nsc_ref, on_ref = rest, None, None, None
    w0 = r_ref[:, 2:3]
    w1 = r_ref[:, 3:4]
    m = w0 * yp_ref[:, 0:D_MODEL] + w1 * yp_ref[:, D_MODEL:2 * D_MODEL]
    r = ALPHA * x_ref[...] + gate_ref[...] * m
    _post_norm_store(r, g_ref, b_ref, nsh_ref, nsc_ref, o_ref, on_ref)


def _combine_call(ypair2, route, x, mod, gate_chunk, g, b, row_block0, next_mod=None, next_dtype=None):
    t = x.shape[0]
    tm = 256
    row = pl.BlockSpec((tm, D_MODEL), lambda i: (i, 0))
    vec = pl.BlockSpec((1, D_MODEL), lambda i: (0, 0))
    chunk = lambda c: pl.BlockSpec((1, D_MODEL), lambda i: (0, c))
    in_specs = [pl.BlockSpec((tm, 2 * D_MODEL), lambda i: (i + row_block0, 0)),
                pl.BlockSpec((tm, LANES), lambda i: (i + row_block0, 0)),
                row, chunk(gate_chunk), vec, vec]
    args = [ypair2, route, x, mod, g.reshape(1, D_MODEL), b.reshape(1, D_MODEL)]
    out_shape = jax.ShapeDtypeStruct((t, D_MODEL), F32)
    out_specs = row
    if next_mod is not None:
        nmod, sh, sc = next_mod
        in_specs += [chunk(sh), chunk(sc)]
        args += [nmod, nmod]
        out_shape = (out_shape, jax.ShapeDtypeStruct((t, D_MODEL), next_dtype))
        out_specs = (row, row)
    return pl.pallas_call(
        functools.partial(_combine_kernel, fused=next_mod is not None),
        out_shape=out_shape,
        grid=(t // tm,),
        in_specs=in_specs,
        out_specs=out_specs,
        compiler_params=_cparams("arbitrary"),
        name="moe_combine_norm",
    )(*args)


def _rope_tables(s):
    t = jnp.arange(s, dtype=jnp.int32)
    rows = (t // GRID_W).astype(F32)
    cols = (t % GRID_W).astype(F32)
    half = HEAD_DIM // 2
    inv = ROPE_THETA ** (-jnp.arange(0, half, 2, dtype=F32) / half)
    ar = rows[:, None] * inv[None, :]
    ac = cols[:, None] * inv[None, :]
    cos = jnp.concatenate([jnp.cos(ar), jnp.cos(ar), jnp.cos(ac), jnp.cos(ac)], axis=-1)
    sin_signed = jnp.concatenate([-jnp.sin(ar), jnp.sin(ar), -jnp.sin(ac), jnp.sin(ac)], axis=-1)
    return cos, sin_signed


def _moe(xf, w_grp, b_grp, w_exp, b_exp, w_gate, w_up, w_down, layer):
    pad = LANES - N_GROUPS - N_EXPERTS
    wr = jnp.concatenate([w_grp, w_exp, jnp.zeros((D_MODEL, pad), F32)], axis=1).astype(BF16)
    br = jnp.concatenate([b_grp, b_exp, jnp.zeros((pad,), F32)]).reshape(1, LANES)
    route = _router_call(xf, wr, br)
    eid = route[:, 0:2].astype(jnp.int32)
    y2 = _moe_experts(_moe_plan(eid, MOE_TM), xf, w_gate, w_up, w_down, layer)
    return y2, route


def kernel(x, c, ctx, c_ctx, w_mod, b_mod, w_in, b_in, attn_sink, lru_conv_w, lru_conv_b,
           lru_w_a, lru_b_a, lru_w_x, lru_b_x, lru_lam, sc_conv_w, w_branch, w_out,
           ln1_g, ln1_b, ln2_g, ln2_b, w_grp, b_grp, w_exp, b_exp, w_gate, w_up, w_down):
    assert x.shape == (1, SEQ, D_MODEL) and ctx.shape == (1, CTX_LEN, D_MODEL)
    xs = x[0]
    xc = ctx[0]
    cos, sin_signed = _rope_tables(SEQ)
    c8 = jnp.concatenate([c, c_ctx[None, :], jnp.zeros((SUBLANES - 2, D_MODEL), F32)], axis=0)
    zero_h0 = jnp.zeros((2, LRU_WIDTH), F32)

    mods = [_mod_call(c8, w_mod, b_mod[l], l) for l in range(DEPTH)]
    u = u_ctx = None
    for l in range(DEPTH):
        last = l == DEPTH - 1
        m_lat = mods[l][0:1]
        m_ctx = mods[l][1:2]

        if l == 0:
            u = _lnmod_call(xs, m_lat, 0, 1, BF16)
            u_ctx = _lnmod_call(xc, m_ctx, 0, 1, BF16)
        p = _proj_call(u, w_in, b_in[l], cos, sin_signed, l, rope=True, tm=1024)
        p_ctx = _proj_call(u_ctx, w_in, b_in[l], cos, sin_signed, l, rope=False, tm=CTX_LEN)

        y_att = _attn_call(p, p_ctx, attn_sink[l])
        lru_args = (lru_conv_w[l], lru_conv_b[l], lru_w_a[l], lru_b_a[l], lru_w_x[l], lru_b_x[l], lru_lam[l])
        hcf, hcr = _lru_call(p_ctx, zero_h0, *lru_args, tc=CTX_LEN)
        h0 = jnp.concatenate([hcf[CTX_LEN - 1:CTX_LEN], hcr[0:1]], axis=0)
        hf, hr = _lru_call(p, h0, *lru_args, tc=256)
        y_lru, y_sc = _prep_call(p, hf, hr, sc_conv_w[l], tm=256)
        z = _mergez_call(y_att, y_lru, y_sc, w_branch, p, l, tm=1024)

        if last:
            xs, xf = _outnorm_call(z, w_out, l, xs, m_lat, 2, ln1_g[l], ln1_b[l],
                                   next_mod=(m_lat, 3, 4), next_dtype=F32)
            ypair2, route = _moe(xf, w_grp[l], b_grp[l], w_exp[l], b_exp[l], w_gate, w_up, w_down, l)
            xs = _combine_call(ypair2, route, xs, m_lat, 5, ln2_g[l], ln2_b[l], 0)
        else:
            xs = _outnorm_call(z, w_out, l, xs, m_lat, 2, ln1_g[l], ln1_b[l])
            yc_att = _ctx_attn_call(p_ctx, attn_sink[l])
            yc_lru, yc_sc = _prep_call(p_ctx, hcf, hcr, sc_conv_w[l], tm=CTX_LEN)
            zc = _mergez_call(yc_att, yc_lru, yc_sc, w_branch, p_ctx, l, tm=CTX_LEN)
            xc = _outnorm_call(zc, w_out, l, xc, m_ctx, 2, ln1_g[l], ln1_b[l])
            xf = _lnmod_pair_call(xs, m_lat, xc, m_ctx, 3, 4, F32)
            ypair2, route = _moe(xf, w_grp[l], b_grp[l], w_exp[l], b_exp[l], w_gate, w_up, w_down, l)
            n_lat = mods[l + 1][0:1]
            n_ctx = mods[l + 1][1:2]
            xs, u = _combine_call(ypair2, route, xs, m_lat, 5, ln2_g[l], ln2_b[l], 0,
                                  next_mod=(n_lat, 0, 1), next_dtype=BF16)
            xc, u_ctx = _combine_call(ypair2, route, xc, m_ctx, 5, ln2_g[l], ln2_b[l], SEQ // 256,
                                      next_mod=(n_ctx, 0, 1), next_dtype=BF16)
    return xs[None]
```

```python
import functools

import jax
import jax.numpy as jnp
import numpy as np
from jax import lax
from jax.experimental import pallas as pl
from jax.experimental.pallas import tpu as pltpu

F32 = jnp.float32
BF16 = jnp.bfloat16

D_MODEL = 2048
SEQ = 8192
CTX_LEN = 256
DEPTH = 2
GRID_W = 64

N_HEADS = 8
N_KV_HEADS = 2
HEAD_DIM = 128
Q_GROUPS = N_HEADS // N_KV_HEADS
WINDOW = 128
BLOCK = 128
ROPE_THETA = 10000.0
Q_SCALE = HEAD_DIM ** -0.5

LRU_WIDTH = 1024
LRU_BLOCKS = 8
LRU_BLOCK_DIM = 128
LRU_C = 8.0
SC_WIDTH = 1024
N_BRANCH = 3
BRANCH_W = 1024

Q_W = N_HEADS * HEAD_DIM
KV_W = N_KV_HEADS * HEAD_DIM
OFF_K = Q_W
OFF_V = OFF_K + KV_W
OFF_LX = OFF_V + KV_W
OFF_LG = OFF_LX + LRU_WIDTH
OFF_SB = OFF_LG + LRU_WIDTH
OFF_SC = OFF_SB + SC_WIDTH
OFF_SX = OFF_SC + SC_WIDTH
OFF_GATE = OFF_SX + SC_WIDTH
PROJ_W = OFF_GATE + N_BRANCH * D_MODEL

N_GROUPS = 4
EXPERTS_PER_GROUP = 8
N_EXPERTS = N_GROUPS * EXPERTS_PER_GROUP
D_EXPERT = 1024

ALPHA = (2 * DEPTH) ** 0.25
LN_EPS = 1e-5

VMEM_LIMIT_BYTES = 52 * 1024 * 1024
BIG_VMEM_LIMIT_BYTES = 58 * 1024 * 1024
LANES = 128
SUBLANES = 8

HALF_W = 512
PROJ_TN = 1280
MOE_TM = 256


def _cparams(*sem):
    return pltpu.CompilerParams(dimension_semantics=sem, vmem_limit_bytes=VMEM_LIMIT_BYTES)


def _layer_norm(x):
    mu = jnp.mean(x, axis=-1, keepdims=True)
    xc = x - mu
    var = jnp.mean(xc * xc, axis=-1, keepdims=True)
    return xc * lax.rsqrt(var + LN_EPS)


def _mod_kernel(c_ref, w_ref, b_ref, o_ref):
    a = jax.nn.silu(c_ref[...]).astype(BF16)
    o_ref[...] = jnp.dot(a, w_ref[...].astype(BF16), preferred_element_type=F32) + b_ref[...]


def _mod_call(c8, w_all, b, layer):
    n = w_all.shape[2]
    tn = 1024
    return pl.pallas_call(
        _mod_kernel,
        out_shape=jax.ShapeDtypeStruct((SUBLANES, n), F32),
        grid=(n // tn,),
        in_specs=[pl.BlockSpec((SUBLANES, D_MODEL), lambda j: (0, 0)),
                  pl.BlockSpec((None, D_MODEL, tn), lambda j: (layer, 0, j)),
                  pl.BlockSpec((1, tn), lambda j: (0, j))],
        out_specs=pl.BlockSpec((SUBLANES, tn), lambda j: (0, j)),
        compiler_params=_cparams("arbitrary"),
        name="mod_vectors",
    )(c8, w_all, b.reshape(1, n))


def _lnmod_kernel(x_ref, sh_ref, sc_ref, o_ref):
    y = _layer_norm(x_ref[...])
    o_ref[...] = (y * (1.0 + sc_ref[...]) + sh_ref[...]).astype(o_ref.dtype)


def _lnmod_call(x, mod, shift_chunk, scale_chunk, out_dtype):
    t = x.shape[0]
    tm = 256
    return pl.pallas_call(
        _lnmod_kernel,
        out_shape=jax.ShapeDtypeStruct((t, D_MODEL), out_dtype),
        grid=(t // tm,),
        in_specs=[pl.BlockSpec((tm, D_MODEL), lambda i: (i, 0)),
                  pl.BlockSpec((1, D_MODEL), lambda i: (0, shift_chunk)),
                  pl.BlockSpec((1, D_MODEL), lambda i: (0, scale_chunk))],
        out_specs=pl.BlockSpec((tm, D_MODEL), lambda i: (i, 0)),
        compiler_params=_cparams("arbitrary"),
        name="ln_modulate",
    )(x, mod, mod)


def _lnmod_pair_kernel(xa_ref, xb_ref, sha_ref, sca_ref, shb_ref, scb_ref, wr_ref, br_ref,
                       o_ref, route_ref, *, na):
    i = pl.program_id(0)

    def one(x_ref, sh_ref, sc_ref):
        y = _layer_norm(x_ref[...]) * (1.0 + sc_ref[...]) + sh_ref[...]
        o_ref[...] = y
        route_ref[...] = _route(y, wr_ref, br_ref)

    @pl.when(i < na)
    def _():
        one(xa_ref, sha_ref, sca_ref)

    @pl.when(i >= na)
    def _():
        one(xb_ref, shb_ref, scb_ref)


def _lnmod_pair_call(xa, mod_a, xb, mod_b, shift_chunk, scale_chunk, router):
    tm = 256
    na = xa.shape[0] // tm
    nb = xb.shape[0] // tm
    rows = (na + nb) * tm
    vec = lambda chunk: pl.BlockSpec((1, D_MODEL), lambda i: (0, chunk))
    r_args, r_specs = _router_operands(*router)
    return pl.pallas_call(
        functools.partial(_lnmod_pair_kernel, na=na),
        out_shape=(jax.ShapeDtypeStruct((rows, D_MODEL), F32), jax.ShapeDtypeStruct((rows, LANES), F32)),
        grid=(na + nb,),
        in_specs=[pl.BlockSpec((tm, D_MODEL), lambda i: (jnp.minimum(i, na - 1), 0)),
                  pl.BlockSpec((tm, D_MODEL), lambda i: (jnp.maximum(i - na, 0), 0)),
                  vec(shift_chunk), vec(scale_chunk), vec(shift_chunk), vec(scale_chunk)] + r_specs,
        out_specs=(pl.BlockSpec((tm, D_MODEL), lambda i: (i, 0)),
                   pl.BlockSpec((tm, LANES), lambda i: (i, 0))),
        compiler_params=_cparams("arbitrary"),
        name="ln_modulate_route",
    )(xa, xb, mod_a, mod_a, mod_b, mod_b, *r_args)


def _rope_head(xh, cos, sin_signed, first_half):
    partner = jnp.where(first_half, pltpu.roll(xh, LANES - 32, 1), pltpu.roll(xh, 32, 1))
    return xh * cos + partner * sin_signed


def _proj_kernel(u_ref, w_ref, b_ref, cos_ref, sin_ref, o_ref, w_scr, *, rope):
    j = pl.program_id(0)
    i = pl.program_id(1)

    @pl.when(i == 0)
    def _():
        w_scr[...] = w_ref[...].astype(BF16)

    def project():
        return jnp.dot(u_ref[...], w_scr[...], preferred_element_type=F32) + b_ref[...]

    if not rope:
        o_ref[...] = project()
        return

    @pl.when(j != 0)
    def _():
        o_ref[...] = project()

    @pl.when(j == 0)
    def _():
        acc = project()
        cos = cos_ref[...]
        sin_signed = sin_ref[...]
        lane = lax.broadcasted_iota(jnp.int32, cos.shape, 1)
        first_half = (lane & 63) < 32
        for h in range(N_HEADS + N_KV_HEADS):
            sl = slice(h * HEAD_DIM, (h + 1) * HEAD_DIM)
            r = _rope_head(acc[:, sl], cos, sin_signed, first_half)
            if h < N_HEADS:
                r = r * Q_SCALE
            o_ref[:, sl] = r


def _proj_call(u, w_all, b, cos, sin_signed, layer, *, rope, tm):
    t = u.shape[0]
    n = w_all.shape[2]
    kern = functools.partial(_proj_kernel, rope=rope)
    return pl.pallas_call(
        kern,
        out_shape=jax.ShapeDtypeStruct((t, n), F32),
        grid=(n // PROJ_TN, t // tm),
        in_specs=[pl.BlockSpec((tm, D_MODEL), lambda j, i: (i, 0)),
                  pl.BlockSpec((None, D_MODEL, PROJ_TN), lambda j, i: (layer, 0, j)),
                  pl.BlockSpec((1, PROJ_TN), lambda j, i: (0, j)),
                  pl.BlockSpec((tm, HEAD_DIM), lambda j, i: (i, 0)),
                  pl.BlockSpec((tm, HEAD_DIM), lambda j, i: (i, 0))],
        out_specs=pl.BlockSpec((tm, PROJ_TN), lambda j, i: (i, j)),
        scratch_shapes=[pltpu.VMEM((D_MODEL, PROJ_TN), BF16)],
        compiler_params=pltpu.CompilerParams(dimension_semantics=("arbitrary", "arbitrary"),
                                             vmem_limit_bytes=BIG_VMEM_LIMIT_BYTES),
        name="in_proj_rope" if rope else "in_proj_ctx",
    )(u, w_all, b.reshape(1, n), cos, sin_signed)


def _softmax_pv(q, k_all, v_all, bias, sink_col):
    s = lax.dot_general(q, k_all, (((1,), (1,)), ((), ())), preferred_element_type=F32)
    if bias is not None:
        s = s + bias
    m = jnp.maximum(jnp.max(s, axis=-1, keepdims=True), sink_col)
    e = jnp.exp(s - m)
    den = jnp.sum(e, axis=-1, keepdims=True) + jnp.exp(sink_col - m)
    p = (e / den).astype(BF16)
    return jnp.dot(p, v_all, preferred_element_type=F32)


def _sink_column(sink_ref, kv, rows):
    row = lax.broadcasted_iota(jnp.int32, (Q_GROUPS * rows, 1), 0)
    col = jnp.full((Q_GROUPS * rows, 1), sink_ref[kv * Q_GROUPS + Q_GROUPS - 1], F32)
    for g in range(Q_GROUPS - 2, -1, -1):
        col = jnp.where(row < (g + 1) * rows, sink_ref[kv * Q_GROUPS + g], col)
    return col


def _attn_kernel(sink_ref, q_ref, kp_ref, kc_ref, kn_ref, vp_ref, vc_ref, vn_ref,
                 kx_ref, vx_ref, o_ref, *, seq):
    n = pl.program_id(0)
    n_loc = 3 * BLOCK
    n_keys = n_loc + CTX_LEN
    qi = lax.broadcasted_iota(jnp.int32, (BLOCK, n_keys), 0)
    kj = lax.broadcasted_iota(jnp.int32, (BLOCK, n_keys), 1)
    rel = kj - BLOCK - qi
    kpos = n * BLOCK - BLOCK + kj
    bad = jnp.where(kj < n_loc,
                    jnp.where(jnp.abs(rel) > WINDOW, 1, 0) + jnp.where(kpos < 0, 1, 0)
                    + jnp.where(kpos >= seq, 1, 0), 0)
    bias1 = jnp.where(bad > 0, -jnp.inf, 0.0).astype(F32)
    bias = jnp.concatenate([bias1] * Q_GROUPS, axis=0)
    for kv in range(N_KV_HEADS):
        ks = slice(kv * HEAD_DIM, (kv + 1) * HEAD_DIM)
        k_all = jnp.concatenate([kp_ref[:, ks], kc_ref[:, ks], kn_ref[:, ks], kx_ref[:, ks]],
                                axis=0).astype(BF16)
        v_all = jnp.concatenate([vp_ref[:, ks], vc_ref[:, ks], vn_ref[:, ks], vx_ref[:, ks]],
                                axis=0).astype(BF16)
        q = jnp.concatenate(
            [q_ref[:, (kv * Q_GROUPS + g) * HEAD_DIM:(kv * Q_GROUPS + g + 1) * HEAD_DIM]
             for g in range(Q_GROUPS)], axis=0).astype(BF16)
        o = _softmax_pv(q, k_all, v_all, bias, _sink_column(sink_ref, kv, BLOCK))
        for g in range(Q_GROUPS):
            h = kv * Q_GROUPS + g
            o_ref[:, h * HEAD_DIM:(h + 1) * HEAD_DIM] = o[g * BLOCK:(g + 1) * BLOCK].astype(o_ref.dtype)


def _attn_call(p, p_ctx, sink):
    s = p.shape[0]
    nb = s // BLOCK
    kcol = OFF_K // KV_W
    vcol = OFF_V // KV_W
    kv_spec = lambda f, c: pl.BlockSpec((BLOCK, KV_W), lambda n: (f(n), c))
    prev = lambda n: jnp.maximum(n - 1, 0)
    cur = lambda n: n
    nxt = lambda n: jnp.minimum(n + 1, nb - 1)
    return pl.pallas_call(
        functools.partial(_attn_kernel, seq=s),
        out_shape=jax.ShapeDtypeStruct((s, Q_W), BF16),
        grid=(nb,),
        in_specs=[pl.BlockSpec(memory_space=pltpu.SMEM),
                  pl.BlockSpec((BLOCK, Q_W), lambda n: (n, 0)),
                  kv_spec(prev, kcol), kv_spec(cur, kcol), kv_spec(nxt, kcol),
                  kv_spec(prev, vcol), kv_spec(cur, vcol), kv_spec(nxt, vcol),
                  pl.BlockSpec((CTX_LEN, KV_W), lambda n: (0, kcol)),
                  pl.BlockSpec((CTX_LEN, KV_W), lambda n: (0, vcol))],
        out_specs=pl.BlockSpec((BLOCK, Q_W), lambda n: (n, 0)),
        compiler_params=_cparams("arbitrary"),
        name="window_attention",
    )(sink, p, p, p, p, p, p, p, p_ctx, p_ctx)


def _ctx_attn_kernel(sink_ref, q_ref, k_ref, v_ref, o_ref):
    rows = q_ref.shape[0]
    for kv in range(N_KV_HEADS):
        ks = slice(kv * HEAD_DIM, (kv + 1) * HEAD_DIM)
        q = jnp.concatenate(
            [q_ref[:, (kv * Q_GROUPS + g) * HEAD_DIM:(kv * Q_GROUPS + g + 1) * HEAD_DIM] * Q_SCALE
             for g in range(Q_GROUPS)], axis=0).astype(BF16)
        o = _softmax_pv(q, k_ref[:, ks].astype(BF16), v_ref[:, ks].astype(BF16), None,
                        _sink_column(sink_ref, kv, rows))
        for g in range(Q_GROUPS):
            h = kv * Q_GROUPS + g
            o_ref[:, h * HEAD_DIM:(h + 1) * HEAD_DIM] = o[g * rows:(g + 1) * rows].astype(o_ref.dtype)


def _ctx_attn_call(p_ctx, sink):
    lc = p_ctx.shape[0]
    return pl.pallas_call(
        _ctx_attn_kernel,
        out_shape=jax.ShapeDtypeStruct((lc, Q_W), BF16),
        grid=(1,),
        in_specs=[pl.BlockSpec(memory_space=pltpu.SMEM),
                  pl.BlockSpec((lc, Q_W), lambda n: (0, 0)),
                  pl.BlockSpec((lc, KV_W), lambda n: (0, OFF_K // KV_W)),
                  pl.BlockSpec((lc, KV_W), lambda n: (0, OFF_V // KV_W))],
        out_specs=pl.BlockSpec((lc, Q_W), lambda n: (0, 0)),
        compiler_params=_cparams("arbitrary"),
        name="context_attention",
    )(sink, p_ctx, p_ctx, p_ctx)


def _lru_conv(ext_scr, cur_ref, prev_ref, next_ref, is_first, is_last, cw_ref, cb_ref, tc):
    ext_scr[0:SUBLANES, :] = jnp.where(is_first, 0.0, prev_ref[...])
    ext_scr[SUBLANES:SUBLANES + tc, :] = cur_ref[...]
    ext_scr[SUBLANES + tc:, :] = jnp.where(is_last, 0.0, next_ref[...])
    out = cb_ref[...] + cw_ref[0:1, :] * ext_scr[SUBLANES - 1:SUBLANES - 1 + tc, :]
    for j in range(1, 4):
        out = out + cw_ref[j:j + 1, :] * ext_scr[SUBLANES - 1 + j:SUBLANES - 1 + j + tc, :]
    return out


def _lru_gates(xb, d, wa_ref, ba_ref, wx_ref, bx_ref, lam_ref, a_scr, u_scr):
    xb_bf = xb.astype(BF16)
    for b in range(HALF_W // LRU_BLOCK_DIM):
        sl = slice(b * LRU_BLOCK_DIM, (b + 1) * LRU_BLOCK_DIM)
        xs = xb_bf[:, sl]
        r = jax.nn.sigmoid(jnp.dot(xs, wa_ref[d, b].astype(BF16), preferred_element_type=F32)
                           + ba_ref[d:d + 1, sl])
        ig = jax.nn.sigmoid(jnp.dot(xs, wx_ref[d, b].astype(BF16), preferred_element_type=F32)
                            + bx_ref[d:d + 1, sl])
        log_a = -LRU_C * r * jax.nn.softplus(-lam_ref[d:d + 1, sl])
        a_scr[d, :, sl] = jnp.exp(log_a)
        u_scr[d, :, sl] = jnp.sqrt(1.0 - jnp.exp(2.0 * log_a)) * ig * xb[:, sl]


def _lru_kernel(fc_ref, fp_ref, fn_ref, rc_ref, rp_ref, rn_ref,
                cw_ref, cb_ref, wa_ref, ba_ref, wx_ref, bx_ref, lam_ref, h0_ref,
                hf_ref, hr_ref, ext_scr, a_scr, u_scr, cf_scr, cr_scr, *, tc, nchunks):
    i = pl.program_id(1)
    w = HALF_W

    @pl.when(i == 0)
    def _():
        cf_scr[...] = jnp.broadcast_to(h0_ref[0:1, :], (SUBLANES, w))
        cr_scr[...] = jnp.broadcast_to(h0_ref[1:2, :], (SUBLANES, w))

    xb_f = _lru_conv(ext_scr, fc_ref, fp_ref, fn_ref, i == 0, i == nchunks - 1, cw_ref, cb_ref, tc)
    _lru_gates(xb_f, 0, wa_ref, ba_ref, wx_ref, bx_ref, lam_ref, a_scr, u_scr)
    xb_r = _lru_conv(ext_scr, rc_ref, rp_ref, rn_ref, i == nchunks - 1, i == 0, cw_ref, cb_ref, tc)
    _lru_gates(xb_r, 1, wa_ref, ba_ref, wx_ref, bx_ref, lam_ref, a_scr, u_scr)

    row = lax.broadcasted_iota(jnp.int32, (SUBLANES, w), 0)
    ntile = tc // SUBLANES

    def body(j, carry):
        cf, cr = carry
        off = pl.multiple_of(j * SUBLANES, SUBLANES)
        a = a_scr[0, pl.ds(off, SUBLANES), :]
        u = u_scr[0, pl.ds(off, SUBLANES), :]
        for s in (1, 2, 4):
            m = row >= s
            u = jnp.where(m, u + a * pltpu.roll(u, s, 0), u)
            a = jnp.where(m, a * pltpu.roll(a, s, 0), a)
        h = u + a * cf
        hf_ref[pl.ds(off, SUBLANES), :] = h
        cf = jnp.broadcast_to(h[SUBLANES - 1:SUBLANES, :], (SUBLANES, w))
        offr = pl.multiple_of((ntile - 1 - j) * SUBLANES, SUBLANES)
        a = a_scr[1, pl.ds(offr, SUBLANES), :]
        u = u_scr[1, pl.ds(offr, SUBLANES), :]
        for s in (1, 2, 4):
            m = row < SUBLANES - s
            u = jnp.where(m, u + a * pltpu.roll(u, SUBLANES - s, 0), u)
            a = jnp.where(m, a * pltpu.roll(a, SUBLANES - s, 0), a)
        h = u + a * cr
        hr_ref[pl.ds(offr, SUBLANES), :] = h
        cr = jnp.broadcast_to(h[0:1, :], (SUBLANES, w))
        return cf, cr

    cf, cr = lax.fori_loop(0, ntile, body, (cf_scr[...], cr_scr[...]))
    cf_scr[...] = cf
    cr_scr[...] = cr


def _lru_call(p, h0, cw, cb, wa, ba, wx, bx, lam, *, tc):
    t = p.shape[0]
    nchunks = t // tc
    w = HALF_W
    c0 = OFF_LX // w
    r8 = tc // SUBLANES
    last8 = t // SUBLANES - 1
    fwd = lambda i: i
    rev = lambda i: nchunks - 1 - i
    cur_spec = lambda f: pl.BlockSpec((tc, w), lambda h, i: (f(i), c0 + h))
    prev_spec = lambda f: pl.BlockSpec((SUBLANES, w), lambda h, i: (jnp.maximum(f(i) * r8 - 1, 0), c0 + h))
    next_spec = lambda f: pl.BlockSpec((SUBLANES, w), lambda h, i: (jnp.minimum((f(i) + 1) * r8, last8), c0 + h))
    vec_spec = lambda rows: pl.BlockSpec((rows, w), lambda h, i: (0, h))
    gate_w_spec = pl.BlockSpec((2, w // LRU_BLOCK_DIM, LRU_BLOCK_DIM, LRU_BLOCK_DIM),
                               lambda h, i: (0, h, 0, 0))
    kern = functools.partial(_lru_kernel, tc=tc, nchunks=nchunks)
    return pl.pallas_call(
        kern,
        out_shape=(jax.ShapeDtypeStruct((t, LRU_WIDTH), F32), jax.ShapeDtypeStruct((t, LRU_WIDTH), F32)),
        grid=(LRU_WIDTH // w, nchunks),
        in_specs=[cur_spec(fwd), prev_spec(fwd), next_spec(fwd),
                  cur_spec(rev), prev_spec(rev), next_spec(rev),
                  vec_spec(4), vec_spec(1), gate_w_spec, vec_spec(2), gate_w_spec, vec_spec(2),
                  vec_spec(2), vec_spec(2)],
        out_specs=(pl.BlockSpec((tc, w), lambda h, i: (i, h)),
                   pl.BlockSpec((tc, w), lambda h, i: (nchunks - 1 - i, h))),
        scratch_shapes=[pltpu.VMEM((tc + 2 * SUBLANES, w), F32),
                        pltpu.VMEM((2, tc, w), F32), pltpu.VMEM((2, tc, w), F32),
                        pltpu.VMEM((SUBLANES, w), F32), pltpu.VMEM((SUBLANES, w), F32)],
        compiler_params=_cparams("arbitrary", "arbitrary"),
        name="rg_lru",
    )(p, p, p, p, p, p, cw, cb.reshape(1, LRU_WIDTH), wa, ba.reshape(2, LRU_WIDTH),
      wx, bx.reshape(2, LRU_WIDTH), lam, h0)


def _prep_kernel(lg_ref, hf_ref, hr_ref, sb_ref, scc_ref, scp_ref, scn_ref,
                 sxc_ref, sxp_ref, sxn_ref, w3_ref, ylru_ref, ysc_ref, ext_scr, *, tm, ntiles):
    i = pl.program_id(0)
    ylru_ref[...] = ((hf_ref[...] + hr_ref[...]) * jax.nn.gelu(lg_ref[...])).astype(ylru_ref.dtype)
    ext_scr[0:SUBLANES, :] = jnp.where(i == 0, 0.0, scp_ref[...] * sxp_ref[...])
    ext_scr[SUBLANES:SUBLANES + tm, :] = scc_ref[...] * sxc_ref[...]
    ext_scr[SUBLANES + tm:, :] = jnp.where(i == ntiles - 1, 0.0, scn_ref[...] * sxn_ref[...])
    conv = w3_ref[0:1, :] * ext_scr[SUBLANES - 1:SUBLANES - 1 + tm, :]
    for j in range(1, 3):
        conv = conv + w3_ref[j:j + 1, :] * ext_scr[SUBLANES - 1 + j:SUBLANES - 1 + j + tm, :]
    ysc_ref[...] = (sb_ref[...] * conv).astype(ysc_ref.dtype)


def _prep_call(p, hf, hr, w3, *, tm):
    t = p.shape[0]
    ntiles = t // tm
    w = HALF_W
    r8 = tm // SUBLANES
    last8 = t // SUBLANES - 1
    col = lambda off: pl.BlockSpec((tm, w), lambda i, h: (i, off // w + h))
    colp = lambda off: pl.BlockSpec((SUBLANES, w), lambda i, h: (jnp.maximum(i * r8 - 1, 0), off // w + h))
    coln = lambda off: pl.BlockSpec((SUBLANES, w), lambda i, h: (jnp.minimum((i + 1) * r8, last8), off // w + h))
    own = pl.BlockSpec((tm, w), lambda i, h: (i, h))
    kern = functools.partial(_prep_kernel, tm=tm, ntiles=ntiles)
    return pl.pallas_call(
        kern,
        out_shape=(jax.ShapeDtypeStruct((t, LRU_WIDTH), BF16), jax.ShapeDtypeStruct((t, SC_WIDTH), BF16)),
        grid=(ntiles, LRU_WIDTH // w),
        in_specs=[col(OFF_LG), own, own, col(OFF_SB),
                  col(OFF_SC), colp(OFF_SC), coln(OFF_SC),
                  col(OFF_SX), colp(OFF_SX), coln(OFF_SX),
                  pl.BlockSpec((3, w), lambda i, h: (0, h))],
        out_specs=(own, own),
        scratch_shapes=[pltpu.VMEM((tm + 2 * SUBLANES, w), F32)],
        compiler_params=_cparams("arbitrary", "arbitrary"),
        name="branch_prep",
    )(p, hf, hr, p, p, p, p, p, p, p, w3)


def _mergez_kernel(b0_ref, b1_ref, b2_ref, wb_ref, g0_ref, g1_ref, g2_ref, z_ref):
    acc = None
    for j, (b_ref, g_ref) in enumerate(((b0_ref, g0_ref), (b1_ref, g1_ref), (b2_ref, g2_ref))):
        t = jax.nn.sigmoid(g_ref[...]) * jnp.dot(b_ref[...], wb_ref[j].astype(BF16),
                                                  preferred_element_type=F32)
        acc = t if acc is None else acc + t
    z_ref[...] = acc.astype(z_ref.dtype)


def _mergez_call(y_att, y_lru, y_sc, wb_all, p, layer, *, tm):
    t = p.shape[0]
    tn = 512
    br = pl.BlockSpec((tm, BRANCH_W), lambda i, j: (i, 0))
    gate = lambda k: pl.BlockSpec((tm, tn), lambda i, j: (i, (OFF_GATE + k * D_MODEL) // tn + j))
    return pl.pallas_call(
        _mergez_kernel,
        out_shape=jax.ShapeDtypeStruct((t, D_MODEL), BF16),
        grid=(t // tm, D_MODEL // tn),
        in_specs=[br, br, br,
                  pl.BlockSpec((None, N_BRANCH, BRANCH_W, tn), lambda i, j: (layer, 0, 0, j)),
                  gate(0), gate(1), gate(2)],
        out_specs=pl.BlockSpec((tm, tn), lambda i, j: (i, j)),
        compiler_params=_cparams("arbitrary", "arbitrary"),
        name="branch_merge",
    )(y_att, y_lru, y_sc, wb_all, p, p, p)


def _post_norm_store(r, g_ref, b_ref, nsh_ref, nsc_ref, o_ref, on_ref):
    o = _layer_norm(r) * g_ref[...] + b_ref[...]
    o_ref[...] = o
    if on_ref is None:
        return None
    on = _layer_norm(o) * (1.0 + nsc_ref[...]) + nsh_ref[...]
    on_ref[...] = on.astype(on_ref.dtype)
    return on


def _outnorm_kernel(z_ref, w_ref, x_ref, gate_ref, g_ref, b_ref, *rest, fused, routed):
    nsh_ref = nsc_ref = on_ref = wr_ref = br_ref = route_ref = None
    if routed:
        nsh_ref, nsc_ref, wr_ref, br_ref, o_ref, on_ref, route_ref, w_scr = rest
    elif fused:
        nsh_ref, nsc_ref, o_ref, on_ref, w_scr = rest
    else:
        o_ref, w_scr = rest

    @pl.when(pl.program_id(0) == 0)
    def _():
        w_scr[...] = w_ref[...].astype(BF16)

    y = jnp.dot(z_ref[...], w_scr[...], preferred_element_type=F32)
    r = ALPHA * x_ref[...] + gate_ref[...] * y
    on = _post_norm_store(r, g_ref, b_ref, nsh_ref, nsc_ref, o_ref, on_ref)
    if routed:
        route_ref[...] = _route(on, wr_ref, br_ref)


def _outnorm_call(z, w_all, layer, x, mod, gate_chunk, g, b, next_mod=None, next_dtype=None,
                  router=None):
    t = z.shape[0]
    tm = 256
    row = pl.BlockSpec((tm, D_MODEL), lambda i: (i, 0))
    vec = pl.BlockSpec((1, D_MODEL), lambda i: (0, 0))
    chunk = lambda c: pl.BlockSpec((1, D_MODEL), lambda i: (0, c))
    in_specs = [row,
                pl.BlockSpec((None, D_MODEL, D_MODEL), lambda i: (layer, 0, 0),
                             pipeline_mode=pl.Buffered(1)),
                row, chunk(gate_chunk), vec, vec]
    args = [z, w_all, x, mod, g.reshape(1, D_MODEL), b.reshape(1, D_MODEL)]
    out_shape = jax.ShapeDtypeStruct((t, D_MODEL), F32)
    out_specs = row
    if next_mod is not None:
        nmod, sh, sc = next_mod
        in_specs += [chunk(sh), chunk(sc)]
        args += [nmod, nmod]
        out_shape = (out_shape, jax.ShapeDtypeStruct((t, D_MODEL), next_dtype))
        out_specs = (row, row)
    if router is not None:
        r_args, r_specs = _router_operands(*router)
        in_specs += r_specs
        args += r_args
        out_shape = out_shape + (jax.ShapeDtypeStruct((t, LANES), F32),)
        out_specs = out_specs + (pl.BlockSpec((tm, LANES), lambda i: (i, 0)),)
    return pl.pallas_call(
        functools.partial(_outnorm_kernel, fused=next_mod is not None, routed=router is not None),
        out_shape=out_shape,
        grid=(t // tm,),
        in_specs=in_specs,
        out_specs=out_specs,
        scratch_shapes=[pltpu.VMEM((D_MODEL, D_MODEL), BF16)],
        compiler_params=_cparams("arbitrary"),
        name="out_proj_norm",
    )(*args)


def _route(xf, w_ref, b_ref):
    logits = jnp.dot(xf.astype(BF16), w_ref[...], preferred_element_type=F32) + b_ref[...]
    lane = lax.broadcasted_iota(jnp.int32, logits.shape, 1)
    neg = -jnp.inf
    is_grp = lane < N_GROUPS
    gl = jnp.where(is_grp, logits, neg)
    gmax = jnp.max(gl, axis=-1, keepdims=True)
    g_idx = jnp.min(jnp.where(gl == gmax, lane, LANES), axis=-1, keepdims=True)
    gsum = jnp.sum(jnp.where(is_grp, jnp.exp(gl - gmax), 0.0), axis=-1, keepdims=True)
    g_w = 1.0 / gsum
    lo = N_GROUPS + g_idx * EXPERTS_PER_GROUP
    in_grp = jnp.where(lane >= lo, 1, 0) * jnp.where(lane < lo + EXPERTS_PER_GROUP, 1, 0)
    sel = jnp.where(in_grp > 0, logits, neg)
    v0 = jnp.max(sel, axis=-1, keepdims=True)
    i0 = jnp.min(jnp.where(sel == v0, lane, LANES), axis=-1, keepdims=True)
    sel1 = jnp.where(lane == i0, neg, sel)
    v1 = jnp.max(sel1, axis=-1, keepdims=True)
    i1 = jnp.min(jnp.where(sel1 == v1, lane, LANES), axis=-1, keepdims=True)
    e1 = jnp.exp(v1 - v0)
    den = 1.0 + e1
    w0 = (1.0 / den) * g_w
    w1 = (e1 / den) * g_w
    return jnp.where(lane == 0, (i0 - N_GROUPS).astype(F32),
                     jnp.where(lane == 1, (i1 - N_GROUPS).astype(F32),
                               jnp.where(lane == 2, w0, jnp.where(lane == 3, w1, 0.0))))


def _router_operands(w_grp, b_grp, w_exp, b_exp):
    pad = LANES - N_GROUPS - N_EXPERTS
    wr = jnp.concatenate([w_grp, w_exp, jnp.zeros((D_MODEL, pad), F32)], axis=1).astype(BF16)
    br = jnp.concatenate([b_grp, b_exp, jnp.zeros((pad,), F32)]).reshape(1, LANES)
    specs = [pl.BlockSpec((D_MODEL, LANES), lambda i: (0, 0)), pl.BlockSpec((1, LANES), lambda i: (0, 0))]
    return [wr, br], specs


MOE_UNROLL = 8


def _tile_rows(q0_ref, cnt_ref, order_ref, tile, fn, static_rows):
    base = q0_ref[tile]
    cnt = cnt_ref[tile]
    if static_rows:
        for r in range(MOE_TM):
            fn(r, order_ref[base + r], r < cnt)
        return

    def body(blk, c):
        for k in range(MOE_UNROLL):
            r = blk * MOE_UNROLL + k
            fn(r, order_ref[base + r], r < cnt)
        return c
    lax.fori_loop(0, MOE_TM // MOE_UNROLL, body, 0)


def _is_new_expert(te_ref, i):
    return jnp.logical_or(i == 0, te_ref[i] != te_ref[jnp.maximum(i - 1, 0)])


def _moe_kernel(te_ref, nx_ref, q0_ref, cnt_ref, nt_ref, order_ref, tok_ref,
                x_hbm, wg_hbm, wu_hbm, wd_hbm, y_hbm,
                xbuf, ybuf, wg_stage, wu_stage, wd_stage, wg_scr, wu_scr, wd_scr,
                gsem, ssem, wsem, *, layer, ntok):
    i = pl.program_id(0)
    nt = nt_ref[0]

    def weight_copies(e):
        return (pltpu.make_async_copy(wg_hbm.at[layer, e], wg_stage, wsem.at[0]),
                pltpu.make_async_copy(wu_hbm.at[layer, e], wu_stage, wsem.at[1]),
                pltpu.make_async_copy(wd_hbm.at[layer, e], wd_stage, wsem.at[2]))

    def start_gather(tile, s, static_rows):
        def issue(r, tok, valid):
            pltpu.make_async_copy(x_hbm.at[pl.ds(tok, 1), :], xbuf.at[s, pl.ds(r, 1), :],
                                  gsem.at[s]).start()
        _tile_rows(q0_ref, cnt_ref, tok_ref, tile, issue, static_rows)

    def wait_gather(s):
        pltpu.make_async_copy(x_hbm.at[pl.ds(0, MOE_TM), :], xbuf.at[s], gsem.at[s]).wait()

    def wait_scatter(s):
        pltpu.make_async_copy(ybuf.at[s], y_hbm.at[pl.ds(0, MOE_TM), pl.ds(0, D_MODEL)], ssem.at[s]).wait()

    def start_scatter(tile, s):
        def issue(r, pair, valid):
            dst = jnp.where(valid, pair, 2 * ntok + s * MOE_TM + r)
            tok = lax.shift_right_logical(dst, 1)
            col = pl.multiple_of((dst & 1) * D_MODEL, D_MODEL)
            pltpu.make_async_copy(ybuf.at[s, pl.ds(r, 1), :],
                                  y_hbm.at[pl.ds(tok, 1), pl.ds(col, D_MODEL)], ssem.at[s]).start()
        _tile_rows(q0_ref, cnt_ref, order_ref, tile, issue, True)

    @pl.when(i == 0)
    def _():
        start_gather(0, 0, False)
        for copy in weight_copies(te_ref[0]):
            copy.start()
        ybuf[0] = jnp.zeros(ybuf.shape[1:], ybuf.dtype)
        for half in range(2):
            spare = pltpu.make_async_copy(
                ybuf.at[0], y_hbm.at[pl.ds(ntok, MOE_TM), pl.ds(half * D_MODEL, D_MODEL)], ssem.at[0])
            spare.start()
            spare.wait()

    @pl.when(_is_new_expert(te_ref, i))
    def _():
        for copy in weight_copies(te_ref[i]):
            copy.wait()
        wg_scr[...] = wg_stage[...].astype(BF16)
        wu_scr[...] = wu_stage[...].astype(BF16)
        wd_scr[...] = wd_stage[...].astype(BF16)

        @pl.when(nx_ref[i] >= 0)
        def _():
            for copy in weight_copies(nx_ref[i]):
                copy.start()

    for s in range(2):
        @pl.when(jnp.logical_and(i < nt, i % 2 == s))
        def _(s=s):
            wait_gather(s)

            @pl.when(i >= 2)
            def _():
                wait_scatter(s)

            start_gather(jnp.minimum(i + 1, nt - 1), 1 - s, True)
            x = xbuf[s].astype(BF16)
            hg = jnp.dot(x, wg_scr[...], preferred_element_type=F32)
            hu = jnp.dot(x, wu_scr[...], preferred_element_type=F32)
            h = (jax.nn.silu(hg) * hu).astype(BF16)
            ybuf[s] = jnp.dot(h, wd_scr[...], preferred_element_type=F32)
            start_scatter(i, s)

            @pl.when(i == nt - 1)
            def _():
                wait_gather(1 - s)

                @pl.when(i >= 1)
                def _():
                    wait_scatter(1 - s)
                wait_scatter(s)


def _moe_experts(plan, xf, w_gate, w_up, w_down, layer):
    te, nxt, q0, cnt, nt, order, tok = plan
    n = xf.shape[0]
    ntiles = te.shape[0]
    hbm = pl.BlockSpec(memory_space=pl.ANY)
    return pl.pallas_call(
        functools.partial(_moe_kernel, layer=layer, ntok=n),
        out_shape=jax.ShapeDtypeStruct((n + MOE_TM, 2 * D_MODEL), F32),
        grid_spec=pltpu.PrefetchScalarGridSpec(
            num_scalar_prefetch=7,
            grid=(ntiles,),
            in_specs=[hbm, hbm, hbm, hbm],
            out_specs=hbm,
            scratch_shapes=[pltpu.VMEM((2, MOE_TM, D_MODEL), F32),
                            pltpu.VMEM((2, MOE_TM, D_MODEL), F32),
                            pltpu.VMEM((D_MODEL, D_EXPERT), F32),
                            pltpu.VMEM((D_MODEL, D_EXPERT), F32),
                            pltpu.VMEM((D_EXPERT, D_MODEL), F32),
                            pltpu.VMEM((D_MODEL, D_EXPERT), BF16),
                            pltpu.VMEM((D_MODEL, D_EXPERT), BF16),
                            pltpu.VMEM((D_EXPERT, D_MODEL), BF16),
                            pltpu.SemaphoreType.DMA((2,)),
                            pltpu.SemaphoreType.DMA((2,)),
                            pltpu.SemaphoreType.DMA((3,))]),
        compiler_params=pltpu.CompilerParams(dimension_semantics=("arbitrary",),
                                             vmem_limit_bytes=BIG_VMEM_LIMIT_BYTES),
        name="moe_experts",
    )(te, nxt, q0, cnt, nt, order, tok, xf, w_gate, w_up, w_down)


def _lookup(table, idx):
    onehot = idx[:, None] == jnp.arange(table.shape[0], dtype=jnp.int32)[None, :]
    return jnp.sum(jnp.where(onehot, table[None, :], 0), axis=1)


def _moe_plan(eid, tm):
    npair = eid.size
    ntiles = npair // tm + N_EXPERTS
    flat = eid.reshape(-1)
    order = jnp.argsort(flat, stable=True).astype(jnp.int32)
    experts = jnp.arange(N_EXPERTS, dtype=jnp.int32)
    counts = jnp.sum((flat[:, None] == experts[None, :]).astype(jnp.int32), axis=0)
    padded = ((counts + tm - 1) // tm) * tm
    pend = jnp.cumsum(padded)
    pstart = pend - padded
    start = jnp.cumsum(counts) - counts
    nt = pend[-1] // tm
    tiles = jnp.arange(ntiles, dtype=jnp.int32)
    tile_start = tiles * tm
    te = jnp.minimum(jnp.sum((tile_start[:, None] >= pend[None, :]).astype(jnp.int32), axis=1), N_EXPERTS - 1)
    in_use = tiles < nt
    off = tile_start - _lookup(pstart, te)
    q0 = jnp.where(in_use, _lookup(start, te) + off, 0)
    cnt = jnp.where(in_use, jnp.clip(_lookup(counts, te) - off, 0, tm), 0)
    te_last = jnp.sum(jnp.where(tiles == nt - 1, te, 0))
    te = jnp.where(in_use, te, te_last)
    later = (experts[None, :] > experts[:, None]) & (counts[None, :] > 0)
    next_expert = jnp.min(jnp.where(later, experts[None, :], N_EXPERTS), axis=1)
    next_expert = jnp.where(next_expert < N_EXPERTS, next_expert, -1)
    nxt = _lookup(next_expert, te)
    i32 = lambda a: a.astype(jnp.int32)
    order = jnp.concatenate([order, jnp.zeros((tm,), jnp.int32)])
    return i32(te), i32(nxt), i32(q0), i32(cnt), i32(nt).reshape(1), order, order // 2


def _combine_kernel(yp_ref, r_ref, x_ref, gate_ref, g_ref, b_ref, *rest, fused):
    if fused:
        nsh_ref, nsc_ref, o_ref, on_ref = rest
    else:
        (o_ref,), nsh_ref, nsc_ref, on_ref = rest, None, None, None
    w0 = r_ref[:, 2:3]
    w1 = r_ref[:, 3:4]
    m = w0 * yp_ref[:, 0:D_MODEL] + w1 * yp_ref[:, D_MODEL:2 * D_MODEL]
    r = ALPHA * x_ref[...] + gate_ref[...] * m
    _post_norm_store(r, g_ref, b_ref, nsh_ref, nsc_ref, o_ref, on_ref)


def _combine_call(ypair2, route, x, mod, gate_chunk, g, b, row_block0, next_mod=None, next_dtype=None):
    t = x.shape[0]
    tm = 256
    row = pl.BlockSpec((tm, D_MODEL), lambda i: (i, 0))
    vec = pl.BlockSpec((1, D_MODEL), lambda i: (0, 0))
    chunk = lambda c: pl.BlockSpec((1, D_MODEL), lambda i: (0, c))
    in_specs = [pl.BlockSpec((tm, 2 * D_MODEL), lambda i: (i + row_block0, 0)),
                pl.BlockSpec((tm, LANES), lambda i: (i + row_block0, 0)),
                row, chunk(gate_chunk), vec, vec]
    args = [ypair2, route, x, mod, g.reshape(1, D_MODEL), b.reshape(1, D_MODEL)]
    out_shape = jax.ShapeDtypeStruct((t, D_MODEL), F32)
    out_specs = row
    if next_mod is not None:
        nmod, sh, sc = next_mod
        in_specs += [chunk(sh), chunk(sc)]
        args += [nmod, nmod]
        out_shape = (out_shape, jax.ShapeDtypeStruct((t, D_MODEL), next_dtype))
        out_specs = (row, row)
    return pl.pallas_call(
        functools.partial(_combine_kernel, fused=next_mod is not None),
        out_shape=out_shape,
        grid=(t // tm,),
        in_specs=in_specs,
        out_specs=out_specs,
        compiler_params=_cparams("arbitrary"),
        name="moe_combine_norm",
    )(*args)


def _rope_tables(s):
    t = np.arange(s)
    rows = (t // GRID_W).astype(np.float32)
    cols = (t % GRID_W).astype(np.float32)
    half = HEAD_DIM // 2
    inv = np.float32(ROPE_THETA) ** (-np.arange(0, half, 2, dtype=np.float32) / np.float32(half))
    ar = (rows[:, None] * inv[None, :]).astype(np.float32)
    ac = (cols[:, None] * inv[None, :]).astype(np.float32)
    cos = np.concatenate([np.cos(ar), np.cos(ar), np.cos(ac), np.cos(ac)], axis=-1)
    sin_signed = np.concatenate([-np.sin(ar), np.sin(ar), -np.sin(ac), np.sin(ac)], axis=-1)
    return jnp.asarray(cos, F32), jnp.asarray(sin_signed, F32)


def _moe(xf, route, w_gate, w_up, w_down, layer):
    eid = route[:, 0:2].astype(jnp.int32)
    return _moe_experts(_moe_plan(eid, MOE_TM), xf, w_gate, w_up, w_down, layer)


def kernel(x, c, ctx, c_ctx, w_mod, b_mod, w_in, b_in, attn_sink, lru_conv_w, lru_conv_b,
           lru_w_a, lru_b_a, lru_w_x, lru_b_x, lru_lam, sc_conv_w, w_branch, w_out,
           ln1_g, ln1_b, ln2_g, ln2_b, w_grp, b_grp, w_exp, b_exp, w_gate, w_up, w_down):
    assert x.shape == (1, SEQ, D_MODEL) and ctx.shape == (1, CTX_LEN, D_MODEL)
    xs = x[0]
    xc = ctx[0]
    cos, sin_signed = _rope_tables(SEQ)
    c8 = jnp.concatenate([c, c_ctx[None, :], jnp.zeros((SUBLANES - 2, D_MODEL), F32)], axis=0)
    zero_h0 = jnp.zeros((2, LRU_WIDTH), F32)

    mods = [_mod_call(c8, w_mod, b_mod[l], l) for l in range(DEPTH)]
    u = u_ctx = None
    for l in range(DEPTH):
        last = l == DEPTH - 1
        m_lat = mods[l][0:1]
        m_ctx = mods[l][1:2]

        if l == 0:
            u = _lnmod_call(xs, m_lat, 0, 1, BF16)
            u_ctx = _lnmod_call(xc, m_ctx, 0, 1, BF16)
        p = _proj_call(u, w_in, b_in[l], cos, sin_signed, l, rope=True, tm=1024)
        p_ctx = _proj_call(u_ctx, w_in, b_in[l], cos, sin_signed, l, rope=False, tm=CTX_LEN)

        y_att = _attn_call(p, p_ctx, attn_sink[l])
        lru_args = (lru_conv_w[l], lru_conv_b[l], lru_w_a[l], lru_b_a[l], lru_w_x[l], lru_b_x[l], lru_lam[l])
        hcf, hcr = _lru_call(p_ctx, zero_h0, *lru_args, tc=CTX_LEN)
        h0 = jnp.concatenate([hcf[CTX_LEN - 1:CTX_LEN], hcr[0:1]], axis=0)
        hf, hr = _lru_call(p, h0, *lru_args, tc=256)
        y_lru, y_sc = _prep_call(p, hf, hr, sc_conv_w[l], tm=256)
        z = _mergez_call(y_att, y_lru, y_sc, w_branch, p, l, tm=1024)

        router = (w_grp[l], b_grp[l], w_exp[l], b_exp[l])
        if last:
            xs, xf, route = _outnorm_call(z, w_out, l, xs, m_lat, 2, ln1_g[l], ln1_b[l],
                                          next_mod=(m_lat, 3, 4), next_dtype=F32, router=router)
            ypair2 = _moe(xf, route, w_gate, w_up, w_down, l)
            xs = _combine_call(ypair2, route, xs, m_lat, 5, ln2_g[l], ln2_b[l], 0)
        else:
            xs = _outnorm_call(z, w_out, l, xs, m_lat, 2, ln1_g[l], ln1_b[l])
            yc_att = _ctx_attn_call(p_ctx, attn_sink[l])
            yc_lru, yc_sc = _prep_call(p_ctx, hcf, hcr, sc_conv_w[l], tm=CTX_LEN)
            zc = _mergez_call(yc_att, yc_lru, yc_sc, w_branch, p_ctx, l, tm=CTX_LEN)
            xc = _outnorm_call(zc, w_out, l, xc, m_ctx, 2, ln1_g[l], ln1_b[l])
            xf, route = _lnmod_pair_call(xs, m_lat, xc, m_ctx, 3, 4, router)
            ypair2 = _moe(xf, route, w_gate, w_up, w_down, l)
            n_lat = mods[l + 1][0:1]
            n_ctx = mods[l + 1][1:2]
            xs, u = _combine_call(ypair2, route, xs, m_lat, 5, ln2_g[l], ln2_b[l], 0,
                                  next_mod=(n_lat, 0, 1), next_dtype=BF16)
            xc, u_ctx = _combine_call(ypair2, route, xc, m_ctx, 5, ln2_g[l], ln2_b[l], SEQ // 256,
                                      next_mod=(n_ctx, 0, 1), next_dtype=BF16)
    return xs[None]
```

```python
import functools

import jax
import jax.numpy as jnp
import numpy as np
from jax import lax
from jax.experimental import pallas as pl
from jax.experimental.pallas import tpu as pltpu

F32 = jnp.float32
BF16 = jnp.bfloat16

D_MODEL = 2048
SEQ = 8192
CTX_LEN = 256
DEPTH = 2
GRID_W = 64

N_HEADS = 8
N_KV_HEADS = 2
HEAD_DIM = 128
Q_GROUPS = N_HEADS // N_KV_HEADS
WINDOW = 128
BLOCK = 128
ROPE_THETA = 10000.0
Q_SCALE = HEAD_DIM ** -0.5

LRU_WIDTH = 1024
LRU_BLOCKS = 8
LRU_BLOCK_DIM = 128
LRU_C = 8.0
SC_WIDTH = 1024
N_BRANCH = 3
BRANCH_W = 1024

Q_W = N_HEADS * HEAD_DIM
KV_W = N_KV_HEADS * HEAD_DIM
OFF_K = Q_W
OFF_V = OFF_K + KV_W
OFF_LX = OFF_V + KV_W
OFF_LG = OFF_LX + LRU_WIDTH
OFF_SB = OFF_LG + LRU_WIDTH
OFF_SC = OFF_SB + SC_WIDTH
OFF_SX = OFF_SC + SC_WIDTH
OFF_GATE = OFF_SX + SC_WIDTH
PROJ_W = OFF_GATE + N_BRANCH * D_MODEL

N_GROUPS = 4
EXPERTS_PER_GROUP = 8
N_EXPERTS = N_GROUPS * EXPERTS_PER_GROUP
D_EXPERT = 1024

ALPHA = (2 * DEPTH) ** 0.25
LN_EPS = 1e-5

VMEM_LIMIT_BYTES = 52 * 1024 * 1024
BIG_VMEM_LIMIT_BYTES = 58 * 1024 * 1024
LANES = 128
SUBLANES = 8

HALF_W = 512
PROJ_TN = 1280
MOE_TM = 256


def _cparams(*sem):
    return pltpu.CompilerParams(dimension_semantics=sem, vmem_limit_bytes=VMEM_LIMIT_BYTES)


def _layer_norm(x):
    mu = jnp.mean(x, axis=-1, keepdims=True)
    xc = x - mu
    var = jnp.mean(xc * xc, axis=-1, keepdims=True)
    return xc * lax.rsqrt(var + LN_EPS)


def _mod_kernel(c_ref, w_ref, b_ref, o_ref):
    a = jax.nn.silu(c_ref[...]).astype(BF16)
    o_ref[...] = jnp.dot(a, w_ref[...].astype(BF16), preferred_element_type=F32) + b_ref[...]


def _mod_call(c8, w_all, b, layer):
    n = w_all.shape[2]
    tn = 1024
    return pl.pallas_call(
        _mod_kernel,
        out_shape=jax.ShapeDtypeStruct((SUBLANES, n), F32),
        grid=(n // tn,),
        in_specs=[pl.BlockSpec((SUBLANES, D_MODEL), lambda j: (0, 0)),
                  pl.BlockSpec((None, D_MODEL, tn), lambda j: (layer, 0, j)),
                  pl.BlockSpec((1, tn), lambda j: (0, j))],
        out_specs=pl.BlockSpec((SUBLANES, tn), lambda j: (0, j)),
        compiler_params=_cparams("arbitrary"),
        name="mod_vectors",
    )(c8, w_all, b.reshape(1, n))


def _lnmod_kernel(x_ref, sh_ref, sc_ref, o_ref):
    y = _layer_norm(x_ref[...])
    o_ref[...] = (y * (1.0 + sc_ref[...]) + sh_ref[...]).astype(o_ref.dtype)


def _lnmod_call(x, mod, shift_chunk, scale_chunk, out_dtype):
    t = x.shape[0]
    tm = 256
    return pl.pallas_call(
        _lnmod_kernel,
        out_shape=jax.ShapeDtypeStruct((t, D_MODEL), out_dtype),
        grid=(t // tm,),
        in_specs=[pl.BlockSpec((tm, D_MODEL), lambda i: (i, 0)),
                  pl.BlockSpec((1, D_MODEL), lambda i: (0, shift_chunk)),
                  pl.BlockSpec((1, D_MODEL), lambda i: (0, scale_chunk))],
        out_specs=pl.BlockSpec((tm, D_MODEL), lambda i: (i, 0)),
        compiler_params=_cparams("arbitrary"),
        name="ln_modulate",
    )(x, mod, mod)


def _lnmod_pair_kernel(xa_ref, xb_ref, sha_ref, sca_ref, shb_ref, scb_ref, wr_ref, br_ref,
                       o_ref, route_ref, *, na):
    i = pl.program_id(0)

    def one(x_ref, sh_ref, sc_ref):
        y = _layer_norm(x_ref[...]) * (1.0 + sc_ref[...]) + sh_ref[...]
        o_ref[...] = y
        route_ref[...] = _route(y, wr_ref, br_ref)

    @pl.when(i < na)
    def _():
        one(xa_ref, sha_ref, sca_ref)

    @pl.when(i >= na)
    def _():
        one(xb_ref, shb_ref, scb_ref)


def _lnmod_pair_call(xa, mod_a, xb, mod_b, shift_chunk, scale_chunk, router):
    tm = 256
    na = xa.shape[0] // tm
    nb = xb.shape[0] // tm
    rows = (na + nb) * tm
    vec = lambda chunk: pl.BlockSpec((1, D_MODEL), lambda i: (0, chunk))
    r_args, r_specs = _router_operands(*router)
    return pl.pallas_call(
        functools.partial(_lnmod_pair_kernel, na=na),
        out_shape=(jax.ShapeDtypeStruct((rows, D_MODEL), F32), jax.ShapeDtypeStruct((rows, LANES), F32)),
        grid=(na + nb,),
        in_specs=[pl.BlockSpec((tm, D_MODEL), lambda i: (jnp.minimum(i, na - 1), 0)),
                  pl.BlockSpec((tm, D_MODEL), lambda i: (jnp.maximum(i - na, 0), 0)),
                  vec(shift_chunk), vec(scale_chunk), vec(shift_chunk), vec(scale_chunk)] + r_specs,
        out_specs=(pl.BlockSpec((tm, D_MODEL), lambda i: (i, 0)),
                   pl.BlockSpec((tm, LANES), lambda i: (i, 0))),
        compiler_params=_cparams("arbitrary"),
        name="ln_modulate_route",
    )(xa, xb, mod_a, mod_a, mod_b, mod_b, *r_args)


def _rope_head(xh, cos, sin_signed, first_half):
    partner = jnp.where(first_half, pltpu.roll(xh, LANES - 32, 1), pltpu.roll(xh, 32, 1))
    return xh * cos + partner * sin_signed


def _proj_kernel(u_ref, uc_ref, w_ref, b_ref, cos_ref, sin_ref, o_ref, oc_ref, w_scr):
    j = pl.program_id(0)
    i = pl.program_id(1)

    @pl.when(i == 0)
    def _():
        w_scr[...] = w_ref[...].astype(BF16)
        oc_ref[...] = jnp.dot(uc_ref[...], w_scr[...], preferred_element_type=F32) + b_ref[...]

    def project():
        return jnp.dot(u_ref[...], w_scr[...], preferred_element_type=F32) + b_ref[...]

    @pl.when(j != 0)
    def _():
        o_ref[...] = project()

    @pl.when(j == 0)
    def _():
        acc = project()
        cos = cos_ref[...]
        sin_signed = sin_ref[...]
        lane = lax.broadcasted_iota(jnp.int32, cos.shape, 1)
        first_half = (lane & 63) < 32
        for h in range(N_HEADS + N_KV_HEADS):
            sl = slice(h * HEAD_DIM, (h + 1) * HEAD_DIM)
            r = _rope_head(acc[:, sl], cos, sin_signed, first_half)
            if h < N_HEADS:
                r = r * Q_SCALE
            o_ref[:, sl] = r


def _proj_call(u, u_ctx, w_all, b, cos, sin_signed, layer, *, tm):
    t = u.shape[0]
    lc = u_ctx.shape[0]
    n = w_all.shape[2]
    return pl.pallas_call(
        _proj_kernel,
        out_shape=(jax.ShapeDtypeStruct((t, n), F32), jax.ShapeDtypeStruct((lc, n), F32)),
        grid=(n // PROJ_TN, t // tm),
        in_specs=[pl.BlockSpec((tm, D_MODEL), lambda j, i: (i, 0)),
                  pl.BlockSpec((lc, D_MODEL), lambda j, i: (0, 0)),
                  pl.BlockSpec((None, D_MODEL, PROJ_TN), lambda j, i: (layer, 0, j)),
                  pl.BlockSpec((1, PROJ_TN), lambda j, i: (0, j)),
                  pl.BlockSpec((tm, HEAD_DIM), lambda j, i: (i, 0)),
                  pl.BlockSpec((tm, HEAD_DIM), lambda j, i: (i, 0))],
        out_specs=(pl.BlockSpec((tm, PROJ_TN), lambda j, i: (i, j)),
                   pl.BlockSpec((lc, PROJ_TN), lambda j, i: (0, j))),
        scratch_shapes=[pltpu.VMEM((D_MODEL, PROJ_TN), BF16)],
        compiler_params=pltpu.CompilerParams(dimension_semantics=("arbitrary", "arbitrary"),
                                             vmem_limit_bytes=BIG_VMEM_LIMIT_BYTES),
        name="in_proj_rope",
    )(u, u_ctx, w_all, b.reshape(1, n), cos, sin_signed)


def _softmax_pv(q, k_all, v_all, bias, sink_col):
    s = lax.dot_general(q, k_all, (((1,), (1,)), ((), ())), preferred_element_type=F32)
    if bias is not None:
        s = s + bias
    m = jnp.maximum(jnp.max(s, axis=-1, keepdims=True), sink_col)
    e = jnp.exp(s - m)
    den = jnp.sum(e, axis=-1, keepdims=True) + jnp.exp(sink_col - m)
    p = (e / den).astype(BF16)
    return jnp.dot(p, v_all, preferred_element_type=F32)


def _sink_column(sink_ref, kv, rows):
    row = lax.broadcasted_iota(jnp.int32, (Q_GROUPS * rows, 1), 0)
    col = jnp.full((Q_GROUPS * rows, 1), sink_ref[kv * Q_GROUPS + Q_GROUPS - 1], F32)
    for g in range(Q_GROUPS - 2, -1, -1):
        col = jnp.where(row < (g + 1) * rows, sink_ref[kv * Q_GROUPS + g], col)
    return col


def _attn_kernel(sink_ref, q_ref, kp_ref, kc_ref, kn_ref, vp_ref, vc_ref, vn_ref,
                 kx_ref, vx_ref, o_ref, *, seq):
    n = pl.program_id(0)
    n_loc = 3 * BLOCK
    n_keys = n_loc + CTX_LEN
    qi = lax.broadcasted_iota(jnp.int32, (BLOCK, n_keys), 0)
    kj = lax.broadcasted_iota(jnp.int32, (BLOCK, n_keys), 1)
    rel = kj - BLOCK - qi
    kpos = n * BLOCK - BLOCK + kj
    bad = jnp.where(kj < n_loc,
                    jnp.where(jnp.abs(rel) > WINDOW, 1, 0) + jnp.where(kpos < 0, 1, 0)
                    + jnp.where(kpos >= seq, 1, 0), 0)
    bias1 = jnp.where(bad > 0, -jnp.inf, 0.0).astype(F32)
    bias = jnp.concatenate([bias1] * Q_GROUPS, axis=0)
    for kv in range(N_KV_HEADS):
        ks = slice(kv * HEAD_DIM, (kv + 1) * HEAD_DIM)
        k_all = jnp.concatenate([kp_ref[:, ks], kc_ref[:, ks], kn_ref[:, ks], kx_ref[:, ks]],
                                axis=0).astype(BF16)
        v_all = jnp.concatenate([vp_ref[:, ks], vc_ref[:, ks], vn_ref[:, ks], vx_ref[:, ks]],
                                axis=0).astype(BF16)
        q = jnp.concatenate(
            [q_ref[:, (kv * Q_GROUPS + g) * HEAD_DIM:(kv * Q_GROUPS + g + 1) * HEAD_DIM]
             for g in range(Q_GROUPS)], axis=0).astype(BF16)
        o = _softmax_pv(q, k_all, v_all, bias, _sink_column(sink_ref, kv, BLOCK))
        for g in range(Q_GROUPS):
            h = kv * Q_GROUPS + g
            o_ref[:, h * HEAD_DIM:(h + 1) * HEAD_DIM] = o[g * BLOCK:(g + 1) * BLOCK].astype(o_ref.dtype)


def _attn_call(p, p_ctx, sink):
    s = p.shape[0]
    nb = s // BLOCK
    kcol = OFF_K // KV_W
    vcol = OFF_V // KV_W
    kv_spec = lambda f, c: pl.BlockSpec((BLOCK, KV_W), lambda n: (f(n), c))
    prev = lambda n: jnp.maximum(n - 1, 0)
    cur = lambda n: n
    nxt = lambda n: jnp.minimum(n + 1, nb - 1)
    return pl.pallas_call(
        functools.partial(_attn_kernel, seq=s),
        out_shape=jax.ShapeDtypeStruct((s, Q_W), BF16),
        grid=(nb,),
        in_specs=[pl.BlockSpec(memory_space=pltpu.SMEM),
                  pl.BlockSpec((BLOCK, Q_W), lambda n: (n, 0)),
                  kv_spec(prev, kcol), kv_spec(cur, kcol), kv_spec(nxt, kcol),
                  kv_spec(prev, vcol), kv_spec(cur, vcol), kv_spec(nxt, vcol),
                  pl.BlockSpec((CTX_LEN, KV_W), lambda n: (0, kcol)),
                  pl.BlockSpec((CTX_LEN, KV_W), lambda n: (0, vcol))],
        out_specs=pl.BlockSpec((BLOCK, Q_W), lambda n: (n, 0)),
        compiler_params=_cparams("arbitrary"),
        name="window_attention",
    )(sink, p, p, p, p, p, p, p, p_ctx, p_ctx)


def _ctx_attn_kernel(sink_ref, q_ref, k_ref, v_ref, o_ref):
    rows = q_ref.shape[0]
    for kv in range(N_KV_HEADS):
        ks = slice(kv * HEAD_DIM, (kv + 1) * HEAD_DIM)
        q = jnp.concatenate(
            [q_ref[:, (kv * Q_GROUPS + g) * HEAD_DIM:(kv * Q_GROUPS + g + 1) * HEAD_DIM] * Q_SCALE
             for g in range(Q_GROUPS)], axis=0).astype(BF16)
        o = _softmax_pv(q, k_ref[:, ks].astype(BF16), v_ref[:, ks].astype(BF16), None,
                        _sink_column(sink_ref, kv, rows))
        for g in range(Q_GROUPS):
            h = kv * Q_GROUPS + g
            o_ref[:, h * HEAD_DIM:(h + 1) * HEAD_DIM] = o[g * rows:(g + 1) * rows].astype(o_ref.dtype)


def _ctx_attn_call(p_ctx, sink):
    lc = p_ctx.shape[0]
    return pl.pallas_call(
        _ctx_attn_kernel,
        out_shape=jax.ShapeDtypeStruct((lc, Q_W), BF16),
        grid=(1,),
        in_specs=[pl.BlockSpec(memory_space=pltpu.SMEM),
                  pl.BlockSpec((lc, Q_W), lambda n: (0, 0)),
                  pl.BlockSpec((lc, KV_W), lambda n: (0, OFF_K // KV_W)),
                  pl.BlockSpec((lc, KV_W), lambda n: (0, OFF_V // KV_W))],
        out_specs=pl.BlockSpec((lc, Q_W), lambda n: (0, 0)),
        compiler_params=_cparams("arbitrary"),
        name="context_attention",
    )(sink, p_ctx, p_ctx, p_ctx)


def _lru_conv(ext_scr, cur_ref, prev_ref, next_ref, is_first, is_last, cw_ref, cb_ref, tc):
    ext_scr[0:SUBLANES, :] = jnp.where(is_first, 0.0, prev_ref[...])
    ext_scr[SUBLANES:SUBLANES + tc, :] = cur_ref[...]
    ext_scr[SUBLANES + tc:, :] = jnp.where(is_last, 0.0, next_ref[...])
    out = cb_ref[...] + cw_ref[0:1, :] * ext_scr[SUBLANES - 1:SUBLANES - 1 + tc, :]
    for j in range(1, 4):
        out = out + cw_ref[j:j + 1, :] * ext_scr[SUBLANES - 1 + j:SUBLANES - 1 + j + tc, :]
    return out


def _lru_gates(xb, d, wa_ref, ba_ref, wx_ref, bx_ref, lam_ref, a_scr, u_scr):
    xb_bf = xb.astype(BF16)
    for b in range(HALF_W // LRU_BLOCK_DIM):
        sl = slice(b * LRU_BLOCK_DIM, (b + 1) * LRU_BLOCK_DIM)
        xs = xb_bf[:, sl]
        r = jax.nn.sigmoid(jnp.dot(xs, wa_ref[d, b].astype(BF16), preferred_element_type=F32)
                           + ba_ref[d:d + 1, sl])
        ig = jax.nn.sigmoid(jnp.dot(xs, wx_ref[d, b].astype(BF16), preferred_element_type=F32)
                            + bx_ref[d:d + 1, sl])
        log_a = -LRU_C * r * jax.nn.softplus(-lam_ref[d:d + 1, sl])
        a_scr[d, :, sl] = jnp.exp(log_a)
        u_scr[d, :, sl] = jnp.sqrt(1.0 - jnp.exp(2.0 * log_a)) * ig * xb[:, sl]


def _lru_kernel(fc_ref, fp_ref, fn_ref, rc_ref, rp_ref, rn_ref,
                cw_ref, cb_ref, wa_ref, ba_ref, wx_ref, bx_ref, lam_ref, h0_ref,
                hf_ref, hr_ref, ext_scr, a_scr, u_scr, cf_scr, cr_scr, *, tc, nchunks):
    i = pl.program_id(1)
    w = HALF_W

    @pl.when(i == 0)
    def _():
        cf_scr[...] = jnp.broadcast_to(h0_ref[0:1, :], (SUBLANES, w))
        cr_scr[...] = jnp.broadcast_to(h0_ref[1:2, :], (SUBLANES, w))

    xb_f = _lru_conv(ext_scr, fc_ref, fp_ref, fn_ref, i == 0, i == nchunks - 1, cw_ref, cb_ref, tc)
    _lru_gates(xb_f, 0, wa_ref, ba_ref, wx_ref, bx_ref, lam_ref, a_scr, u_scr)
    xb_r = _lru_conv(ext_scr, rc_ref, rp_ref, rn_ref, i == nchunks - 1, i == 0, cw_ref, cb_ref, tc)
    _lru_gates(xb_r, 1, wa_ref, ba_ref, wx_ref, bx_ref, lam_ref, a_scr, u_scr)

    row = lax.broadcasted_iota(jnp.int32, (SUBLANES, w), 0)
    ntile = tc // SUBLANES

    def body(j, carry):
        cf, cr = carry
        off = pl.multiple_of(j * SUBLANES, SUBLANES)
        a = a_scr[0, pl.ds(off, SUBLANES), :]
        u = u_scr[0, pl.ds(off, SUBLANES), :]
        for s in (1, 2, 4):
            m = row >= s
            u = jnp.where(m, u + a * pltpu.roll(u, s, 0), u)
            a = jnp.where(m, a * pltpu.roll(a, s, 0), a)
        h = u + a * cf
        hf_ref[pl.ds(off, SUBLANES), :] = h
        cf = jnp.broadcast_to(h[SUBLANES - 1:SUBLANES, :], (SUBLANES, w))
        offr = pl.multiple_of((ntile - 1 - j) * SUBLANES, SUBLANES)
        a = a_scr[1, pl.ds(offr, SUBLANES), :]
        u = u_scr[1, pl.ds(offr, SUBLANES), :]
        for s in (1, 2, 4):
            m = row < SUBLANES - s
            u = jnp.where(m, u + a * pltpu.roll(u, SUBLANES - s, 0), u)
            a = jnp.where(m, a * pltpu.roll(a, SUBLANES - s, 0), a)
        h = u + a * cr
        hr_ref[pl.ds(offr, SUBLANES), :] = h
        cr = jnp.broadcast_to(h[0:1, :], (SUBLANES, w))
        return cf, cr

    cf, cr = lax.fori_loop(0, ntile, body, (cf_scr[...], cr_scr[...]))
    cf_scr[...] = cf
    cr_scr[...] = cr


def _lru_call(p, h0, cw, cb, wa, ba, wx, bx, lam, *, tc):
    t = p.shape[0]
    nchunks = t // tc
    w = HALF_W
    c0 = OFF_LX // w
    r8 = tc // SUBLANES
    last8 = t // SUBLANES - 1
    fwd = lambda i: i
    rev = lambda i: nchunks - 1 - i
    cur_spec = lambda f: pl.BlockSpec((tc, w), lambda h, i: (f(i), c0 + h))
    prev_spec = lambda f: pl.BlockSpec((SUBLANES, w), lambda h, i: (jnp.maximum(f(i) * r8 - 1, 0), c0 + h))
    next_spec = lambda f: pl.BlockSpec((SUBLANES, w), lambda h, i: (jnp.minimum((f(i) + 1) * r8, last8), c0 + h))
    vec_spec = lambda rows: pl.BlockSpec((rows, w), lambda h, i: (0, h))
    gate_w_spec = pl.BlockSpec((2, w // LRU_BLOCK_DIM, LRU_BLOCK_DIM, LRU_BLOCK_DIM),
                               lambda h, i: (0, h, 0, 0))
    kern = functools.partial(_lru_kernel, tc=tc, nchunks=nchunks)
    return pl.pallas_call(
        kern,
        out_shape=(jax.ShapeDtypeStruct((t, LRU_WIDTH), F32), jax.ShapeDtypeStruct((t, LRU_WIDTH), F32)),
        grid=(LRU_WIDTH // w, nchunks),
        in_specs=[cur_spec(fwd), prev_spec(fwd), next_spec(fwd),
                  cur_spec(rev), prev_spec(rev), next_spec(rev),
                  vec_spec(4), vec_spec(1), gate_w_spec, vec_spec(2), gate_w_spec, vec_spec(2),
                  vec_spec(2), vec_spec(2)],
        out_specs=(pl.BlockSpec((tc, w), lambda h, i: (i, h)),
                   pl.BlockSpec((tc, w), lambda h, i: (nchunks - 1 - i, h))),
        scratch_shapes=[pltpu.VMEM((tc + 2 * SUBLANES, w), F32),
                        pltpu.VMEM((2, tc, w), F32), pltpu.VMEM((2, tc, w), F32),
                        pltpu.VMEM((SUBLANES, w), F32), pltpu.VMEM((SUBLANES, w), F32)],
        compiler_params=_cparams("arbitrary", "arbitrary"),
        name="rg_lru",
    )(p, p, p, p, p, p, cw, cb.reshape(1, LRU_WIDTH), wa, ba.reshape(2, LRU_WIDTH),
      wx, bx.reshape(2, LRU_WIDTH), lam, h0)


def _prep_kernel(lg_ref, hf_ref, hr_ref, sb_ref, scc_ref, scp_ref, scn_ref,
                 sxc_ref, sxp_ref, sxn_ref, w3_ref, ylru_ref, ysc_ref, ext_scr, *, tm, ntiles):
    i = pl.program_id(0)
    ylru_ref[...] = ((hf_ref[...] + hr_ref[...]) * jax.nn.gelu(lg_ref[...])).astype(ylru_ref.dtype)
    ext_scr[0:SUBLANES, :] = jnp.where(i == 0, 0.0, scp_ref[...] * sxp_ref[...])
    ext_scr[SUBLANES:SUBLANES + tm, :] = scc_ref[...] * sxc_ref[...]
    ext_scr[SUBLANES + tm:, :] = jnp.where(i == ntiles - 1, 0.0, scn_ref[...] * sxn_ref[...])
    conv = w3_ref[0:1, :] * ext_scr[SUBLANES - 1:SUBLANES - 1 + tm, :]
    for j in range(1, 3):
        conv = conv + w3_ref[j:j + 1, :] * ext_scr[SUBLANES - 1 + j:SUBLANES - 1 + j + tm, :]
    ysc_ref[...] = (sb_ref[...] * conv).astype(ysc_ref.dtype)


def _prep_call(p, hf, hr, w3, *, tm):
    t = p.shape[0]
    ntiles = t // tm
    w = HALF_W
    r8 = tm // SUBLANES
    last8 = t // SUBLANES - 1
    col = lambda off: pl.BlockSpec((tm, w), lambda i, h: (i, off // w + h))
    colp = lambda off: pl.BlockSpec((SUBLANES, w), lambda i, h: (jnp.maximum(i * r8 - 1, 0), off // w + h))
    coln = lambda off: pl.BlockSpec((SUBLANES, w), lambda i, h: (jnp.minimum((i + 1) * r8, last8), off // w + h))
    own = pl.BlockSpec((tm, w), lambda i, h: (i, h))
    kern = functools.partial(_prep_kernel, tm=tm, ntiles=ntiles)
    return pl.pallas_call(
        kern,
        out_shape=(jax.ShapeDtypeStruct((t, LRU_WIDTH), BF16), jax.ShapeDtypeStruct((t, SC_WIDTH), BF16)),
        grid=(ntiles, LRU_WIDTH // w),
        in_specs=[col(OFF_LG), own, own, col(OFF_SB),
                  col(OFF_SC), colp(OFF_SC), coln(OFF_SC),
                  col(OFF_SX), colp(OFF_SX), coln(OFF_SX),
                  pl.BlockSpec((3, w), lambda i, h: (0, h))],
        out_specs=(own, own),
        scratch_shapes=[pltpu.VMEM((tm + 2 * SUBLANES, w), F32)],
        compiler_params=_cparams("arbitrary", "arbitrary"),
        name="branch_prep",
    )(p, hf, hr, p, p, p, p, p, p, p, w3)


def _mergez_kernel(b0_ref, b1_ref, b2_ref, wb_ref, g0_ref, g1_ref, g2_ref, z_ref):
    acc = None
    for j, (b_ref, g_ref) in enumerate(((b0_ref, g0_ref), (b1_ref, g1_ref), (b2_ref, g2_ref))):
        t = jax.nn.sigmoid(g_ref[...]) * jnp.dot(b_ref[...], wb_ref[j].astype(BF16),
                                                  preferred_element_type=F32)
        acc = t if acc is None else acc + t
    z_ref[...] = acc.astype(z_ref.dtype)


def _mergez_call(y_att, y_lru, y_sc, wb_all, p, layer, *, tm):
    t = p.shape[0]
    tn = 256
    br = pl.BlockSpec((tm, BRANCH_W), lambda i, j: (i, 0))
    gate = lambda k: pl.BlockSpec((tm, tn), lambda i, j: (i, (OFF_GATE + k * D_MODEL) // tn + j))
    return pl.pallas_call(
        _mergez_kernel,
        out_shape=jax.ShapeDtypeStruct((t, D_MODEL), BF16),
        grid=(t // tm, D_MODEL // tn),
        in_specs=[br, br, br,
                  pl.BlockSpec((None, N_BRANCH, BRANCH_W, tn), lambda i, j: (layer, 0, 0, j)),
                  gate(0), gate(1), gate(2)],
        out_specs=pl.BlockSpec((tm, tn), lambda i, j: (i, j)),
        compiler_params=pltpu.CompilerParams(dimension_semantics=("arbitrary", "arbitrary"),
                                             vmem_limit_bytes=BIG_VMEM_LIMIT_BYTES),
        name="branch_merge",
    )(y_att, y_lru, y_sc, wb_all, p, p, p)


def _post_norm_store(r, g_ref, b_ref, nsh_ref, nsc_ref, o_ref, on_ref):
    o = _layer_norm(r) * g_ref[...] + b_ref[...]
    o_ref[...] = o
    if on_ref is None:
        return None
    on = _layer_norm(o) * (1.0 + nsc_ref[...]) + nsh_ref[...]
    on_ref[...] = on.astype(on_ref.dtype)
    return on


def _outnorm_kernel(z_ref, w_ref, x_ref, gate_ref, g_ref, b_ref, *rest, fused, routed):
    nsh_ref = nsc_ref = on_ref = wr_ref = br_ref = route_ref = None
    if routed:
        nsh_ref, nsc_ref, wr_ref, br_ref, o_ref, on_ref, route_ref, w_scr = rest
    elif fused:
        nsh_ref, nsc_ref, o_ref, on_ref, w_scr = rest
    else:
        o_ref, w_scr = rest

    @pl.when(pl.program_id(0) == 0)
    def _():
        w_scr[...] = w_ref[...].astype(BF16)

    y = jnp.dot(z_ref[...], w_scr[...], preferred_element_type=F32)
    r = ALPHA * x_ref[...] + gate_ref[...] * y
    on = _post_norm_store(r, g_ref, b_ref, nsh_ref, nsc_ref, o_ref, on_ref)
    if routed:
        route_ref[...] = _route(on, wr_ref, br_ref)


def _outnorm_call(z, w_all, layer, x, mod, gate_chunk, g, b, next_mod=None, next_dtype=None,
                  router=None):
    t = z.shape[0]
    tm = 256
    row = pl.BlockSpec((tm, D_MODEL), lambda i: (i, 0))
    vec = pl.BlockSpec((1, D_MODEL), lambda i: (0, 0))
    chunk = lambda c: pl.BlockSpec((1, D_MODEL), lambda i: (0, c))
    in_specs = [row,
                pl.BlockSpec((None, D_MODEL, D_MODEL), lambda i: (layer, 0, 0),
                             pipeline_mode=pl.Buffered(1)),
                row, chunk(gate_chunk), vec, vec]
    args = [z, w_all, x, mod, g.reshape(1, D_MODEL), b.reshape(1, D_MODEL)]
    out_shape = jax.ShapeDtypeStruct((t, D_MODEL), F32)
    out_specs = row
    if next_mod is not None:
        nmod, sh, sc = next_mod
        in_specs += [chunk(sh), chunk(sc)]
        args += [nmod, nmod]
        out_shape = (out_shape, jax.ShapeDtypeStruct((t, D_MODEL), next_dtype))
        out_specs = (row, row)
    if router is not None:
        r_args, r_specs = _router_operands(*router)
        in_specs += r_specs
        args += r_args
        out_shape = out_shape + (jax.ShapeDtypeStruct((t, LANES), F32),)
        out_specs = out_specs + (pl.BlockSpec((tm, LANES), lambda i: (i, 0)),)
    return pl.pallas_call(
        functools.partial(_outnorm_kernel, fused=next_mod is not None, routed=router is not None),
        out_shape=out_shape,
        grid=(t // tm,),
        in_specs=in_specs,
        out_specs=out_specs,
        scratch_shapes=[pltpu.VMEM((D_MODEL, D_MODEL), BF16)],
        compiler_params=_cparams("arbitrary"),
        name="out_proj_norm",
    )(*args)


def _route(xf, w_ref, b_ref):
    logits = jnp.dot(xf.astype(BF16), w_ref[...], preferred_element_type=F32) + b_ref[...]
    lane = lax.broadcasted_iota(jnp.int32, logits.shape, 1)
    neg = -jnp.inf
    is_grp = lane < N_GROUPS
    gl = jnp.where(is_grp, logits, neg)
    gmax = jnp.max(gl, axis=-1, keepdims=True)
    g_idx = jnp.min(jnp.where(gl == gmax, lane, LANES), axis=-1, keepdims=True)
    gsum = jnp.sum(jnp.where(is_grp, jnp.exp(gl - gmax), 0.0), axis=-1, keepdims=True)
    g_w = 1.0 / gsum
    lo = N_GROUPS + g_idx * EXPERTS_PER_GROUP
    in_grp = jnp.where(lane >= lo, 1, 0) * jnp.where(lane < lo + EXPERTS_PER_GROUP, 1, 0)
    sel = jnp.where(in_grp > 0, logits, neg)
    v0 = jnp.max(sel, axis=-1, keepdims=True)
    i0 = jnp.min(jnp.where(sel == v0, lane, LANES), axis=-1, keepdims=True)
    sel1 = jnp.where(lane == i0, neg, sel)
    v1 = jnp.max(sel1, axis=-1, keepdims=True)
    i1 = jnp.min(jnp.where(sel1 == v1, lane, LANES), axis=-1, keepdims=True)
    e1 = jnp.exp(v1 - v0)
    den = 1.0 + e1
    w0 = (1.0 / den) * g_w
    w1 = (e1 / den) * g_w
    return jnp.where(lane == 0, (i0 - N_GROUPS).astype(F32),
                     jnp.where(lane == 1, (i1 - N_GROUPS).astype(F32),
                               jnp.where(lane == 2, w0, jnp.where(lane == 3, w1, 0.0))))


def _router_operands(w_grp, b_grp, w_exp, b_exp):
    pad = LANES - N_GROUPS - N_EXPERTS
    wr = jnp.concatenate([w_grp, w_exp, jnp.zeros((D_MODEL, pad), F32)], axis=1).astype(BF16)
    br = jnp.concatenate([b_grp, b_exp, jnp.zeros((pad,), F32)]).reshape(1, LANES)
    specs = [pl.BlockSpec((D_MODEL, LANES), lambda i: (0, 0)), pl.BlockSpec((1, LANES), lambda i: (0, 0))]
    return [wr, br], specs


MOE_UNROLL = 8


def _tile_rows(q0_ref, cnt_ref, order_ref, tile, fn, static_rows):
    base = q0_ref[tile]
    cnt = cnt_ref[tile]
    if static_rows:
        for r in range(MOE_TM):
            fn(r, order_ref[base + r], r < cnt)
        return

    def body(blk, c):
        for k in range(MOE_UNROLL):
            r = blk * MOE_UNROLL + k
            fn(r, order_ref[base + r], r < cnt)
        return c
    lax.fori_loop(0, MOE_TM // MOE_UNROLL, body, 0)


def _is_new_expert(te_ref, i):
    return jnp.logical_or(i == 0, te_ref[i] != te_ref[jnp.maximum(i - 1, 0)])


def _moe_kernel(te_ref, nx_ref, q0_ref, cnt_ref, nt_ref, order_ref, tok_ref,
                x_hbm, wg_hbm, wu_hbm, wd_hbm, y_hbm,
                xbuf, ybuf, wg_stage, wu_stage, wd_stage, wg_scr, wu_scr, wd_scr,
                gsem, ssem, wsem, *, layer, ntok):
    i = pl.program_id(0)
    nt = nt_ref[0]

    def weight_copies(e):
        return (pltpu.make_async_copy(wg_hbm.at[layer, e], wg_stage, wsem.at[0]),
                pltpu.make_async_copy(wu_hbm.at[layer, e], wu_stage, wsem.at[1]),
                pltpu.make_async_copy(wd_hbm.at[layer, e], wd_stage, wsem.at[2]))

    def start_gather(tile, s, static_rows):
        def issue(r, tok, valid):
            pltpu.make_async_copy(x_hbm.at[pl.ds(tok, 1), :], xbuf.at[s, pl.ds(r, 1), :],
                                  gsem.at[s]).start()
        _tile_rows(q0_ref, cnt_ref, tok_ref, tile, issue, static_rows)

    def wait_gather(s):
        pltpu.make_async_copy(x_hbm.at[pl.ds(0, MOE_TM), :], xbuf.at[s], gsem.at[s]).wait()

    def wait_scatter(s):
        pltpu.make_async_copy(ybuf.at[s], y_hbm.at[pl.ds(0, MOE_TM), pl.ds(0, D_MODEL)], ssem.at[s]).wait()

    def start_scatter(tile, s):
        def issue(r, pair, valid):
            dst = jnp.where(valid, pair, 2 * ntok + s * MOE_TM + r)
            tok = lax.shift_right_logical(dst, 1)
            col = pl.multiple_of((dst & 1) * D_MODEL, D_MODEL)
            pltpu.make_async_copy(ybuf.at[s, pl.ds(r, 1), :],
                                  y_hbm.at[pl.ds(tok, 1), pl.ds(col, D_MODEL)], ssem.at[s]).start()
        _tile_rows(q0_ref, cnt_ref, order_ref, tile, issue, True)

    @pl.when(i == 0)
    def _():
        start_gather(0, 0, False)
        for copy in weight_copies(te_ref[0]):
            copy.start()
        ybuf[0] = jnp.zeros(ybuf.shape[1:], ybuf.dtype)
        for half in range(2):
            spare = pltpu.make_async_copy(
                ybuf.at[0], y_hbm.at[pl.ds(ntok, MOE_TM), pl.ds(half * D_MODEL, D_MODEL)], ssem.at[0])
            spare.start()
            spare.wait()

    @pl.when(_is_new_expert(te_ref, i))
    def _():
        for copy in weight_copies(te_ref[i]):
            copy.wait()
        wg_scr[...] = wg_stage[...].astype(BF16)
        wu_scr[...] = wu_stage[...].astype(BF16)
        wd_scr[...] = wd_stage[...].astype(BF16)

        @pl.when(nx_ref[i] >= 0)
        def _():
            for copy in weight_copies(nx_ref[i]):
                copy.start()

    for s in range(2):
        @pl.when(jnp.logical_and(i < nt, i % 2 == s))
        def _(s=s):
            wait_gather(s)

            @pl.when(i >= 2)
            def _():
                wait_scatter(s)

            start_gather(jnp.minimum(i + 1, nt - 1), 1 - s, True)
            x = xbuf[s].astype(BF16)
            hg = jnp.dot(x, wg_scr[...], preferred_element_type=F32)
            hu = jnp.dot(x, wu_scr[...], preferred_element_type=F32)
            h = (jax.nn.silu(hg) * hu).astype(BF16)
            ybuf[s] = jnp.dot(h, wd_scr[...], preferred_element_type=F32)
            start_scatter(i, s)

            @pl.when(i == nt - 1)
            def _():
                wait_gather(1 - s)

                @pl.when(i >= 1)
                def _():
                    wait_scatter(1 - s)
                wait_scatter(s)


def _moe_experts(plan, xf, w_gate, w_up, w_down, layer):
    te, nxt, q0, cnt, nt, order, tok = plan
    n = xf.shape[0]
    ntiles = te.shape[0]
    hbm = pl.BlockSpec(memory_space=pl.ANY)
    return pl.pallas_call(
        functools.partial(_moe_kernel, layer=layer, ntok=n),
        out_shape=jax.ShapeDtypeStruct((n + MOE_TM, 2 * D_MODEL), F32),
        grid_spec=pltpu.PrefetchScalarGridSpec(
            num_scalar_prefetch=7,
            grid=(ntiles,),
            in_specs=[hbm, hbm, hbm, hbm],
            out_specs=hbm,
            scratch_shapes=[pltpu.VMEM((2, MOE_TM, D_MODEL), F32),
                            pltpu.VMEM((2, MOE_TM, D_MODEL), F32),
                            pltpu.VMEM((D_MODEL, D_EXPERT), F32),
                            pltpu.VMEM((D_MODEL, D_EXPERT), F32),
                            pltpu.VMEM((D_EXPERT, D_MODEL), F32),
                            pltpu.VMEM((D_MODEL, D_EXPERT), BF16),
                            pltpu.VMEM((D_MODEL, D_EXPERT), BF16),
                            pltpu.VMEM((D_EXPERT, D_MODEL), BF16),
                            pltpu.SemaphoreType.DMA((2,)),
                            pltpu.SemaphoreType.DMA((2,)),
                            pltpu.SemaphoreType.DMA((3,))]),
        compiler_params=pltpu.CompilerParams(dimension_semantics=("arbitrary",),
                                             vmem_limit_bytes=BIG_VMEM_LIMIT_BYTES),
        name="moe_experts",
    )(te, nxt, q0, cnt, nt, order, tok, xf, w_gate, w_up, w_down)


def _lookup(table, idx):
    onehot = idx[:, None] == jnp.arange(table.shape[0], dtype=jnp.int32)[None, :]
    return jnp.sum(jnp.where(onehot, table[None, :], 0), axis=1)


def _moe_plan(eid, tm):
    npair = eid.size
    ntiles = npair // tm + N_EXPERTS
    flat = eid.reshape(-1)
    order = jnp.argsort(flat, stable=True).astype(jnp.int32)
    experts = jnp.arange(N_EXPERTS, dtype=jnp.int32)
    counts = jnp.sum((flat[:, None] == experts[None, :]).astype(jnp.int32), axis=0)
    padded = ((counts + tm - 1) // tm) * tm
    pend = jnp.cumsum(padded)
    pstart = pend - padded
    start = jnp.cumsum(counts) - counts
    nt = pend[-1] // tm
    tiles = jnp.arange(ntiles, dtype=jnp.int32)
    tile_start = tiles * tm
    te = jnp.minimum(jnp.sum((tile_start[:, None] >= pend[None, :]).astype(jnp.int32), axis=1), N_EXPERTS - 1)
    in_use = tiles < nt
    off = tile_start - _lookup(pstart, te)
    q0 = jnp.where(in_use, _lookup(start, te) + off, 0)
    cnt = jnp.where(in_use, jnp.clip(_lookup(counts, te) - off, 0, tm), 0)
    te_last = jnp.sum(jnp.where(tiles == nt - 1, te, 0))
    te = jnp.where(in_use, te, te_last)
    later = (experts[None, :] > experts[:, None]) & (counts[None, :] > 0)
    next_expert = jnp.min(jnp.where(later, experts[None, :], N_EXPERTS), axis=1)
    next_expert = jnp.where(next_expert < N_EXPERTS, next_expert, -1)
    nxt = _lookup(next_expert, te)
    i32 = lambda a: a.astype(jnp.int32)
    order = jnp.concatenate([order, jnp.zeros((tm,), jnp.int32)])
    return i32(te), i32(nxt), i32(q0), i32(cnt), i32(nt).reshape(1), order, order // 2


def _combine_kernel(yp_ref, r_ref, x_ref, gate_ref, g_ref, b_ref, *rest, fused):
    if fused:
        nsh_ref, nsc_ref, o_ref, on_ref = rest
    else:
        (o_ref,), nsh_ref, nsc_ref, on_ref = rest, None, None, None
    w0 = r_ref[:, 2:3]
    w1 = r_ref[:, 3:4]
    m = w0 * yp_ref[:, 0:D_MODEL] + w1 * yp_ref[:, D_MODEL:2 * D_MODEL]
    r = ALPHA * x_ref[...] + gate_ref[...] * m
    _post_norm_store(r, g_ref, b_ref, nsh_ref, nsc_ref, o_ref, on_ref)


def _combine_call(ypair2, route, x, mod, gate_chunk, g, b, row_block0, next_mod=None, next_dtype=None):
    t = x.shape[0]
    tm = 256
    row = pl.BlockSpec((tm, D_MODEL), lambda i: (i, 0))
    vec = pl.BlockSpec((1, D_MODEL), lambda i: (0, 0))
    chunk = lambda c: pl.BlockSpec((1, D_MODEL), lambda i: (0, c))
    in_specs = [pl.BlockSpec((tm, 2 * D_MODEL), lambda i: (i + row_block0, 0)),
                pl.BlockSpec((tm, LANES), lambda i: (i + row_block0, 0)),
                row, chunk(gate_chunk), vec, vec]
    args = [ypair2, route, x, mod, g.reshape(1, D_MODEL), b.reshape(1, D_MODEL)]
    out_shape = jax.ShapeDtypeStruct((t, D_MODEL), F32)
    out_specs = row
    if next_mod is not None:
        nmod, sh, sc = next_mod
        in_specs += [chunk(sh), chunk(sc)]
        args += [nmod, nmod]
        out_shape = (out_shape, jax.ShapeDtypeStruct((t, D_MODEL), next_dtype))
        out_specs = (row, row)
    return pl.pallas_call(
        functools.partial(_combine_kernel, fused=next_mod is not None),
        out_shape=out_shape,
        grid=(t // tm,),
        in_specs=in_specs,
        out_specs=out_specs,
        compiler_params=_cparams("arbitrary"),
        name="moe_combine_norm",
    )(*args)


def _rope_tables(s):
    t = np.arange(s)
    rows = (t // GRID_W).astype(np.float32)
    cols = (t % GRID_W).astype(np.float32)
    half = HEAD_DIM // 2
    inv = np.float32(ROPE_THETA) ** (-np.arange(0, half, 2, dtype=np.float32) / np.float32(half))
    ar = (rows[:, None] * inv[None, :]).astype(np.float32)
    ac = (cols[:, None] * inv[None, :]).astype(np.float32)
    cos = np.concatenate([np.cos(ar), np.cos(ar), np.cos(ac), np.cos(ac)], axis=-1)
    sin_signed = np.concatenate([-np.sin(ar), np.sin(ar), -np.sin(ac), np.sin(ac)], axis=-1)
    return jnp.asarray(cos, F32), jnp.asarray(sin_signed, F32)


def _moe(xf, route, w_gate, w_up, w_down, layer):
    eid = route[:, 0:2].astype(jnp.int32)
    return _moe_experts(_moe_plan(eid, MOE_TM), xf, w_gate, w_up, w_down, layer)


def kernel(x, c, ctx, c_ctx, w_mod, b_mod, w_in, b_in, attn_sink, lru_conv_w, lru_conv_b,
           lru_w_a, lru_b_a, lru_w_x, lru_b_x, lru_lam, sc_conv_w, w_branch, w_out,
           ln1_g, ln1_b, ln2_g, ln2_b, w_grp, b_grp, w_exp, b_exp, w_gate, w_up, w_down):
    assert x.shape == (1, SEQ, D_MODEL) and ctx.shape == (1, CTX_LEN, D_MODEL)
    xs = x[0]
    xc = ctx[0]
    cos, sin_signed = _rope_tables(SEQ)
    c8 = jnp.concatenate([c, c_ctx[None, :], jnp.zeros((SUBLANES - 2, D_MODEL), F32)], axis=0)
    zero_h0 = jnp.zeros((2, LRU_WIDTH), F32)

    mods = [_mod_call(c8, w_mod, b_mod[l], l) for l in range(DEPTH)]
    u = u_ctx = None
    for l in range(DEPTH):
        last = l == DEPTH - 1
        m_lat = mods[l][0:1]
        m_ctx = mods[l][1:2]

        if l == 0:
            u = _lnmod_call(xs, m_lat, 0, 1, BF16)
            u_ctx = _lnmod_call(xc, m_ctx, 0, 1, BF16)
        p, p_ctx = _proj_call(u, u_ctx, w_in, b_in[l], cos, sin_signed, l, tm=1024)

        y_att = _attn_call(p, p_ctx, attn_sink[l])
        lru_args = (lru_conv_w[l], lru_conv_b[l], lru_w_a[l], lru_b_a[l], lru_w_x[l], lru_b_x[l], lru_lam[l])
        hcf, hcr = _lru_call(p_ctx, zero_h0, *lru_args, tc=CTX_LEN)
        h0 = jnp.concatenate([hcf[CTX_LEN - 1:CTX_LEN], hcr[0:1]], axis=0)
        hf, hr = _lru_call(p, h0, *lru_args, tc=256)
        y_lru, y_sc = _prep_call(p, hf, hr, sc_conv_w[l], tm=256)
        z = _mergez_call(y_att, y_lru, y_sc, w_branch, p, l, tm=2048)

        router = (w_grp[l], b_grp[l], w_exp[l], b_exp[l])
        if last:
            xs, xf, route = _outnorm_call(z, w_out, l, xs, m_lat, 2, ln1_g[l], ln1_b[l],
                                          next_mod=(m_lat, 3, 4), next_dtype=F32, router=router)
            ypair2 = _moe(xf, route, w_gate, w_up, w_down, l)
            xs = _combine_call(ypair2, route, xs, m_lat, 5, ln2_g[l], ln2_b[l], 0)
        else:
            xs = _outnorm_call(z, w_out, l, xs, m_lat, 2, ln1_g[l], ln1_b[l])
            yc_att = _ctx_attn_call(p_ctx, attn_sink[l])
            yc_lru, yc_sc = _prep_call(p_ctx, hcf, hcr, sc_conv_w[l], tm=CTX_LEN)
            zc = _mergez_call(yc_att, yc_lru, yc_sc, w_branch, p_ctx, l, tm=CTX_LEN)
            xc = _outnorm_call(zc, w_out, l, xc, m_ctx, 2, ln1_g[l], ln1_b[l])
            xf, route = _lnmod_pair_call(xs, m_lat, xc, m_ctx, 3, 4, router)
            ypair2 = _moe(xf, route, w_gate, w_up, w_down, l)
            n_lat = mods[l + 1][0:1]
            n_ctx = mods[l + 1][1:2]
            xs, u = _combine_call(ypair2, route, xs, m_lat, 5, ln2_g[l], ln2_b[l], 0,
                                  next_mod=(n_lat, 0, 1), next_dtype=BF16)
            xc, u_ctx = _combine_call(ypair2, route, xc, m_ctx, 5, ln2_g[l], ln2_b[l], SEQ // 256,
                                      next_mod=(n_ctx, 0, 1), next_dtype=BF16)
    return xs[None]
```

```python
import functools

import jax
import jax.numpy as jnp
import numpy as np
from jax import lax
from jax.experimental import pallas as pl
from jax.experimental.pallas import tpu as pltpu

F32 = jnp.float32
BF16 = jnp.bfloat16

D_MODEL = 2048
SEQ = 8192
CTX_LEN = 256
DEPTH = 2
GRID_W = 64

N_HEADS = 8
N_KV_HEADS = 2
HEAD_DIM = 128
Q_GROUPS = N_HEADS // N_KV_HEADS
WINDOW = 128
BLOCK = 128
ROPE_THETA = 10000.0
Q_SCALE = HEAD_DIM ** -0.5

LRU_WIDTH = 1024
LRU_BLOCKS = 8
LRU_BLOCK_DIM = 128
LRU_C = 8.0
SC_WIDTH = 1024
N_BRANCH = 3
BRANCH_W = 1024

Q_W = N_HEADS * HEAD_DIM
KV_W = N_KV_HEADS * HEAD_DIM
OFF_K = Q_W
OFF_V = OFF_K + KV_W
OFF_LX = OFF_V + KV_W
OFF_LG = OFF_LX + LRU_WIDTH
OFF_SB = OFF_LG + LRU_WIDTH
OFF_SC = OFF_SB + SC_WIDTH
OFF_SX = OFF_SC + SC_WIDTH
OFF_GATE = OFF_SX + SC_WIDTH
PROJ_W = OFF_GATE + N_BRANCH * D_MODEL

N_GROUPS = 4
EXPERTS_PER_GROUP = 8
N_EXPERTS = N_GROUPS * EXPERTS_PER_GROUP
D_EXPERT = 1024

ALPHA = (2 * DEPTH) ** 0.25
LN_EPS = 1e-5

VMEM_LIMIT_BYTES = 52 * 1024 * 1024
BIG_VMEM_LIMIT_BYTES = 58 * 1024 * 1024
LANES = 128
SUBLANES = 8

HALF_W = 512
PROJ_TN = 1280
MOE_TM = 256


def _cparams(*sem):
    return pltpu.CompilerParams(dimension_semantics=sem, vmem_limit_bytes=VMEM_LIMIT_BYTES)


def _layer_norm(x):
    mu = jnp.mean(x, axis=-1, keepdims=True)
    xc = x - mu
    var = jnp.mean(xc * xc, axis=-1, keepdims=True)
    return xc * lax.rsqrt(var + LN_EPS)


def _mod_kernel(c_ref, w_ref, b_ref, o_ref):
    a = jax.nn.silu(c_ref[...]).astype(BF16)
    o_ref[...] = jnp.dot(a, w_ref[...].astype(BF16), preferred_element_type=F32) + b_ref[...]


def _mod_call(c8, w_all, b, layer):
    n = w_all.shape[2]
    tn = 1024
    return pl.pallas_call(
        _mod_kernel,
        out_shape=jax.ShapeDtypeStruct((SUBLANES, n), F32),
        grid=(n // tn,),
        in_specs=[pl.BlockSpec((SUBLANES, D_MODEL), lambda j: (0, 0)),
                  pl.BlockSpec((None, D_MODEL, tn), lambda j: (layer, 0, j)),
                  pl.BlockSpec((1, tn), lambda j: (0, j))],
        out_specs=pl.BlockSpec((SUBLANES, tn), lambda j: (0, j)),
        compiler_params=_cparams("arbitrary"),
        name="mod_vectors",
    )(c8, w_all, b.reshape(1, n))


def _lnmod_kernel(x_ref, sh_ref, sc_ref, o_ref):
    y = _layer_norm(x_ref[...])
    o_ref[...] = (y * (1.0 + sc_ref[...]) + sh_ref[...]).astype(o_ref.dtype)


def _lnmod_call(x, mod, shift_chunk, scale_chunk, out_dtype):
    t = x.shape[0]
    tm = 256
    return pl.pallas_call(
        _lnmod_kernel,
        out_shape=jax.ShapeDtypeStruct((t, D_MODEL), out_dtype),
        grid=(t // tm,),
        in_specs=[pl.BlockSpec((tm, D_MODEL), lambda i: (i, 0)),
                  pl.BlockSpec((1, D_MODEL), lambda i: (0, shift_chunk)),
                  pl.BlockSpec((1, D_MODEL), lambda i: (0, scale_chunk))],
        out_specs=pl.BlockSpec((tm, D_MODEL), lambda i: (i, 0)),
        compiler_params=_cparams("arbitrary"),
        name="ln_modulate",
    )(x, mod, mod)


def _lnmod_pair_kernel(xa_ref, xb_ref, sha_ref, sca_ref, shb_ref, scb_ref, wr_ref, br_ref,
                       o_ref, route_ref, *, na):
    i = pl.program_id(0)

    def one(x_ref, sh_ref, sc_ref):
        y = _layer_norm(x_ref[...]) * (1.0 + sc_ref[...]) + sh_ref[...]
        o_ref[...] = y
        route_ref[...] = _route(y, wr_ref, br_ref)

    @pl.when(i < na)
    def _():
        one(xa_ref, sha_ref, sca_ref)

    @pl.when(i >= na)
    def _():
        one(xb_ref, shb_ref, scb_ref)


def _lnmod_pair_call(xa, mod_a, xb, mod_b, shift_chunk, scale_chunk, router):
    tm = 256
    na = xa.shape[0] // tm
    nb = xb.shape[0] // tm
    rows = (na + nb) * tm
    vec = lambda chunk: pl.BlockSpec((1, D_MODEL), lambda i: (0, chunk))
    r_args, r_specs = _router_operands(*router)
    return pl.pallas_call(
        functools.partial(_lnmod_pair_kernel, na=na),
        out_shape=(jax.ShapeDtypeStruct((rows, D_MODEL), F32), jax.ShapeDtypeStruct((rows, LANES), F32)),
        grid=(na + nb,),
        in_specs=[pl.BlockSpec((tm, D_MODEL), lambda i: (jnp.minimum(i, na - 1), 0)),
                  pl.BlockSpec((tm, D_MODEL), lambda i: (jnp.maximum(i - na, 0), 0)),
                  vec(shift_chunk), vec(scale_chunk), vec(shift_chunk), vec(scale_chunk)] + r_specs,
        out_specs=(pl.BlockSpec((tm, D_MODEL), lambda i: (i, 0)),
                   pl.BlockSpec((tm, LANES), lambda i: (i, 0))),
        compiler_params=_cparams("arbitrary"),
        name="ln_modulate_route",
    )(xa, xb, mod_a, mod_a, mod_b, mod_b, *r_args)


def _rope_head(xh, cos, sin_signed, first_half):
    partner = jnp.where(first_half, pltpu.roll(xh, LANES - 32, 1), pltpu.roll(xh, 32, 1))
    return xh * cos + partner * sin_signed


def _proj_kernel(u_ref, uc_ref, w_ref, b_ref, cos_ref, sin_ref, o_ref, oc_ref, w_scr):
    j = pl.program_id(0)
    i = pl.program_id(1)

    @pl.when(i == 0)
    def _():
        w_scr[...] = w_ref[...].astype(BF16)
        oc_ref[...] = jnp.dot(uc_ref[...], w_scr[...], preferred_element_type=F32) + b_ref[...]

    def project():
        return jnp.dot(u_ref[...], w_scr[...], preferred_element_type=F32) + b_ref[...]

    @pl.when(j != 0)
    def _():
        o_ref[...] = project()

    @pl.when(j == 0)
    def _():
        acc = project()
        cos = cos_ref[...]
        sin_signed = sin_ref[...]
        lane = lax.broadcasted_iota(jnp.int32, cos.shape, 1)
        first_half = (lane & 63) < 32
        for h in range(N_HEADS + N_KV_HEADS):
            sl = slice(h * HEAD_DIM, (h + 1) * HEAD_DIM)
            r = _rope_head(acc[:, sl], cos, sin_signed, first_half)
            if h < N_HEADS:
                r = r * Q_SCALE
            o_ref[:, sl] = r


def _proj_call(u, u_ctx, w_all, b, cos, sin_signed, layer, *, tm):
    t = u.shape[0]
    lc = u_ctx.shape[0]
    n = w_all.shape[2]
    return pl.pallas_call(
        _proj_kernel,
        out_shape=(jax.ShapeDtypeStruct((t, n), F32), jax.ShapeDtypeStruct((lc, n), F32)),
        grid=(n // PROJ_TN, t // tm),
        in_specs=[pl.BlockSpec((tm, D_MODEL), lambda j, i: (i, 0)),
                  pl.BlockSpec((lc, D_MODEL), lambda j, i: (0, 0)),
                  pl.BlockSpec((None, D_MODEL, PROJ_TN), lambda j, i: (layer, 0, j)),
                  pl.BlockSpec((1, PROJ_TN), lambda j, i: (0, j)),
                  pl.BlockSpec((tm, HEAD_DIM), lambda j, i: (i, 0)),
                  pl.BlockSpec((tm, HEAD_DIM), lambda j, i: (i, 0))],
        out_specs=(pl.BlockSpec((tm, PROJ_TN), lambda j, i: (i, j)),
                   pl.BlockSpec((lc, PROJ_TN), lambda j, i: (0, j))),
        scratch_shapes=[pltpu.VMEM((D_MODEL, PROJ_TN), BF16)],
        compiler_params=pltpu.CompilerParams(dimension_semantics=("arbitrary", "arbitrary"),
                                             vmem_limit_bytes=BIG_VMEM_LIMIT_BYTES),
        name="in_proj_rope",
    )(u, u_ctx, w_all, b.reshape(1, n), cos, sin_signed)


def _softmax_pv(q, k_all, v_all, bias, sink_col):
    s = lax.dot_general(q, k_all, (((1,), (1,)), ((), ())), preferred_element_type=F32)
    if bias is not None:
        s = s + bias
    m = jnp.maximum(jnp.max(s, axis=-1, keepdims=True), sink_col)
    e = jnp.exp(s - m)
    den = jnp.sum(e, axis=-1, keepdims=True) + jnp.exp(sink_col - m)
    p = (e / den).astype(BF16)
    return jnp.dot(p, v_all, preferred_element_type=F32)


def _sink_column(sink_ref, kv, rows):
    row = lax.broadcasted_iota(jnp.int32, (Q_GROUPS * rows, 1), 0)
    col = jnp.full((Q_GROUPS * rows, 1), sink_ref[kv * Q_GROUPS + Q_GROUPS - 1], F32)
    for g in range(Q_GROUPS - 2, -1, -1):
        col = jnp.where(row < (g + 1) * rows, sink_ref[kv * Q_GROUPS + g], col)
    return col


ATTN_QB = 2


def _attn_kernel(sink_ref, q_ref, kp_ref, kc_ref, kn_ref, vp_ref, vc_ref, vn_ref,
                 kx_ref, vx_ref, o_ref, *, seq):
    n = pl.program_id(0)
    n_loc = 3 * BLOCK
    n_keys = n_loc + CTX_LEN
    qi = lax.broadcasted_iota(jnp.int32, (BLOCK, n_keys), 0)
    kj = lax.broadcasted_iota(jnp.int32, (BLOCK, n_keys), 1)
    rel = kj - BLOCK - qi
    sinks = [_sink_column(sink_ref, kv, BLOCK) for kv in range(N_KV_HEADS)]
    for sub in range(ATTN_QB):
        blk = n * ATTN_QB + sub
        kpos = blk * BLOCK - BLOCK + kj
        bad = jnp.where(kj < n_loc,
                        jnp.where(jnp.abs(rel) > WINDOW, 1, 0) + jnp.where(kpos < 0, 1, 0)
                        + jnp.where(kpos >= seq, 1, 0), 0)
        bias1 = jnp.where(bad > 0, -jnp.inf, 0.0).astype(F32)
        bias = jnp.concatenate([bias1] * Q_GROUPS, axis=0)
        rows = slice(sub * BLOCK, (sub + 1) * BLOCK)

        def band(prev_ref, cur_ref, next_ref, ctx_ref, ks):
            own = [cur_ref[j * BLOCK:(j + 1) * BLOCK, ks] for j in range(ATTN_QB)]
            blocks = [prev_ref[:, ks]] + own + [next_ref[:, ks]]
            return jnp.concatenate(blocks[sub:sub + 3] + [ctx_ref[:, ks]], axis=0).astype(BF16)

        for kv in range(N_KV_HEADS):
            ks = slice(kv * HEAD_DIM, (kv + 1) * HEAD_DIM)
            k_all = band(kp_ref, kc_ref, kn_ref, kx_ref, ks)
            v_all = band(vp_ref, vc_ref, vn_ref, vx_ref, ks)
            q = jnp.concatenate(
                [q_ref[rows, (kv * Q_GROUPS + g) * HEAD_DIM:(kv * Q_GROUPS + g + 1) * HEAD_DIM]
                 for g in range(Q_GROUPS)], axis=0).astype(BF16)
            o = _softmax_pv(q, k_all, v_all, bias, sinks[kv])
            for g in range(Q_GROUPS):
                h = kv * Q_GROUPS + g
                o_ref[rows, h * HEAD_DIM:(h + 1) * HEAD_DIM] = (
                    o[g * BLOCK:(g + 1) * BLOCK].astype(o_ref.dtype))


def _attn_call(p, p_ctx, sink):
    s = p.shape[0]
    nb = s // BLOCK
    step_rows = ATTN_QB * BLOCK
    kcol = OFF_K // KV_W
    vcol = OFF_V // KV_W
    prev = lambda c: pl.BlockSpec((BLOCK, KV_W), lambda n: (jnp.maximum(n * ATTN_QB - 1, 0), c))
    nxt = lambda c: pl.BlockSpec((BLOCK, KV_W), lambda n: (jnp.minimum((n + 1) * ATTN_QB, nb - 1), c))
    cur = lambda c: pl.BlockSpec((step_rows, KV_W), lambda n: (n, c))
    return pl.pallas_call(
        functools.partial(_attn_kernel, seq=s),
        out_shape=jax.ShapeDtypeStruct((s, Q_W), BF16),
        grid=(s // step_rows,),
        in_specs=[pl.BlockSpec(memory_space=pltpu.SMEM),
                  pl.BlockSpec((step_rows, Q_W), lambda n: (n, 0)),
                  prev(kcol), cur(kcol), nxt(kcol),
                  prev(vcol), cur(vcol), nxt(vcol),
                  pl.BlockSpec((CTX_LEN, KV_W), lambda n: (0, kcol)),
                  pl.BlockSpec((CTX_LEN, KV_W), lambda n: (0, vcol))],
        out_specs=pl.BlockSpec((step_rows, Q_W), lambda n: (n, 0)),
        compiler_params=_cparams("arbitrary"),
        name="window_attention",
    )(sink, p, p, p, p, p, p, p, p_ctx, p_ctx)


def _ctx_attn_kernel(sink_ref, q_ref, k_ref, v_ref, o_ref):
    rows = q_ref.shape[0]
    for kv in range(N_KV_HEADS):
        ks = slice(kv * HEAD_DIM, (kv + 1) * HEAD_DIM)
        q = jnp.concatenate(
            [q_ref[:, (kv * Q_GROUPS + g) * HEAD_DIM:(kv * Q_GROUPS + g + 1) * HEAD_DIM] * Q_SCALE
             for g in range(Q_GROUPS)], axis=0).astype(BF16)
        o = _softmax_pv(q, k_ref[:, ks].astype(BF16), v_ref[:, ks].astype(BF16), None,
                        _sink_column(sink_ref, kv, rows))
        for g in range(Q_GROUPS):
            h = kv * Q_GROUPS + g
            o_ref[:, h * HEAD_DIM:(h + 1) * HEAD_DIM] = o[g * rows:(g + 1) * rows].astype(o_ref.dtype)


def _ctx_attn_call(p_ctx, sink):
    lc = p_ctx.shape[0]
    return pl.pallas_call(
        _ctx_attn_kernel,
        out_shape=jax.ShapeDtypeStruct((lc, Q_W), BF16),
        grid=(1,),
        in_specs=[pl.BlockSpec(memory_space=pltpu.SMEM),
                  pl.BlockSpec((lc, Q_W), lambda n: (0, 0)),
                  pl.BlockSpec((lc, KV_W), lambda n: (0, OFF_K // KV_W)),
                  pl.BlockSpec((lc, KV_W), lambda n: (0, OFF_V // KV_W))],
        out_specs=pl.BlockSpec((lc, Q_W), lambda n: (0, 0)),
        compiler_params=_cparams("arbitrary"),
        name="context_attention",
    )(sink, p_ctx, p_ctx, p_ctx)


def _lru_conv(ext_scr, cur_ref, prev_ref, next_ref, is_first, is_last, cw_ref, cb_ref, tc):
    ext_scr[0:SUBLANES, :] = jnp.where(is_first, 0.0, prev_ref[...])
    ext_scr[SUBLANES:SUBLANES + tc, :] = cur_ref[...]
    ext_scr[SUBLANES + tc:, :] = jnp.where(is_last, 0.0, next_ref[...])
    out = cb_ref[...] + cw_ref[0:1, :] * ext_scr[SUBLANES - 1:SUBLANES - 1 + tc, :]
    for j in range(1, 4):
        out = out + cw_ref[j:j + 1, :] * ext_scr[SUBLANES - 1 + j:SUBLANES - 1 + j + tc, :]
    return out


def _lru_gates(xb, d, wa_ref, ba_ref, wx_ref, bx_ref, lam_ref, a_scr, u_scr):
    xb_bf = xb.astype(BF16)
    for b in range(HALF_W // LRU_BLOCK_DIM):
        sl = slice(b * LRU_BLOCK_DIM, (b + 1) * LRU_BLOCK_DIM)
        xs = xb_bf[:, sl]
        r = jax.nn.sigmoid(jnp.dot(xs, wa_ref[d, b].astype(BF16), preferred_element_type=F32)
                           + ba_ref[d:d + 1, sl])
        ig = jax.nn.sigmoid(jnp.dot(xs, wx_ref[d, b].astype(BF16), preferred_element_type=F32)
                            + bx_ref[d:d + 1, sl])
        log_a = -LRU_C * r * jax.nn.softplus(-lam_ref[d:d + 1, sl])
        a_scr[d, :, sl] = jnp.exp(log_a)
        u_scr[d, :, sl] = jnp.sqrt(1.0 - jnp.exp(2.0 * log_a)) * ig * xb[:, sl]


def _lru_kernel(fc_ref, fp_ref, fn_ref, rc_ref, rp_ref, rn_ref,
                cw_ref, cb_ref, wa_ref, ba_ref, wx_ref, bx_ref, lam_ref, h0_ref,
                hf_ref, hr_ref, ext_scr, a_scr, u_scr, cf_scr, cr_scr, *, tc, nchunks):
    i = pl.program_id(1)
    w = HALF_W

    @pl.when(i == 0)
    def _():
        cf_scr[...] = jnp.broadcast_to(h0_ref[0:1, :], (SUBLANES, w))
        cr_scr[...] = jnp.broadcast_to(h0_ref[1:2, :], (SUBLANES, w))

    xb_f = _lru_conv(ext_scr, fc_ref, fp_ref, fn_ref, i == 0, i == nchunks - 1, cw_ref, cb_ref, tc)
    _lru_gates(xb_f, 0, wa_ref, ba_ref, wx_ref, bx_ref, lam_ref, a_scr, u_scr)
    xb_r = _lru_conv(ext_scr, rc_ref, rp_ref, rn_ref, i == nchunks - 1, i == 0, cw_ref, cb_ref, tc)
    _lru_gates(xb_r, 1, wa_ref, ba_ref, wx_ref, bx_ref, lam_ref, a_scr, u_scr)

    row = lax.broadcasted_iota(jnp.int32, (SUBLANES, w), 0)
    ntile = tc // SUBLANES

    def body(j, carry):
        cf, cr = carry
        off = pl.multiple_of(j * SUBLANES, SUBLANES)
        a = a_scr[0, pl.ds(off, SUBLANES), :]
        u = u_scr[0, pl.ds(off, SUBLANES), :]
        for s in (1, 2, 4):
            m = row >= s
            u = jnp.where(m, u + a * pltpu.roll(u, s, 0), u)
            a = jnp.where(m, a * pltpu.roll(a, s, 0), a)
        h = u + a * cf
        hf_ref[pl.ds(off, SUBLANES), :] = h
        cf = jnp.broadcast_to(h[SUBLANES - 1:SUBLANES, :], (SUBLANES, w))
        offr = pl.multiple_of((ntile - 1 - j) * SUBLANES, SUBLANES)
        a = a_scr[1, pl.ds(offr, SUBLANES), :]
        u = u_scr[1, pl.ds(offr, SUBLANES), :]
        for s in (1, 2, 4):
            m = row < SUBLANES - s
            u = jnp.where(m, u + a * pltpu.roll(u, SUBLANES - s, 0), u)
            a = jnp.where(m, a * pltpu.roll(a, SUBLANES - s, 0), a)
        h = u + a * cr
        hr_ref[pl.ds(offr, SUBLANES), :] = h
        cr = jnp.broadcast_to(h[0:1, :], (SUBLANES, w))
        return cf, cr

    cf, cr = lax.fori_loop(0, ntile, body, (cf_scr[...], cr_scr[...]))
    cf_scr[...] = cf
    cr_scr[...] = cr


def _lru_call(p, h0, cw, cb, wa, ba, wx, bx, lam, *, tc):
    t = p.shape[0]
    nchunks = t // tc
    w = HALF_W
    c0 = OFF_LX // w
    r8 = tc // SUBLANES
    last8 = t // SUBLANES - 1
    fwd = lambda i: i
    rev = lambda i: nchunks - 1 - i
    cur_spec = lambda f: pl.BlockSpec((tc, w), lambda h, i: (f(i), c0 + h))
    prev_spec = lambda f: pl.BlockSpec((SUBLANES, w), lambda h, i: (jnp.maximum(f(i) * r8 - 1, 0), c0 + h))
    next_spec = lambda f: pl.BlockSpec((SUBLANES, w), lambda h, i: (jnp.minimum((f(i) + 1) * r8, last8), c0 + h))
    vec_spec = lambda rows: pl.BlockSpec((rows, w), lambda h, i: (0, h))
    gate_w_spec = pl.BlockSpec((2, w // LRU_BLOCK_DIM, LRU_BLOCK_DIM, LRU_BLOCK_DIM),
                               lambda h, i: (0, h, 0, 0))
    kern = functools.partial(_lru_kernel, tc=tc, nchunks=nchunks)
    return pl.pallas_call(
        kern,
        out_shape=(jax.ShapeDtypeStruct((t, LRU_WIDTH), F32), jax.ShapeDtypeStruct((t, LRU_WIDTH), F32)),
        grid=(LRU_WIDTH // w, nchunks),
        in_specs=[cur_spec(fwd), prev_spec(fwd), next_spec(fwd),
                  cur_spec(rev), prev_spec(rev), next_spec(rev),
                  vec_spec(4), vec_spec(1), gate_w_spec, vec_spec(2), gate_w_spec, vec_spec(2),
                  vec_spec(2), vec_spec(2)],
        out_specs=(pl.BlockSpec((tc, w), lambda h, i: (i, h)),
                   pl.BlockSpec((tc, w), lambda h, i: (nchunks - 1 - i, h))),
        scratch_shapes=[pltpu.VMEM((tc + 2 * SUBLANES, w), F32),
                        pltpu.VMEM((2, tc, w), F32), pltpu.VMEM((2, tc, w), F32),
                        pltpu.VMEM((SUBLANES, w), F32), pltpu.VMEM((SUBLANES, w), F32)],
        compiler_params=_cparams("arbitrary", "arbitrary"),
        name="rg_lru",
    )(p, p, p, p, p, p, cw, cb.reshape(1, LRU_WIDTH), wa, ba.reshape(2, LRU_WIDTH),
      wx, bx.reshape(2, LRU_WIDTH), lam, h0)


def _prep_kernel(lg_ref, hf_ref, hr_ref, sb_ref, scc_ref, scp_ref, scn_ref,
                 sxc_ref, sxp_ref, sxn_ref, w3_ref, ylru_ref, ysc_ref, ext_scr, *, tm, ntiles):
    i = pl.program_id(0)
    ylru_ref[...] = ((hf_ref[...] + hr_ref[...]) * jax.nn.gelu(lg_ref[...])).astype(ylru_ref.dtype)
    ext_scr[0:SUBLANES, :] = jnp.where(i == 0, 0.0, scp_ref[...] * sxp_ref[...])
    ext_scr[SUBLANES:SUBLANES + tm, :] = scc_ref[...] * sxc_ref[...]
    ext_scr[SUBLANES + tm:, :] = jnp.where(i == ntiles - 1, 0.0, scn_ref[...] * sxn_ref[...])
    conv = w3_ref[0:1, :] * ext_scr[SUBLANES - 1:SUBLANES - 1 + tm, :]
    for j in range(1, 3):
        conv = conv + w3_ref[j:j + 1, :] * ext_scr[SUBLANES - 1 + j:SUBLANES - 1 + j + tm, :]
    ysc_ref[...] = (sb_ref[...] * conv).astype(ysc_ref.dtype)


def _prep_call(p, hf, hr, w3, *, tm):
    t = p.shape[0]
    ntiles = t // tm
    w = HALF_W
    r8 = tm // SUBLANES
    last8 = t // SUBLANES - 1
    col = lambda off: pl.BlockSpec((tm, w), lambda i, h: (i, off // w + h))
    colp = lambda off: pl.BlockSpec((SUBLANES, w), lambda i, h: (jnp.maximum(i * r8 - 1, 0), off // w + h))
    coln = lambda off: pl.BlockSpec((SUBLANES, w), lambda i, h: (jnp.minimum((i + 1) * r8, last8), off // w + h))
    own = pl.BlockSpec((tm, w), lambda i, h: (i, h))
    kern = functools.partial(_prep_kernel, tm=tm, ntiles=ntiles)
    return pl.pallas_call(
        kern,
        out_shape=(jax.ShapeDtypeStruct((t, LRU_WIDTH), BF16), jax.ShapeDtypeStruct((t, SC_WIDTH), BF16)),
        grid=(ntiles, LRU_WIDTH // w),
        in_specs=[col(OFF_LG), own, own, col(OFF_SB),
                  col(OFF_SC), colp(OFF_SC), coln(OFF_SC),
                  col(OFF_SX), colp(OFF_SX), coln(OFF_SX),
                  pl.BlockSpec((3, w), lambda i, h: (0, h))],
        out_specs=(own, own),
        scratch_shapes=[pltpu.VMEM((tm + 2 * SUBLANES, w), F32)],
        compiler_params=_cparams("arbitrary", "arbitrary"),
        name="branch_prep",
    )(p, hf, hr, p, p, p, p, p, p, p, w3)


def _mergez_kernel(b0_ref, b1_ref, b2_ref, wb_ref, g0_ref, g1_ref, g2_ref, z_ref):
    acc = None
    for j, (b_ref, g_ref) in enumerate(((b0_ref, g0_ref), (b1_ref, g1_ref), (b2_ref, g2_ref))):
        t = jax.nn.sigmoid(g_ref[...]) * jnp.dot(b_ref[...], wb_ref[j].astype(BF16),
                                                  preferred_element_type=F32)
        acc = t if acc is None else acc + t
    z_ref[...] = acc.astype(z_ref.dtype)


def _mergez_call(y_att, y_lru, y_sc, wb_all, p, layer, *, tm):
    t = p.shape[0]
    tn = 256
    br = pl.BlockSpec((tm, BRANCH_W), lambda i, j: (i, 0))
    gate = lambda k: pl.BlockSpec((tm, tn), lambda i, j: (i, (OFF_GATE + k * D_MODEL) // tn + j))
    return pl.pallas_call(
        _mergez_kernel,
        out_shape=jax.ShapeDtypeStruct((t, D_MODEL), BF16),
        grid=(t // tm, D_MODEL // tn),
        in_specs=[br, br, br,
                  pl.BlockSpec((None, N_BRANCH, BRANCH_W, tn), lambda i, j: (layer, 0, 0, j)),
                  gate(0), gate(1), gate(2)],
        out_specs=pl.BlockSpec((tm, tn), lambda i, j: (i, j)),
        compiler_params=pltpu.CompilerParams(dimension_semantics=("arbitrary", "arbitrary"),
                                             vmem_limit_bytes=BIG_VMEM_LIMIT_BYTES),
        name="branch_merge",
    )(y_att, y_lru, y_sc, wb_all, p, p, p)


def _post_norm_store(r, g_ref, b_ref, nsh_ref, nsc_ref, o_ref, on_ref):
    o = _layer_norm(r) * g_ref[...] + b_ref[...]
    o_ref[...] = o
    if on_ref is None:
        return None
    on = _layer_norm(o) * (1.0 + nsc_ref[...]) + nsh_ref[...]
    on_ref[...] = on.astype(on_ref.dtype)
    return on


def _outnorm_kernel(z_ref, w_ref, x_ref, gate_ref, g_ref, b_ref, *rest, fused, routed):
    nsh_ref = nsc_ref = on_ref = wr_ref = br_ref = route_ref = None
    if routed:
        nsh_ref, nsc_ref, wr_ref, br_ref, o_ref, on_ref, route_ref, w_scr = rest
    elif fused:
        nsh_ref, nsc_ref, o_ref, on_ref, w_scr = rest
    else:
        o_ref, w_scr = rest

    @pl.when(pl.program_id(0) == 0)
    def _():
        w_scr[...] = w_ref[...].astype(BF16)

    y = jnp.dot(z_ref[...], w_scr[...], preferred_element_type=F32)
    r = ALPHA * x_ref[...] + gate_ref[...] * y
    on = _post_norm_store(r, g_ref, b_ref, nsh_ref, nsc_ref, o_ref, on_ref)
    if routed:
        route_ref[...] = _route(on, wr_ref, br_ref)


def _outnorm_call(z, w_all, layer, x, mod, gate_chunk, g, b, next_mod=None, next_dtype=None,
                  router=None):
    t = z.shape[0]
    tm = 256
    row = pl.BlockSpec((tm, D_MODEL), lambda i: (i, 0))
    vec = pl.BlockSpec((1, D_MODEL), lambda i: (0, 0))
    chunk = lambda c: pl.BlockSpec((1, D_MODEL), lambda i: (0, c))
    in_specs = [row,
                pl.BlockSpec((None, D_MODEL, D_MODEL), lambda i: (layer, 0, 0),
                             pipeline_mode=pl.Buffered(1)),
                row, chunk(gate_chunk), vec, vec]
    args = [z, w_all, x, mod, g.reshape(1, D_MODEL), b.reshape(1, D_MODEL)]
    out_shape = jax.ShapeDtypeStruct((t, D_MODEL), F32)
    out_specs = row
    if next_mod is not None:
        nmod, sh, sc = next_mod
        in_specs += [chunk(sh), chunk(sc)]
        args += [nmod, nmod]
        out_shape = (out_shape, jax.ShapeDtypeStruct((t, D_MODEL), next_dtype))
        out_specs = (row, row)
    if router is not None:
        r_args, r_specs = _router_operands(*router)
        in_specs += r_specs
        args += r_args
        out_shape = out_shape + (jax.ShapeDtypeStruct((t, LANES), F32),)
        out_specs = out_specs + (pl.BlockSpec((tm, LANES), lambda i: (i, 0)),)
    return pl.pallas_call(
        functools.partial(_outnorm_kernel, fused=next_mod is not None, routed=router is not None),
        out_shape=out_shape,
        grid=(t // tm,),
        in_specs=in_specs,
        out_specs=out_specs,
        scratch_shapes=[pltpu.VMEM((D_MODEL, D_MODEL), BF16)],
        compiler_params=_cparams("arbitrary"),
        name="out_proj_norm",
    )(*args)


def _route(xf, w_ref, b_ref):
    logits = jnp.dot(xf.astype(BF16), w_ref[...], preferred_element_type=F32) + b_ref[...]
    lane = lax.broadcasted_iota(jnp.int32, logits.shape, 1)
    neg = -jnp.inf
    is_grp = lane < N_GROUPS
    gl = jnp.where(is_grp, logits, neg)
    gmax = jnp.max(gl, axis=-1, keepdims=True)
    g_idx = jnp.min(jnp.where(gl == gmax, lane, LANES), axis=-1, keepdims=True)
    gsum = jnp.sum(jnp.where(is_grp, jnp.exp(gl - gmax), 0.0), axis=-1, keepdims=True)
    g_w = 1.0 / gsum
    lo = N_GROUPS + g_idx * EXPERTS_PER_GROUP
    in_grp = jnp.where(lane >= lo, 1, 0) * jnp.where(lane < lo + EXPERTS_PER_GROUP, 1, 0)
    sel = jnp.where(in_grp > 0, logits, neg)
    v0 = jnp.max(sel, axis=-1, keepdims=True)
    i0 = jnp.min(jnp.where(sel == v0, lane, LANES), axis=-1, keepdims=True)
    sel1 = jnp.where(lane == i0, neg, sel)
    v1 = jnp.max(sel1, axis=-1, keepdims=True)
    i1 = jnp.min(jnp.where(sel1 == v1, lane, LANES), axis=-1, keepdims=True)
    e1 = jnp.exp(v1 - v0)
    den = 1.0 + e1
    w0 = (1.0 / den) * g_w
    w1 = (e1 / den) * g_w
    return jnp.where(lane == 0, (i0 - N_GROUPS).astype(F32),
                     jnp.where(lane == 1, (i1 - N_GROUPS).astype(F32),
                               jnp.where(lane == 2, w0, jnp.where(lane == 3, w1, 0.0))))


def _router_operands(w_grp, b_grp, w_exp, b_exp):
    pad = LANES - N_GROUPS - N_EXPERTS
    wr = jnp.concatenate([w_grp, w_exp, jnp.zeros((D_MODEL, pad), F32)], axis=1).astype(BF16)
    br = jnp.concatenate([b_grp, b_exp, jnp.zeros((pad,), F32)]).reshape(1, LANES)
    specs = [pl.BlockSpec((D_MODEL, LANES), lambda i: (0, 0)), pl.BlockSpec((1, LANES), lambda i: (0, 0))]
    return [wr, br], specs


MOE_UNROLL = 8


def _tile_rows(q0_ref, cnt_ref, order_ref, tile, fn, static_rows):
    base = q0_ref[tile]
    cnt = cnt_ref[tile]
    if static_rows:
        for r in range(MOE_TM):
            fn(r, order_ref[base + r], r < cnt)
        return

    def body(blk, c):
        for k in range(MOE_UNROLL):
            r = blk * MOE_UNROLL + k
            fn(r, order_ref[base + r], r < cnt)
        return c
    lax.fori_loop(0, MOE_TM // MOE_UNROLL, body, 0)


def _is_new_expert(te_ref, i):
    return jnp.logical_or(i == 0, te_ref[i] != te_ref[jnp.maximum(i - 1, 0)])


def _moe_kernel(te_ref, nx_ref, q0_ref, cnt_ref, nt_ref, order_ref, tok_ref,
                x_hbm, wg_hbm, wu_hbm, wd_hbm, y_hbm,
                xbuf, ybuf, wg_stage, wu_stage, wd_stage, wg_scr, wu_scr, wd_scr,
                gsem, ssem, wsem, *, layer, ntok):
    i = pl.program_id(0)
    nt = nt_ref[0]

    def weight_copies(e):
        return (pltpu.make_async_copy(wg_hbm.at[layer, e], wg_stage, wsem.at[0]),
                pltpu.make_async_copy(wu_hbm.at[layer, e], wu_stage, wsem.at[1]),
                pltpu.make_async_copy(wd_hbm.at[layer, e], wd_stage, wsem.at[2]))

    def start_gather(tile, s, static_rows):
        def issue(r, tok, valid):
            pltpu.make_async_copy(x_hbm.at[pl.ds(tok, 1), :], xbuf.at[s, pl.ds(r, 1), :],
                                  gsem.at[s]).start()
        _tile_rows(q0_ref, cnt_ref, tok_ref, tile, issue, static_rows)

    def wait_gather(s):
        pltpu.make_async_copy(x_hbm.at[pl.ds(0, MOE_TM), :], xbuf.at[s], gsem.at[s]).wait()

    def wait_scatter(s):
        pltpu.make_async_copy(ybuf.at[s], y_hbm.at[pl.ds(0, MOE_TM), pl.ds(0, D_MODEL)], ssem.at[s]).wait()

    def start_scatter(tile, s):
        def issue(r, pair, valid):
            dst = jnp.where(valid, pair, 2 * ntok + s * MOE_TM + r)
            tok = lax.shift_right_logical(dst, 1)
            col = pl.multiple_of((dst & 1) * D_MODEL, D_MODEL)
            pltpu.make_async_copy(ybuf.at[s, pl.ds(r, 1), :],
                                  y_hbm.at[pl.ds(tok, 1), pl.ds(col, D_MODEL)], ssem.at[s]).start()
        _tile_rows(q0_ref, cnt_ref, order_ref, tile, issue, True)

    @pl.when(i == 0)
    def _():
        start_gather(0, 0, False)
        for copy in weight_copies(te_ref[0]):
            copy.start()
        ybuf[0] = jnp.zeros(ybuf.shape[1:], ybuf.dtype)
        for half in range(2):
            spare = pltpu.make_async_copy(
                ybuf.at[0], y_hbm.at[pl.ds(ntok, MOE_TM), pl.ds(half * D_MODEL, D_MODEL)], ssem.at[0])
            spare.start()
            spare.wait()

    @pl.when(_is_new_expert(te_ref, i))
    def _():
        for copy in weight_copies(te_ref[i]):
            copy.wait()
        wg_scr[...] = wg_stage[...].astype(BF16)
        wu_scr[...] = wu_stage[...].astype(BF16)
        wd_scr[...] = wd_stage[...].astype(BF16)

        @pl.when(nx_ref[i] >= 0)
        def _():
            for copy in weight_copies(nx_ref[i]):
                copy.start()

    for s in range(2):
        @pl.when(jnp.logical_and(i < nt, i % 2 == s))
        def _(s=s):
            wait_gather(s)

            @pl.when(i >= 2)
            def _():
                wait_scatter(s)

            start_gather(jnp.minimum(i + 1, nt - 1), 1 - s, True)
            x = xbuf[s].astype(BF16)
            hg = jnp.dot(x, wg_scr[...], preferred_element_type=F32)
            hu = jnp.dot(x, wu_scr[...], preferred_element_type=F32)
            h = (jax.nn.silu(hg) * hu).astype(BF16)
            ybuf[s] = jnp.dot(h, wd_scr[...], preferred_element_type=F32)
            start_scatter(i, s)

            @pl.when(i == nt - 1)
            def _():
                wait_gather(1 - s)

                @pl.when(i >= 1)
                def _():
                    wait_scatter(1 - s)
                wait_scatter(s)


def _moe_experts(plan, xf, w_gate, w_up, w_down, layer):
    te, nxt, q0, cnt, nt, order, tok = plan
    n = xf.shape[0]
    ntiles = te.shape[0]
    hbm = pl.BlockSpec(memory_space=pl.ANY)
    return pl.pallas_call(
        functools.partial(_moe_kernel, layer=layer, ntok=n),
        out_shape=jax.ShapeDtypeStruct((n + MOE_TM, 2 * D_MODEL), F32),
        grid_spec=pltpu.PrefetchScalarGridSpec(
            num_scalar_prefetch=7,
            grid=(ntiles,),
            in_specs=[hbm, hbm, hbm, hbm],
            out_specs=hbm,
            scratch_shapes=[pltpu.VMEM((2, MOE_TM, D_MODEL), F32),
                            pltpu.VMEM((2, MOE_TM, D_MODEL), F32),
                            pltpu.VMEM((D_MODEL, D_EXPERT), F32),
                            pltpu.VMEM((D_MODEL, D_EXPERT), F32),
                            pltpu.VMEM((D_EXPERT, D_MODEL), F32),
                            pltpu.VMEM((D_MODEL, D_EXPERT), BF16),
                            pltpu.VMEM((D_MODEL, D_EXPERT), BF16),
                            pltpu.VMEM((D_EXPERT, D_MODEL), BF16),
                            pltpu.SemaphoreType.DMA((2,)),
                            pltpu.SemaphoreType.DMA((2,)),
                            pltpu.SemaphoreType.DMA((3,))]),
        compiler_params=pltpu.CompilerParams(dimension_semantics=("arbitrary",),
                                             vmem_limit_bytes=BIG_VMEM_LIMIT_BYTES),
        name="moe_experts",
    )(te, nxt, q0, cnt, nt, order, tok, xf, w_gate, w_up, w_down)


def _lookup(table, idx):
    onehot = idx[:, None] == jnp.arange(table.shape[0], dtype=jnp.int32)[None, :]
    return jnp.sum(jnp.where(onehot, table[None, :], 0), axis=1)


def _moe_plan(eid, tm):
    npair = eid.size
    ntiles = npair // tm + N_EXPERTS
    flat = eid.reshape(-1)
    order = jnp.argsort(flat, stable=True).astype(jnp.int32)
    experts = jnp.arange(N_EXPERTS, dtype=jnp.int32)
    counts = jnp.sum((flat[:, None] == experts[None, :]).astype(jnp.int32), axis=0)
    padded = ((counts + tm - 1) // tm) * tm
    pend = jnp.cumsum(padded)
    pstart = pend - padded
    start = jnp.cumsum(counts) - counts
    nt = pend[-1] // tm
    tiles = jnp.arange(ntiles, dtype=jnp.int32)
    tile_start = tiles * tm
    te = jnp.minimum(jnp.sum((tile_start[:, None] >= pend[None, :]).astype(jnp.int32), axis=1), N_EXPERTS - 1)
    in_use = tiles < nt
    off = tile_start - _lookup(pstart, te)
    q0 = jnp.where(in_use, _lookup(start, te) + off, 0)
    cnt = jnp.where(in_use, jnp.clip(_lookup(counts, te) - off, 0, tm), 0)
    te_last = jnp.sum(jnp.where(tiles == nt - 1, te, 0))
    te = jnp.where(in_use, te, te_last)
    later = (experts[None, :] > experts[:, None]) & (counts[None, :] > 0)
    next_expert = jnp.min(jnp.where(later, experts[None, :], N_EXPERTS), axis=1)
    next_expert = jnp.where(next_expert < N_EXPERTS, next_expert, -1)
    nxt = _lookup(next_expert, te)
    i32 = lambda a: a.astype(jnp.int32)
    order = jnp.concatenate([order, jnp.zeros((tm,), jnp.int32)])
    return i32(te), i32(nxt), i32(q0), i32(cnt), i32(nt).reshape(1), order, order // 2


def _combine_kernel(yp_ref, r_ref, x_ref, gate_ref, g_ref, b_ref, *rest, fused):
    if fused:
        nsh_ref, nsc_ref, o_ref, on_ref = rest
    else:
        (o_ref,), nsh_ref, nsc_ref, on_ref = rest, None, None, None
    w0 = r_ref[:, 2:3]
    w1 = r_ref[:, 3:4]
    m = w0 * yp_ref[:, 0:D_MODEL] + w1 * yp_ref[:, D_MODEL:2 * D_MODEL]
    r = ALPHA * x_ref[...] + gate_ref[...] * m
    _post_norm_store(r, g_ref, b_ref, nsh_ref, nsc_ref, o_ref, on_ref)


def _combine_call(ypair2, route, x, mod, gate_chunk, g, b, row_block0, next_mod=None, next_dtype=None):
    t = x.shape[0]
    tm = 256
    row = pl.BlockSpec((tm, D_MODEL), lambda i: (i, 0))
    vec = pl.BlockSpec((1, D_MODEL), lambda i: (0, 0))
    chunk = lambda c: pl.BlockSpec((1, D_MODEL), lambda i: (0, c))
    in_specs = [pl.BlockSpec((tm, 2 * D_MODEL), lambda i: (i + row_block0, 0)),
                pl.BlockSpec((tm, LANES), lambda i: (i + row_block0, 0)),
                row, chunk(gate_chunk), vec, vec]
    args = [ypair2, route, x, mod, g.reshape(1, D_MODEL), b.reshape(1, D_MODEL)]
    out_shape = jax.ShapeDtypeStruct((t, D_MODEL), F32)
    out_specs = row
    if next_mod is not None:
        nmod, sh, sc = next_mod
        in_specs += [chunk(sh), chunk(sc)]
        args += [nmod, nmod]
        out_shape = (out_shape, jax.ShapeDtypeStruct((t, D_MODEL), next_dtype))
        out_specs = (row, row)
    return pl.pallas_call(
        functools.partial(_combine_kernel, fused=next_mod is not None),
        out_shape=out_shape,
        grid=(t // tm,),
        in_specs=in_specs,
        out_specs=out_specs,
        compiler_params=_cparams("arbitrary"),
        name="moe_combine_norm",
    )(*args)


def _rope_tables(s):
    t = np.arange(s)
    rows = (t // GRID_W).astype(np.float32)
    cols = (t % GRID_W).astype(np.float32)
    half = HEAD_DIM // 2
    inv = np.float32(ROPE_THETA) ** (-np.arange(0, half, 2, dtype=np.float32) / np.float32(half))
    ar = (rows[:, None] * inv[None, :]).astype(np.float32)
    ac = (cols[:, None] * inv[None, :]).astype(np.float32)
    cos = np.concatenate([np.cos(ar), np.cos(ar), np.cos(ac), np.cos(ac)], axis=-1)
    sin_signed = np.concatenate([-np.sin(ar), np.sin(ar), -np.sin(ac), np.sin(ac)], axis=-1)
    return jnp.asarray(cos, F32), jnp.asarray(sin_signed, F32)


def _moe(xf, route, w_gate, w_up, w_down, layer):
    eid = route[:, 0:2].astype(jnp.int32)
    return _moe_experts(_moe_plan(eid, MOE_TM), xf, w_gate, w_up, w_down, layer)


def kernel(x, c, ctx, c_ctx, w_mod, b_mod, w_in, b_in, attn_sink, lru_conv_w, lru_conv_b,
           lru_w_a, lru_b_a, lru_w_x, lru_b_x, lru_lam, sc_conv_w, w_branch, w_out,
           ln1_g, ln1_b, ln2_g, ln2_b, w_grp, b_grp, w_exp, b_exp, w_gate, w_up, w_down):
    assert x.shape == (1, SEQ, D_MODEL) and ctx.shape == (1, CTX_LEN, D_MODEL)
    xs = x[0]
    xc = ctx[0]
    cos, sin_signed = _rope_tables(SEQ)
    c8 = jnp.concatenate([c, c_ctx[None, :], jnp.zeros((SUBLANES - 2, D_MODEL), F32)], axis=0)
    zero_h0 = jnp.zeros((2, LRU_WIDTH), F32)

    mods = [_mod_call(c8, w_mod, b_mod[l], l) for l in range(DEPTH)]
    u = u_ctx = None
    for l in range(DEPTH):
        last = l == DEPTH - 1
        m_lat = mods[l][0:1]
        m_ctx = mods[l][1:2]

        if l == 0:
            u = _lnmod_call(xs, m_lat, 0, 1, BF16)
            u_ctx = _lnmod_call(xc, m_ctx, 0, 1, BF16)
        p, p_ctx = _proj_call(u, u_ctx, w_in, b_in[l], cos, sin_signed, l, tm=1024)

        y_att = _attn_call(p, p_ctx, attn_sink[l])
        lru_args = (lru_conv_w[l], lru_conv_b[l], lru_w_a[l], lru_b_a[l], lru_w_x[l], lru_b_x[l], lru_lam[l])
        hcf, hcr = _lru_call(p_ctx, zero_h0, *lru_args, tc=CTX_LEN)
        h0 = jnp.concatenate([hcf[CTX_LEN - 1:CTX_LEN], hcr[0:1]], axis=0)
        hf, hr = _lru_call(p, h0, *lru_args, tc=512)
        y_lru, y_sc = _prep_call(p, hf, hr, sc_conv_w[l], tm=512)
        z = _mergez_call(y_att, y_lru, y_sc, w_branch, p, l, tm=2048)

        router = (w_grp[l], b_grp[l], w_exp[l], b_exp[l])
        if last:
            xs, xf, route = _outnorm_call(z, w_out, l, xs, m_lat, 2, ln1_g[l], ln1_b[l],
                                          next_mod=(m_lat, 3, 4), next_dtype=F32, router=router)
            ypair2 = _moe(xf, route, w_gate, w_up, w_down, l)
            xs = _combine_call(ypair2, route, xs, m_lat, 5, ln2_g[l], ln2_b[l], 0)
        else:
            xs = _outnorm_call(z, w_out, l, xs, m_lat, 2, ln1_g[l], ln1_b[l])
            yc_att = _ctx_attn_call(p_ctx, attn_sink[l])
            yc_lru, yc_sc = _prep_call(p_ctx, hcf, hcr, sc_conv_w[l], tm=CTX_LEN)
            zc = _mergez_call(yc_att, yc_lru, yc_sc, w_branch, p_ctx, l, tm=CTX_LEN)
            xc = _outnorm_call(zc, w_out, l, xc, m_ctx, 2, ln1_g[l], ln1_b[l])
            xf, route = _lnmod_pair_call(xs, m_lat, xc, m_ctx, 3, 4, router)
            ypair2 = _moe(xf, route, w_gate, w_up, w_down, l)
            n_lat = mods[l + 1][0:1]
            n_ctx = mods[l + 1][1:2]
            xs, u = _combine_call(ypair2, route, xs, m_lat, 5, ln2_g[l], ln2_b[l], 0,
                                  next_mod=(n_lat, 0, 1), next_dtype=BF16)
            xc, u_ctx = _combine_call(ypair2, route, xc, m_ctx, 5, ln2_g[l], ln2_b[l], SEQ // 256,
                                      next_mod=(n_ctx, 0, 1), next_dtype=BF16)
    return xs[None]
```

```python
import functools

import jax
import jax.numpy as jnp
import numpy as np
from jax import lax
from jax.experimental import pallas as pl
from jax.experimental.pallas import tpu as pltpu

F32 = jnp.float32
BF16 = jnp.bfloat16

D_MODEL = 2048
SEQ = 8192
CTX_LEN = 256
DEPTH = 2
GRID_W = 64

N_HEADS = 8
N_KV_HEADS = 2
HEAD_DIM = 128
Q_GROUPS = N_HEADS // N_KV_HEADS
WINDOW = 128
BLOCK = 128
ROPE_THETA = 10000.0
Q_SCALE = HEAD_DIM ** -0.5

LRU_WIDTH = 1024
LRU_BLOCKS = 8
LRU_BLOCK_DIM = 128
LRU_C = 8.0
SC_WIDTH = 1024
N_BRANCH = 3
BRANCH_W = 1024

Q_W = N_HEADS * HEAD_DIM
KV_W = N_KV_HEADS * HEAD_DIM
OFF_K = Q_W
OFF_V = OFF_K + KV_W
OFF_LX = OFF_V + KV_W
OFF_LG = OFF_LX + LRU_WIDTH
OFF_SB = OFF_LG + LRU_WIDTH
OFF_SC = OFF_SB + SC_WIDTH
OFF_SX = OFF_SC + SC_WIDTH
OFF_GATE = OFF_SX + SC_WIDTH
PROJ_W = OFF_GATE + N_BRANCH * D_MODEL

N_GROUPS = 4
EXPERTS_PER_GROUP = 8
N_EXPERTS = N_GROUPS * EXPERTS_PER_GROUP
D_EXPERT = 1024

ALPHA = (2 * DEPTH) ** 0.25
LN_EPS = 1e-5

VMEM_LIMIT_BYTES = 52 * 1024 * 1024
BIG_VMEM_LIMIT_BYTES = 58 * 1024 * 1024
LANES = 128
SUBLANES = 8

HALF_W = 512
PROJ_TN = 1280
MOE_TM = 256


def _cparams(*sem):
    return pltpu.CompilerParams(dimension_semantics=sem, vmem_limit_bytes=VMEM_LIMIT_BYTES)


def _layer_norm(x):
    mu = jnp.mean(x, axis=-1, keepdims=True)
    xc = x - mu
    var = jnp.mean(xc * xc, axis=-1, keepdims=True)
    return xc * lax.rsqrt(var + LN_EPS)


def _mod_kernel(c_ref, w_ref, b_ref, o_ref):
    a = jax.nn.silu(c_ref[...]).astype(BF16)
    o_ref[...] = jnp.dot(a, w_ref[...].astype(BF16), preferred_element_type=F32) + b_ref[...]


def _mod_call(c8, w_all, b, layer):
    n = w_all.shape[2]
    tn = 1024
    return pl.pallas_call(
        _mod_kernel,
        out_shape=jax.ShapeDtypeStruct((SUBLANES, n), F32),
        grid=(n // tn,),
        in_specs=[pl.BlockSpec((SUBLANES, D_MODEL), lambda j: (0, 0)),
                  pl.BlockSpec((None, D_MODEL, tn), lambda j: (layer, 0, j)),
                  pl.BlockSpec((1, tn), lambda j: (0, j))],
        out_specs=pl.BlockSpec((SUBLANES, tn), lambda j: (0, j)),
        compiler_params=_cparams("arbitrary"),
        name="mod_vectors",
    )(c8, w_all, b.reshape(1, n))


def _lnmod_kernel(x_ref, sh_ref, sc_ref, o_ref):
    y = _layer_norm(x_ref[...])
    o_ref[...] = (y * (1.0 + sc_ref[...]) + sh_ref[...]).astype(o_ref.dtype)


def _lnmod_call(x, mod, shift_chunk, scale_chunk, out_dtype):
    t = x.shape[0]
    tm = 256
    return pl.pallas_call(
        _lnmod_kernel,
        out_shape=jax.ShapeDtypeStruct((t, D_MODEL), out_dtype),
        grid=(t // tm,),
        in_specs=[pl.BlockSpec((tm, D_MODEL), lambda i: (i, 0)),
                  pl.BlockSpec((1, D_MODEL), lambda i: (0, shift_chunk)),
                  pl.BlockSpec((1, D_MODEL), lambda i: (0, scale_chunk))],
        out_specs=pl.BlockSpec((tm, D_MODEL), lambda i: (i, 0)),
        compiler_params=_cparams("arbitrary"),
        name="ln_modulate",
    )(x, mod, mod)


def _lnmod_pair_kernel(xa_ref, xb_ref, sha_ref, sca_ref, shb_ref, scb_ref, wr_ref, br_ref,
                       o_ref, route_ref, *, na):
    i = pl.program_id(0)

    def one(x_ref, sh_ref, sc_ref):
        y = _layer_norm(x_ref[...]) * (1.0 + sc_ref[...]) + sh_ref[...]
        o_ref[...] = y
        route_ref[...] = _route(y, wr_ref, br_ref)

    @pl.when(i < na)
    def _():
        one(xa_ref, sha_ref, sca_ref)

    @pl.when(i >= na)
    def _():
        one(xb_ref, shb_ref, scb_ref)


def _lnmod_pair_call(xa, mod_a, xb, mod_b, shift_chunk, scale_chunk, router):
    tm = 256
    na = xa.shape[0] // tm
    nb = xb.shape[0] // tm
    rows = (na + nb) * tm
    vec = lambda chunk: pl.BlockSpec((1, D_MODEL), lambda i: (0, chunk))
    r_args, r_specs = _router_operands(*router)
    return pl.pallas_call(
        functools.partial(_lnmod_pair_kernel, na=na),
        out_shape=(jax.ShapeDtypeStruct((rows, D_MODEL), F32), jax.ShapeDtypeStruct((rows, LANES), F32)),
        grid=(na + nb,),
        in_specs=[pl.BlockSpec((tm, D_MODEL), lambda i: (jnp.minimum(i, na - 1), 0)),
                  pl.BlockSpec((tm, D_MODEL), lambda i: (jnp.maximum(i - na, 0), 0)),
                  vec(shift_chunk), vec(scale_chunk), vec(shift_chunk), vec(scale_chunk)] + r_specs,
        out_specs=(pl.BlockSpec((tm, D_MODEL), lambda i: (i, 0)),
                   pl.BlockSpec((tm, LANES), lambda i: (i, 0))),
        compiler_params=_cparams("arbitrary"),
        name="ln_modulate_route",
    )(xa, xb, mod_a, mod_a, mod_b, mod_b, *r_args)


def _rope_head(xh, cos, sin_signed, first_half):
    partner = jnp.where(first_half, pltpu.roll(xh, LANES - 32, 1), pltpu.roll(xh, 32, 1))
    return xh * cos + partner * sin_signed


def _proj_kernel(u_ref, uc_ref, w_ref, b_ref, cos_ref, sin_ref, o_ref, oc_ref, w_scr):
    j = pl.program_id(0)
    i = pl.program_id(1)

    @pl.when(i == 0)
    def _():
        w_scr[...] = w_ref[...].astype(BF16)
        oc_ref[...] = jnp.dot(uc_ref[...], w_scr[...], preferred_element_type=F32) + b_ref[...]

    def project():
        return jnp.dot(u_ref[...], w_scr[...], preferred_element_type=F32) + b_ref[...]

    @pl.when(j != 0)
    def _():
        o_ref[...] = project()

    @pl.when(j == 0)
    def _():
        acc = project()
        cos = cos_ref[...]
        sin_signed = sin_ref[...]
        lane = lax.broadcasted_iota(jnp.int32, cos.shape, 1)
        first_half = (lane & 63) < 32
        for h in range(N_HEADS + N_KV_HEADS):
            sl = slice(h * HEAD_DIM, (h + 1) * HEAD_DIM)
            r = _rope_head(acc[:, sl], cos, sin_signed, first_half)
            if h < N_HEADS:
                r = r * Q_SCALE
            o_ref[:, sl] = r


def _proj_call(u, u_ctx, w_all, b, cos, sin_signed, layer, *, tm):
    t = u.shape[0]
    lc = u_ctx.shape[0]
    n = w_all.shape[2]
    return pl.pallas_call(
        _proj_kernel,
        out_shape=(jax.ShapeDtypeStruct((t, n), F32), jax.ShapeDtypeStruct((lc, n), F32)),
        grid=(n // PROJ_TN, t // tm),
        in_specs=[pl.BlockSpec((tm, D_MODEL), lambda j, i: (i, 0)),
                  pl.BlockSpec((lc, D_MODEL), lambda j, i: (0, 0)),
                  pl.BlockSpec((None, D_MODEL, PROJ_TN), lambda j, i: (layer, 0, j)),
                  pl.BlockSpec((1, PROJ_TN), lambda j, i: (0, j)),
                  pl.BlockSpec((tm, HEAD_DIM), lambda j, i: (i, 0)),
                  pl.BlockSpec((tm, HEAD_DIM), lambda j, i: (i, 0))],
        out_specs=(pl.BlockSpec((tm, PROJ_TN), lambda j, i: (i, j)),
                   pl.BlockSpec((lc, PROJ_TN), lambda j, i: (0, j))),
        scratch_shapes=[pltpu.VMEM((D_MODEL, PROJ_TN), BF16)],
        compiler_params=pltpu.CompilerParams(dimension_semantics=("arbitrary", "arbitrary"),
                                             vmem_limit_bytes=BIG_VMEM_LIMIT_BYTES),
        name="in_proj_rope",
    )(u, u_ctx, w_all, b.reshape(1, n), cos, sin_signed)


def _softmax_pv(q, k_all, v_all, bias, sink_col):
    s = lax.dot_general(q, k_all, (((1,), (1,)), ((), ())), preferred_element_type=F32)
    if bias is not None:
        s = s + bias
    m = jnp.maximum(jnp.max(s, axis=-1, keepdims=True), sink_col)
    e = jnp.exp(s - m)
    den = jnp.sum(e, axis=-1, keepdims=True) + jnp.exp(sink_col - m)
    p = (e / den).astype(BF16)
    return jnp.dot(p, v_all, preferred_element_type=F32)


def _sink_column(sink_ref, kv, rows):
    row = lax.broadcasted_iota(jnp.int32, (Q_GROUPS * rows, 1), 0)
    col = jnp.full((Q_GROUPS * rows, 1), sink_ref[kv * Q_GROUPS + Q_GROUPS - 1], F32)
    for g in range(Q_GROUPS - 2, -1, -1):
        col = jnp.where(row < (g + 1) * rows, sink_ref[kv * Q_GROUPS + g], col)
    return col


ATTN_QB = 4


def _attn_kernel(sink_ref, q_ref, kp_ref, kc_ref, kn_ref, vp_ref, vc_ref, vn_ref,
                 kx_ref, vx_ref, o_ref, *, seq):
    n = pl.program_id(0)
    n_loc = 3 * BLOCK
    n_keys = n_loc + CTX_LEN
    qi = lax.broadcasted_iota(jnp.int32, (BLOCK, n_keys), 0)
    kj = lax.broadcasted_iota(jnp.int32, (BLOCK, n_keys), 1)
    rel = kj - BLOCK - qi
    sinks = [_sink_column(sink_ref, kv, BLOCK) for kv in range(N_KV_HEADS)]
    for sub in range(ATTN_QB):
        blk = n * ATTN_QB + sub
        kpos = blk * BLOCK - BLOCK + kj
        bad = jnp.where(kj < n_loc,
                        jnp.where(jnp.abs(rel) > WINDOW, 1, 0) + jnp.where(kpos < 0, 1, 0)
                        + jnp.where(kpos >= seq, 1, 0), 0)
        bias1 = jnp.where(bad > 0, -jnp.inf, 0.0).astype(F32)
        bias = jnp.concatenate([bias1] * Q_GROUPS, axis=0)
        rows = slice(sub * BLOCK, (sub + 1) * BLOCK)

        def band(prev_ref, cur_ref, next_ref, ctx_ref, ks):
            own = [cur_ref[j * BLOCK:(j + 1) * BLOCK, ks] for j in range(ATTN_QB)]
            blocks = [prev_ref[:, ks]] + own + [next_ref[:, ks]]
            return jnp.concatenate(blocks[sub:sub + 3] + [ctx_ref[:, ks]], axis=0).astype(BF16)

        for kv in range(N_KV_HEADS):
            ks = slice(kv * HEAD_DIM, (kv + 1) * HEAD_DIM)
            k_all = band(kp_ref, kc_ref, kn_ref, kx_ref, ks)
            v_all = band(vp_ref, vc_ref, vn_ref, vx_ref, ks)
            q = jnp.concatenate(
                [q_ref[rows, (kv * Q_GROUPS + g) * HEAD_DIM:(kv * Q_GROUPS + g + 1) * HEAD_DIM]
                 for g in range(Q_GROUPS)], axis=0).astype(BF16)
            o = _softmax_pv(q, k_all, v_all, bias, sinks[kv])
            for g in range(Q_GROUPS):
                h = kv * Q_GROUPS + g
                o_ref[rows, h * HEAD_DIM:(h + 1) * HEAD_DIM] = (
                    o[g * BLOCK:(g + 1) * BLOCK].astype(o_ref.dtype))


def _attn_call(p, p_ctx, sink):
    s = p.shape[0]
    nb = s // BLOCK
    step_rows = ATTN_QB * BLOCK
    kcol = OFF_K // KV_W
    vcol = OFF_V // KV_W
    prev = lambda c: pl.BlockSpec((BLOCK, KV_W), lambda n: (jnp.maximum(n * ATTN_QB - 1, 0), c))
    nxt = lambda c: pl.BlockSpec((BLOCK, KV_W), lambda n: (jnp.minimum((n + 1) * ATTN_QB, nb - 1), c))
    cur = lambda c: pl.BlockSpec((step_rows, KV_W), lambda n: (n, c))
    return pl.pallas_call(
        functools.partial(_attn_kernel, seq=s),
        out_shape=jax.ShapeDtypeStruct((s, Q_W), BF16),
        grid=(s // step_rows,),
        in_specs=[pl.BlockSpec(memory_space=pltpu.SMEM),
                  pl.BlockSpec((step_rows, Q_W), lambda n: (n, 0)),
                  prev(kcol), cur(kcol), nxt(kcol),
                  prev(vcol), cur(vcol), nxt(vcol),
                  pl.BlockSpec((CTX_LEN, KV_W), lambda n: (0, kcol)),
                  pl.BlockSpec((CTX_LEN, KV_W), lambda n: (0, vcol))],
        out_specs=pl.BlockSpec((step_rows, Q_W), lambda n: (n, 0)),
        compiler_params=_cparams("arbitrary"),
        name="window_attention",
    )(sink, p, p, p, p, p, p, p, p_ctx, p_ctx)


def _ctx_attn_kernel(sink_ref, q_ref, k_ref, v_ref, o_ref):
    rows = q_ref.shape[0]
    for kv in range(N_KV_HEADS):
        ks = slice(kv * HEAD_DIM, (kv + 1) * HEAD_DIM)
        q = jnp.concatenate(
            [q_ref[:, (kv * Q_GROUPS + g) * HEAD_DIM:(kv * Q_GROUPS + g + 1) * HEAD_DIM] * Q_SCALE
             for g in range(Q_GROUPS)], axis=0).astype(BF16)
        o = _softmax_pv(q, k_ref[:, ks].astype(BF16), v_ref[:, ks].astype(BF16), None,
                        _sink_column(sink_ref, kv, rows))
        for g in range(Q_GROUPS):
            h = kv * Q_GROUPS + g
            o_ref[:, h * HEAD_DIM:(h + 1) * HEAD_DIM] = o[g * rows:(g + 1) * rows].astype(o_ref.dtype)


def _ctx_attn_call(p_ctx, sink):
    lc = p_ctx.shape[0]
    return pl.pallas_call(
        _ctx_attn_kernel,
        out_shape=jax.ShapeDtypeStruct((lc, Q_W), BF16),
        grid=(1,),
        in_specs=[pl.BlockSpec(memory_space=pltpu.SMEM),
                  pl.BlockSpec((lc, Q_W), lambda n: (0, 0)),
                  pl.BlockSpec((lc, KV_W), lambda n: (0, OFF_K // KV_W)),
                  pl.BlockSpec((lc, KV_W), lambda n: (0, OFF_V // KV_W))],
        out_specs=pl.BlockSpec((lc, Q_W), lambda n: (0, 0)),
        compiler_params=_cparams("arbitrary"),
        name="context_attention",
    )(sink, p_ctx, p_ctx, p_ctx)


def _lru_conv(ext_scr, cur_ref, prev_ref, next_ref, is_first, is_last, cw_ref, cb_ref, tc):
    ext_scr[0:SUBLANES, :] = jnp.where(is_first, 0.0, prev_ref[...])
    ext_scr[SUBLANES:SUBLANES + tc, :] = cur_ref[...]
    ext_scr[SUBLANES + tc:, :] = jnp.where(is_last, 0.0, next_ref[...])
    out = cb_ref[...] + cw_ref[0:1, :] * ext_scr[SUBLANES - 1:SUBLANES - 1 + tc, :]
    for j in range(1, 4):
        out = out + cw_ref[j:j + 1, :] * ext_scr[SUBLANES - 1 + j:SUBLANES - 1 + j + tc, :]
    return out


def _lru_gates(xb, d, wa_ref, ba_ref, wx_ref, bx_ref, lam_ref, a_scr, u_scr):
    xb_bf = xb.astype(BF16)
    for b in range(HALF_W // LRU_BLOCK_DIM):
        sl = slice(b * LRU_BLOCK_DIM, (b + 1) * LRU_BLOCK_DIM)
        xs = xb_bf[:, sl]
        r = jax.nn.sigmoid(jnp.dot(xs, wa_ref[d, b].astype(BF16), preferred_element_type=F32)
                           + ba_ref[d:d + 1, sl])
        ig = jax.nn.sigmoid(jnp.dot(xs, wx_ref[d, b].astype(BF16), preferred_element_type=F32)
                            + bx_ref[d:d + 1, sl])
        log_a = -LRU_C * r * jax.nn.softplus(-lam_ref[d:d + 1, sl])
        a_scr[d, :, sl] = jnp.exp(log_a)
        u_scr[d, :, sl] = jnp.sqrt(1.0 - jnp.exp(2.0 * log_a)) * ig * xb[:, sl]


def _lru_kernel(fc_ref, fp_ref, fn_ref, rc_ref, rp_ref, rn_ref,
                cw_ref, cb_ref, wa_ref, ba_ref, wx_ref, bx_ref, lam_ref, h0_ref,
                hf_ref, hr_ref, ext_scr, a_scr, u_scr, cf_scr, cr_scr, *, tc, nchunks):
    i = pl.program_id(1)
    w = HALF_W

    @pl.when(i == 0)
    def _():
        cf_scr[...] = jnp.broadcast_to(h0_ref[0:1, :], (SUBLANES, w))
        cr_scr[...] = jnp.broadcast_to(h0_ref[1:2, :], (SUBLANES, w))

    xb_f = _lru_conv(ext_scr, fc_ref, fp_ref, fn_ref, i == 0, i == nchunks - 1, cw_ref, cb_ref, tc)
    _lru_gates(xb_f, 0, wa_ref, ba_ref, wx_ref, bx_ref, lam_ref, a_scr, u_scr)
    xb_r = _lru_conv(ext_scr, rc_ref, rp_ref, rn_ref, i == nchunks - 1, i == 0, cw_ref, cb_ref, tc)
    _lru_gates(xb_r, 1, wa_ref, ba_ref, wx_ref, bx_ref, lam_ref, a_scr, u_scr)

    row = lax.broadcasted_iota(jnp.int32, (SUBLANES, w), 0)
    ntile = tc // SUBLANES

    def body(j, carry):
        cf, cr = carry
        off = pl.multiple_of(j * SUBLANES, SUBLANES)
        a = a_scr[0, pl.ds(off, SUBLANES), :]
        u = u_scr[0, pl.ds(off, SUBLANES), :]
        for s in (1, 2, 4):
            m = row >= s
            u = jnp.where(m, u + a * pltpu.roll(u, s, 0), u)
            a = jnp.where(m, a * pltpu.roll(a, s, 0), a)
        h = u + a * cf
        hf_ref[pl.ds(off, SUBLANES), :] = h
        cf = jnp.broadcast_to(h[SUBLANES - 1:SUBLANES, :], (SUBLANES, w))
        offr = pl.multiple_of((ntile - 1 - j) * SUBLANES, SUBLANES)
        a = a_scr[1, pl.ds(offr, SUBLANES), :]
        u = u_scr[1, pl.ds(offr, SUBLANES), :]
        for s in (1, 2, 4):
            m = row < SUBLANES - s
            u = jnp.where(m, u + a * pltpu.roll(u, SUBLANES - s, 0), u)
            a = jnp.where(m, a * pltpu.roll(a, SUBLANES - s, 0), a)
        h = u + a * cr
        hr_ref[pl.ds(offr, SUBLANES), :] = h
        cr = jnp.broadcast_to(h[0:1, :], (SUBLANES, w))
        return cf, cr

    cf, cr = lax.fori_loop(0, ntile, body, (cf_scr[...], cr_scr[...]))
    cf_scr[...] = cf
    cr_scr[...] = cr


def _lru_call(p, h0, cw, cb, wa, ba, wx, bx, lam, *, tc):
    t = p.shape[0]
    nchunks = t // tc
    w = HALF_W
    c0 = OFF_LX // w
    r8 = tc // SUBLANES
    last8 = t // SUBLANES - 1
    fwd = lambda i: i
    rev = lambda i: nchunks - 1 - i
    cur_spec = lambda f: pl.BlockSpec((tc, w), lambda h, i: (f(i), c0 + h))
    prev_spec = lambda f: pl.BlockSpec((SUBLANES, w), lambda h, i: (jnp.maximum(f(i) * r8 - 1, 0), c0 + h))
    next_spec = lambda f: pl.BlockSpec((SUBLANES, w), lambda h, i: (jnp.minimum((f(i) + 1) * r8, last8), c0 + h))
    vec_spec = lambda rows: pl.BlockSpec((rows, w), lambda h, i: (0, h))
    gate_w_spec = pl.BlockSpec((2, w // LRU_BLOCK_DIM, LRU_BLOCK_DIM, LRU_BLOCK_DIM),
                               lambda h, i: (0, h, 0, 0))
    kern = functools.partial(_lru_kernel, tc=tc, nchunks=nchunks)
    return pl.pallas_call(
        kern,
        out_shape=(jax.ShapeDtypeStruct((t, LRU_WIDTH), F32), jax.ShapeDtypeStruct((t, LRU_WIDTH), F32)),
        grid=(LRU_WIDTH // w, nchunks),
        in_specs=[cur_spec(fwd), prev_spec(fwd), next_spec(fwd),
                  cur_spec(rev), prev_spec(rev), next_spec(rev),
                  vec_spec(4), vec_spec(1), gate_w_spec, vec_spec(2), gate_w_spec, vec_spec(2),
                  vec_spec(2), vec_spec(2)],
        out_specs=(pl.BlockSpec((tc, w), lambda h, i: (i, h)),
                   pl.BlockSpec((tc, w), lambda h, i: (nchunks - 1 - i, h))),
        scratch_shapes=[pltpu.VMEM((tc + 2 * SUBLANES, w), F32),
                        pltpu.VMEM((2, tc, w), F32), pltpu.VMEM((2, tc, w), F32),
                        pltpu.VMEM((SUBLANES, w), F32), pltpu.VMEM((SUBLANES, w), F32)],
        compiler_params=_cparams("arbitrary", "arbitrary"),
        name="rg_lru",
    )(p, p, p, p, p, p, cw, cb.reshape(1, LRU_WIDTH), wa, ba.reshape(2, LRU_WIDTH),
      wx, bx.reshape(2, LRU_WIDTH), lam, h0)


def _prep_kernel(lg_ref, hf_ref, hr_ref, sb_ref, scc_ref, scp_ref, scn_ref,
                 sxc_ref, sxp_ref, sxn_ref, w3_ref, ylru_ref, ysc_ref, ext_scr, *, tm, ntiles):
    i = pl.program_id(0)
    ylru_ref[...] = ((hf_ref[...] + hr_ref[...]) * jax.nn.gelu(lg_ref[...])).astype(ylru_ref.dtype)
    ext_scr[0:SUBLANES, :] = jnp.where(i == 0, 0.0, scp_ref[...] * sxp_ref[...])
    ext_scr[SUBLANES:SUBLANES + tm, :] = scc_ref[...] * sxc_ref[...]
    ext_scr[SUBLANES + tm:, :] = jnp.where(i == ntiles - 1, 0.0, scn_ref[...] * sxn_ref[...])
    conv = w3_ref[0:1, :] * ext_scr[SUBLANES - 1:SUBLANES - 1 + tm, :]
    for j in range(1, 3):
        conv = conv + w3_ref[j:j + 1, :] * ext_scr[SUBLANES - 1 + j:SUBLANES - 1 + j + tm, :]
    ysc_ref[...] = (sb_ref[...] * conv).astype(ysc_ref.dtype)


def _prep_call(p, hf, hr, w3, *, tm):
    t = p.shape[0]
    ntiles = t // tm
    w = HALF_W
    r8 = tm // SUBLANES
    last8 = t // SUBLANES - 1
    col = lambda off: pl.BlockSpec((tm, w), lambda i, h: (i, off // w + h))
    colp = lambda off: pl.BlockSpec((SUBLANES, w), lambda i, h: (jnp.maximum(i * r8 - 1, 0), off // w + h))
    coln = lambda off: pl.BlockSpec((SUBLANES, w), lambda i, h: (jnp.minimum((i + 1) * r8, last8), off // w + h))
    own = pl.BlockSpec((tm, w), lambda i, h: (i, h))
    kern = functools.partial(_prep_kernel, tm=tm, ntiles=ntiles)
    return pl.pallas_call(
        kern,
        out_shape=(jax.ShapeDtypeStruct((t, LRU_WIDTH), BF16), jax.ShapeDtypeStruct((t, SC_WIDTH), BF16)),
        grid=(ntiles, LRU_WIDTH // w),
        in_specs=[col(OFF_LG), own, own, col(OFF_SB),
                  col(OFF_SC), colp(OFF_SC), coln(OFF_SC),
                  col(OFF_SX), colp(OFF_SX), coln(OFF_SX),
                  pl.BlockSpec((3, w), lambda i, h: (0, h))],
        out_specs=(own, own),
        scratch_shapes=[pltpu.VMEM((tm + 2 * SUBLANES, w), F32)],
        compiler_params=_cparams("arbitrary", "arbitrary"),
        name="branch_prep",
    )(p, hf, hr, p, p, p, p, p, p, p, w3)


def _mergez_kernel(b0_ref, b1_ref, b2_ref, wb_ref, g0_ref, g1_ref, g2_ref, z_ref):
    acc = None
    for j, (b_ref, g_ref) in enumerate(((b0_ref, g0_ref), (b1_ref, g1_ref), (b2_ref, g2_ref))):
        t = jax.nn.sigmoid(g_ref[...]) * jnp.dot(b_ref[...], wb_ref[j].astype(BF16),
                                                  preferred_element_type=F32)
        acc = t if acc is None else acc + t
    z_ref[...] = acc.astype(z_ref.dtype)


def _mergez_call(y_att, y_lru, y_sc, wb_all, p, layer, *, tm):
    t = p.shape[0]
    tn = 256
    br = pl.BlockSpec((tm, BRANCH_W), lambda i, j: (i, 0))
    gate = lambda k: pl.BlockSpec((tm, tn), lambda i, j: (i, (OFF_GATE + k * D_MODEL) // tn + j))
    return pl.pallas_call(
        _mergez_kernel,
        out_shape=jax.ShapeDtypeStruct((t, D_MODEL), BF16),
        grid=(t // tm, D_MODEL // tn),
        in_specs=[br, br, br,
                  pl.BlockSpec((None, N_BRANCH, BRANCH_W, tn), lambda i, j: (layer, 0, 0, j)),
                  gate(0), gate(1), gate(2)],
        out_specs=pl.BlockSpec((tm, tn), lambda i, j: (i, j)),
        compiler_params=pltpu.CompilerParams(dimension_semantics=("arbitrary", "arbitrary"),
                                             vmem_limit_bytes=BIG_VMEM_LIMIT_BYTES),
        name="branch_merge",
    )(y_att, y_lru, y_sc, wb_all, p, p, p)


def _post_norm_store(r, g_ref, b_ref, nsh_ref, nsc_ref, o_ref, on_ref):
    o = _layer_norm(r) * g_ref[...] + b_ref[...]
    o_ref[...] = o
    if on_ref is None:
        return None
    on = _layer_norm(o) * (1.0 + nsc_ref[...]) + nsh_ref[...]
    on_ref[...] = on.astype(on_ref.dtype)
    return on


def _outnorm_kernel(z_ref, w_ref, x_ref, gate_ref, g_ref, b_ref, *rest, fused, routed):
    nsh_ref = nsc_ref = on_ref = wr_ref = br_ref = route_ref = None
    if routed:
        nsh_ref, nsc_ref, wr_ref, br_ref, o_ref, on_ref, route_ref, w_scr = rest
    elif fused:
        nsh_ref, nsc_ref, o_ref, on_ref, w_scr = rest
    else:
        o_ref, w_scr = rest

    @pl.when(pl.program_id(0) == 0)
    def _():
        w_scr[...] = w_ref[...].astype(BF16)

    y = jnp.dot(z_ref[...], w_scr[...], preferred_element_type=F32)
    r = ALPHA * x_ref[...] + gate_ref[...] * y
    on = _post_norm_store(r, g_ref, b_ref, nsh_ref, nsc_ref, o_ref, on_ref)
    if routed:
        route_ref[...] = _route(on, wr_ref, br_ref)


def _outnorm_call(z, w_all, layer, x, mod, gate_chunk, g, b, next_mod=None, next_dtype=None,
                  router=None):
    t = z.shape[0]
    tm = 256
    row = pl.BlockSpec((tm, D_MODEL), lambda i: (i, 0))
    vec = pl.BlockSpec((1, D_MODEL), lambda i: (0, 0))
    chunk = lambda c: pl.BlockSpec((1, D_MODEL), lambda i: (0, c))
    in_specs = [row,
                pl.BlockSpec((None, D_MODEL, D_MODEL), lambda i: (layer, 0, 0),
                             pipeline_mode=pl.Buffered(1)),
                row, chunk(gate_chunk), vec, vec]
    args = [z, w_all, x, mod, g.reshape(1, D_MODEL), b.reshape(1, D_MODEL)]
    out_shape = jax.ShapeDtypeStruct((t, D_MODEL), F32)
    out_specs = row
    if next_mod is not None:
        nmod, sh, sc = next_mod
        in_specs += [chunk(sh), chunk(sc)]
        args += [nmod, nmod]
        out_shape = (out_shape, jax.ShapeDtypeStruct((t, D_MODEL), next_dtype))
        out_specs = (row, row)
    if router is not None:
        r_args, r_specs = _router_operands(*router)
        in_specs += r_specs
        args += r_args
        out_shape = out_shape + (jax.ShapeDtypeStruct((t, LANES), F32),)
        out_specs = out_specs + (pl.BlockSpec((tm, LANES), lambda i: (i, 0)),)
    return pl.pallas_call(
        functools.partial(_outnorm_kernel, fused=next_mod is not None, routed=router is not None),
        out_shape=out_shape,
        grid=(t // tm,),
        in_specs=in_specs,
        out_specs=out_specs,
        scratch_shapes=[pltpu.VMEM((D_MODEL, D_MODEL), BF16)],
        compiler_params=_cparams("arbitrary"),
        name="out_proj_norm",
    )(*args)


def _route(xf, w_ref, b_ref):
    logits = jnp.dot(xf.astype(BF16), w_ref[...], preferred_element_type=F32) + b_ref[...]
    lane = lax.broadcasted_iota(jnp.int32, logits.shape, 1)
    neg = -jnp.inf
    is_grp = lane < N_GROUPS
    gl = jnp.where(is_grp, logits, neg)
    gmax = jnp.max(gl, axis=-1, keepdims=True)
    g_idx = jnp.min(jnp.where(gl == gmax, lane, LANES), axis=-1, keepdims=True)
    gsum = jnp.sum(jnp.where(is_grp, jnp.exp(gl - gmax), 0.0), axis=-1, keepdims=True)
    g_w = 1.0 / gsum
    lo = N_GROUPS + g_idx * EXPERTS_PER_GROUP
    in_grp = jnp.where(lane >= lo, 1, 0) * jnp.where(lane < lo + EXPERTS_PER_GROUP, 1, 0)
    sel = jnp.where(in_grp > 0, logits, neg)
    v0 = jnp.max(sel, axis=-1, keepdims=True)
    i0 = jnp.min(jnp.where(sel == v0, lane, LANES), axis=-1, keepdims=True)
    sel1 = jnp.where(lane == i0, neg, sel)
    v1 = jnp.max(sel1, axis=-1, keepdims=True)
    i1 = jnp.min(jnp.where(sel1 == v1, lane, LANES), axis=-1, keepdims=True)
    e1 = jnp.exp(v1 - v0)
    den = 1.0 + e1
    w0 = (1.0 / den) * g_w
    w1 = (e1 / den) * g_w
    return jnp.where(lane == 0, (i0 - N_GROUPS).astype(F32),
                     jnp.where(lane == 1, (i1 - N_GROUPS).astype(F32),
                               jnp.where(lane == 2, w0, jnp.where(lane == 3, w1, 0.0))))


def _router_operands(w_grp, b_grp, w_exp, b_exp):
    pad = LANES - N_GROUPS - N_EXPERTS
    wr = jnp.concatenate([w_grp, w_exp, jnp.zeros((D_MODEL, pad), F32)], axis=1).astype(BF16)
    br = jnp.concatenate([b_grp, b_exp, jnp.zeros((pad,), F32)]).reshape(1, LANES)
    specs = [pl.BlockSpec((D_MODEL, LANES), lambda i: (0, 0)), pl.BlockSpec((1, LANES), lambda i: (0, 0))]
    return [wr, br], specs


MOE_UNROLL = 8


def _tile_rows(q0_ref, cnt_ref, order_ref, tile, fn, static_rows):
    base = q0_ref[tile]
    cnt = cnt_ref[tile]
    if static_rows:
        for r in range(MOE_TM):
            fn(r, order_ref[base + r], r < cnt)
        return

    def body(blk, c):
        for k in range(MOE_UNROLL):
            r = blk * MOE_UNROLL + k
            fn(r, order_ref[base + r], r < cnt)
        return c
    lax.fori_loop(0, MOE_TM // MOE_UNROLL, body, 0)


def _is_new_expert(te_ref, i):
    return jnp.logical_or(i == 0, te_ref[i] != te_ref[jnp.maximum(i - 1, 0)])


def _moe_kernel(te_ref, nx_ref, q0_ref, cnt_ref, nt_ref, order_ref, tok_ref,
                x_hbm, wg_hbm, wu_hbm, wd_hbm, y_hbm,
                xbuf, ybuf, wg_stage, wu_stage, wd_stage, wg_scr, wu_scr, wd_scr,
                gsem, ssem, wsem, *, layer, ntok):
    i = pl.program_id(0)
    nt = nt_ref[0]

    weights = ((wg_hbm, wg_stage, wg_scr), (wu_hbm, wu_stage, wu_scr), (wd_hbm, wd_stage, wd_scr))

    def weight_copy(e, m):
        return pltpu.make_async_copy(weights[m][0].at[layer, e], weights[m][1], wsem.at[m])

    def start_gather(tile, s, static_rows):
        def issue(r, tok, valid):
            pltpu.make_async_copy(x_hbm.at[pl.ds(tok, 1), :], xbuf.at[s, pl.ds(r, 1), :],
                                  gsem.at[s]).start()
        _tile_rows(q0_ref, cnt_ref, tok_ref, tile, issue, static_rows)

    def wait_gather(s):
        pltpu.make_async_copy(x_hbm.at[pl.ds(0, MOE_TM), :], xbuf.at[s], gsem.at[s]).wait()

    def wait_scatter(s):
        pltpu.make_async_copy(ybuf.at[s], y_hbm.at[pl.ds(0, MOE_TM), pl.ds(0, D_MODEL)], ssem.at[s]).wait()

    def start_scatter(tile, s):
        def issue(r, pair, valid):
            dst = jnp.where(valid, pair, 2 * ntok + s * MOE_TM + r)
            tok = lax.shift_right_logical(dst, 1)
            col = pl.multiple_of((dst & 1) * D_MODEL, D_MODEL)
            pltpu.make_async_copy(ybuf.at[s, pl.ds(r, 1), :],
                                  y_hbm.at[pl.ds(tok, 1), pl.ds(col, D_MODEL)], ssem.at[s]).start()
        _tile_rows(q0_ref, cnt_ref, order_ref, tile, issue, True)

    @pl.when(i == 0)
    def _():
        start_gather(0, 0, False)
        for m in range(len(weights)):
            weight_copy(te_ref[0], m).start()
        ybuf[0] = jnp.zeros(ybuf.shape[1:], ybuf.dtype)
        for half in range(2):
            spare = pltpu.make_async_copy(
                ybuf.at[0], y_hbm.at[pl.ds(ntok, MOE_TM), pl.ds(half * D_MODEL, D_MODEL)], ssem.at[0])
            spare.start()
            spare.wait()

    @pl.when(_is_new_expert(te_ref, i))
    def _():
        for m, (_, stage, scr) in enumerate(weights):
            weight_copy(te_ref[i], m).wait()
            scr[...] = stage[...].astype(BF16)

            @pl.when(nx_ref[i] >= 0)
            def _(m=m):
                weight_copy(nx_ref[i], m).start()

    for s in range(2):
        @pl.when(jnp.logical_and(i < nt, i % 2 == s))
        def _(s=s):
            wait_gather(s)

            @pl.when(i >= 2)
            def _():
                wait_scatter(s)

            start_gather(jnp.minimum(i + 1, nt - 1), 1 - s, True)
            x = xbuf[s].astype(BF16)
            hg = jnp.dot(x, wg_scr[...], preferred_element_type=F32)
            hu = jnp.dot(x, wu_scr[...], preferred_element_type=F32)
            h = (jax.nn.silu(hg) * hu).astype(BF16)
            ybuf[s] = jnp.dot(h, wd_scr[...], preferred_element_type=F32)
            start_scatter(i, s)

            @pl.when(i == nt - 1)
            def _():
                wait_gather(1 - s)

                @pl.when(i >= 1)
                def _():
                    wait_scatter(1 - s)
                wait_scatter(s)


def _moe_experts(plan, xf, w_gate, w_up, w_down, layer):
    te, nxt, q0, cnt, nt, order, tok = plan
    n = xf.shape[0]
    ntiles = te.shape[0]
    hbm = pl.BlockSpec(memory_space=pl.ANY)
    return pl.pallas_call(
        functools.partial(_moe_kernel, layer=layer, ntok=n),
        out_shape=jax.ShapeDtypeStruct((n + MOE_TM, 2 * D_MODEL), F32),
        grid_spec=pltpu.PrefetchScalarGridSpec(
            num_scalar_prefetch=7,
            grid=(ntiles,),
            in_specs=[hbm, hbm, hbm, hbm],
            out_specs=hbm,
            scratch_shapes=[pltpu.VMEM((2, MOE_TM, D_MODEL), F32),
                            pltpu.VMEM((2, MOE_TM, D_MODEL), F32),
                            pltpu.VMEM((D_MODEL, D_EXPERT), F32),
                            pltpu.VMEM((D_MODEL, D_EXPERT), F32),
                            pltpu.VMEM((D_EXPERT, D_MODEL), F32),
                            pltpu.VMEM((D_MODEL, D_EXPERT), BF16),
                            pltpu.VMEM((D_MODEL, D_EXPERT), BF16),
                            pltpu.VMEM((D_EXPERT, D_MODEL), BF16),
                            pltpu.SemaphoreType.DMA((2,)),
                            pltpu.SemaphoreType.DMA((2,)),
                            pltpu.SemaphoreType.DMA((3,))]),
        compiler_params=pltpu.CompilerParams(dimension_semantics=("arbitrary",),
                                             vmem_limit_bytes=BIG_VMEM_LIMIT_BYTES),
        name="moe_experts",
    )(te, nxt, q0, cnt, nt, order, tok, xf, w_gate, w_up, w_down)


def _lookup(table, idx):
    onehot = idx[:, None] == jnp.arange(table.shape[0], dtype=jnp.int32)[None, :]
    return jnp.sum(jnp.where(onehot, table[None, :], 0), axis=1)


def _moe_plan(eid, tm):
    npair = eid.size
    ntiles = npair // tm + N_EXPERTS
    flat = eid.reshape(-1)
    order = jnp.argsort(flat, stable=True).astype(jnp.int32)
    experts = jnp.arange(N_EXPERTS, dtype=jnp.int32)
    counts = jnp.sum((flat[:, None] == experts[None, :]).astype(jnp.int32), axis=0)
    padded = ((counts + tm - 1) // tm) * tm
    pend = jnp.cumsum(padded)
    pstart = pend - padded
    start = jnp.cumsum(counts) - counts
    nt = pend[-1] // tm
    tiles = jnp.arange(ntiles, dtype=jnp.int32)
    tile_start = tiles * tm
    te = jnp.minimum(jnp.sum((tile_start[:, None] >= pend[None, :]).astype(jnp.int32), axis=1), N_EXPERTS - 1)
    in_use = tiles < nt
    off = tile_start - _lookup(pstart, te)
    q0 = jnp.where(in_use, _lookup(start, te) + off, 0)
    cnt = jnp.where(in_use, jnp.clip(_lookup(counts, te) - off, 0, tm), 0)
    te_last = jnp.sum(jnp.where(tiles == nt - 1, te, 0))
    te = jnp.where(in_use, te, te_last)
    later = (experts[None, :] > experts[:, None]) & (counts[None, :] > 0)
    next_expert = jnp.min(jnp.where(later, experts[None, :], N_EXPERTS), axis=1)
    next_expert = jnp.where(next_expert < N_EXPERTS, next_expert, -1)
    nxt = _lookup(next_expert, te)
    i32 = lambda a: a.astype(jnp.int32)
    order = jnp.concatenate([order, jnp.zeros((tm,), jnp.int32)])
    return i32(te), i32(nxt), i32(q0), i32(cnt), i32(nt).reshape(1), order, order // 2


def _combine_kernel(yp_ref, r_ref, x_ref, gate_ref, g_ref, b_ref, *rest, fused):
    if fused:
        nsh_ref, nsc_ref, o_ref, on_ref = rest
    else:
        (o_ref,), nsh_ref, nsc_ref, on_ref = rest, None, None, None
    w0 = r_ref[:, 2:3]
    w1 = r_ref[:, 3:4]
    m = w0 * yp_ref[:, 0:D_MODEL] + w1 * yp_ref[:, D_MODEL:2 * D_MODEL]
    r = ALPHA * x_ref[...] + gate_ref[...] * m
    _post_norm_store(r, g_ref, b_ref, nsh_ref, nsc_ref, o_ref, on_ref)


def _combine_call(ypair2, route, x, mod, gate_chunk, g, b, row0, next_mod=None, next_dtype=None):
    t = x.shape[0]
    tm = min(512, t)
    row_block0 = row0 // tm
    row = pl.BlockSpec((tm, D_MODEL), lambda i: (i, 0))
    vec = pl.BlockSpec((1, D_MODEL), lambda i: (0, 0))
    chunk = lambda c: pl.BlockSpec((1, D_MODEL), lambda i: (0, c))
    in_specs = [pl.BlockSpec((tm, 2 * D_MODEL), lambda i: (i + row_block0, 0)),
                pl.BlockSpec((tm, LANES), lambda i: (i + row_block0, 0)),
                row, chunk(gate_chunk), vec, vec]
    args = [ypair2, route, x, mod, g.reshape(1, D_MODEL), b.reshape(1, D_MODEL)]
    out_shape = jax.ShapeDtypeStruct((t, D_MODEL), F32)
    out_specs = row
    if next_mod is not None:
        nmod, sh, sc = next_mod
        in_specs += [chunk(sh), chunk(sc)]
        args += [nmod, nmod]
        out_shape = (out_shape, jax.ShapeDtypeStruct((t, D_MODEL), next_dtype))
        out_specs = (row, row)
    return pl.pallas_call(
        functools.partial(_combine_kernel, fused=next_mod is not None),
        out_shape=out_shape,
        grid=(t // tm,),
        in_specs=in_specs,
        out_specs=out_specs,
        compiler_params=_cparams("arbitrary"),
        name="moe_combine_norm",
    )(*args)


def _rope_tables(s):
    t = np.arange(s)
    rows = (t // GRID_W).astype(np.float32)
    cols = (t % GRID_W).astype(np.float32)
    half = HEAD_DIM // 2
    inv = np.float32(ROPE_THETA) ** (-np.arange(0, half, 2, dtype=np.float32) / np.float32(half))
    ar = (rows[:, None] * inv[None, :]).astype(np.float32)
    ac = (cols[:, None] * inv[None, :]).astype(np.float32)
    cos = np.concatenate([np.cos(ar), np.cos(ar), np.cos(ac), np.cos(ac)], axis=-1)
    sin_signed = np.concatenate([-np.sin(ar), np.sin(ar), -np.sin(ac), np.sin(ac)], axis=-1)
    return jnp.asarray(cos, F32), jnp.asarray(sin_signed, F32)


def _moe(xf, route, w_gate, w_up, w_down, layer):
    eid = route[:, 0:2].astype(jnp.int32)
    return _moe_experts(_moe_plan(eid, MOE_TM), xf, w_gate, w_up, w_down, layer)


def kernel(x, c, ctx, c_ctx, w_mod, b_mod, w_in, b_in, attn_sink, lru_conv_w, lru_conv_b,
           lru_w_a, lru_b_a, lru_w_x, lru_b_x, lru_lam, sc_conv_w, w_branch, w_out,
           ln1_g, ln1_b, ln2_g, ln2_b, w_grp, b_grp, w_exp, b_exp, w_gate, w_up, w_down):
    assert x.shape == (1, SEQ, D_MODEL) and ctx.shape == (1, CTX_LEN, D_MODEL)
    xs = x[0]
    xc = ctx[0]
    cos, sin_signed = _rope_tables(SEQ)
    c8 = jnp.concatenate([c, c_ctx[None, :], jnp.zeros((SUBLANES - 2, D_MODEL), F32)], axis=0)
    zero_h0 = jnp.zeros((2, LRU_WIDTH), F32)

    mods = [_mod_call(c8, w_mod, b_mod[l], l) for l in range(DEPTH)]
    u = u_ctx = None
    for l in range(DEPTH):
        last = l == DEPTH - 1
        m_lat = mods[l][0:1]
        m_ctx = mods[l][1:2]

        if l == 0:
            u = _lnmod_call(xs, m_lat, 0, 1, BF16)
            u_ctx = _lnmod_call(xc, m_ctx, 0, 1, BF16)
        p, p_ctx = _proj_call(u, u_ctx, w_in, b_in[l], cos, sin_signed, l, tm=1024)

        y_att = _attn_call(p, p_ctx, attn_sink[l])
        lru_args = (lru_conv_w[l], lru_conv_b[l], lru_w_a[l], lru_b_a[l], lru_w_x[l], lru_b_x[l], lru_lam[l])
        hcf, hcr = _lru_call(p_ctx, zero_h0, *lru_args, tc=CTX_LEN)
        h0 = jnp.concatenate([hcf[CTX_LEN - 1:CTX_LEN], hcr[0:1]], axis=0)
        hf, hr = _lru_call(p, h0, *lru_args, tc=1024)
        y_lru, y_sc = _prep_call(p, hf, hr, sc_conv_w[l], tm=512)
        z = _mergez_call(y_att, y_lru, y_sc, w_branch, p, l, tm=2048)

        router = (w_grp[l], b_grp[l], w_exp[l], b_exp[l])
        if last:
            xs, xf, route = _outnorm_call(z, w_out, l, xs, m_lat, 2, ln1_g[l], ln1_b[l],
                                          next_mod=(m_lat, 3, 4), next_dtype=F32, router=router)
            ypair2 = _moe(xf, route, w_gate, w_up, w_down, l)
            xs = _combine_call(ypair2, route, xs, m_lat, 5, ln2_g[l], ln2_b[l], 0)
        else:
            xs = _outnorm_call(z, w_out, l, xs, m_lat, 2, ln1_g[l], ln1_b[l])
            yc_att = _ctx_attn_call(p_ctx, attn_sink[l])
            yc_lru, yc_sc = _prep_call(p_ctx, hcf, hcr, sc_conv_w[l], tm=CTX_LEN)
            zc = _mergez_call(yc_att, yc_lru, yc_sc, w_branch, p_ctx, l, tm=CTX_LEN)
            xc = _outnorm_call(zc, w_out, l, xc, m_ctx, 2, ln1_g[l], ln1_b[l])
            xf, route = _lnmod_pair_call(xs, m_lat, xc, m_ctx, 3, 4, router)
            ypair2 = _moe(xf, route, w_gate, w_up, w_down, l)
            n_lat = mods[l + 1][0:1]
            n_ctx = mods[l + 1][1:2]
            xs, u = _combine_call(ypair2, route, xs, m_lat, 5, ln2_g[l], ln2_b[l], 0,
                                  next_mod=(n_lat, 0, 1), next_dtype=BF16)
            xc, u_ctx = _combine_call(ypair2, route, xc, m_ctx, 5, ln2_g[l], ln2_b[l], SEQ,
                                      next_mod=(n_ctx, 0, 1), next_dtype=BF16)
    return xs[None]
```

```python
import functools

import jax
import jax.numpy as jnp
import numpy as np
from jax import lax
from jax.experimental import pallas as pl
from jax.experimental.pallas import tpu as pltpu

F32 = jnp.float32
BF16 = jnp.bfloat16

D_MODEL = 2048
SEQ = 8192
CTX_LEN = 256
DEPTH = 2
GRID_W = 64

N_HEADS = 8
N_KV_HEADS = 2
HEAD_DIM = 128
Q_GROUPS = N_HEADS // N_KV_HEADS
WINDOW = 128
BLOCK = 128
ROPE_THETA = 10000.0
Q_SCALE = HEAD_DIM ** -0.5

LRU_WIDTH = 1024
LRU_BLOCKS = 8
LRU_BLOCK_DIM = 128
LRU_C = 8.0
SC_WIDTH = 1024
N_BRANCH = 3
BRANCH_W = 1024

Q_W = N_HEADS * HEAD_DIM
KV_W = N_KV_HEADS * HEAD_DIM
OFF_K = Q_W
OFF_V = OFF_K + KV_W
OFF_LX = OFF_V + KV_W
OFF_LG = OFF_LX + LRU_WIDTH
OFF_SB = OFF_LG + LRU_WIDTH
OFF_SC = OFF_SB + SC_WIDTH
OFF_SX = OFF_SC + SC_WIDTH
OFF_GATE = OFF_SX + SC_WIDTH
PROJ_W = OFF_GATE + N_BRANCH * D_MODEL

N_GROUPS = 4
EXPERTS_PER_GROUP = 8
N_EXPERTS = N_GROUPS * EXPERTS_PER_GROUP
D_EXPERT = 1024

ALPHA = (2 * DEPTH) ** 0.25
LN_EPS = 1e-5

VMEM_LIMIT_BYTES = 52 * 1024 * 1024
BIG_VMEM_LIMIT_BYTES = 58 * 1024 * 1024
LANES = 128
SUBLANES = 8

HALF_W = 512
PROJ_TN = 1280
MOE_TM = 256


def _cparams(*sem):
    return pltpu.CompilerParams(dimension_semantics=sem, vmem_limit_bytes=VMEM_LIMIT_BYTES)


def _layer_norm(x):
    mu = jnp.mean(x, axis=-1, keepdims=True)
    xc = x - mu
    var = jnp.mean(xc * xc, axis=-1, keepdims=True)
    return xc * lax.rsqrt(var + LN_EPS)


def _mod_kernel(c_ref, w_ref, b_ref, o_ref):
    a = jax.nn.silu(c_ref[...]).astype(BF16)
    o_ref[...] = jnp.dot(a, w_ref[...].astype(BF16), preferred_element_type=F32) + b_ref[...]


def _mod_call(c8, w_all, b, layer):
    n = w_all.shape[2]
    tn = 1024
    return pl.pallas_call(
        _mod_kernel,
        out_shape=jax.ShapeDtypeStruct((SUBLANES, n), F32),
        grid=(n // tn,),
        in_specs=[pl.BlockSpec((SUBLANES, D_MODEL), lambda j: (0, 0)),
                  pl.BlockSpec((None, D_MODEL, tn), lambda j: (layer, 0, j)),
                  pl.BlockSpec((1, tn), lambda j: (0, j))],
        out_specs=pl.BlockSpec((SUBLANES, tn), lambda j: (0, j)),
        compiler_params=_cparams("arbitrary"),
        name="mod_vectors",
    )(c8, w_all, b.reshape(1, n))


def _lnmod_kernel(x_ref, sh_ref, sc_ref, o_ref):
    y = _layer_norm(x_ref[...])
    o_ref[...] = (y * (1.0 + sc_ref[...]) + sh_ref[...]).astype(o_ref.dtype)


def _lnmod_call(x, mod, shift_chunk, scale_chunk, out_dtype):
    t = x.shape[0]
    tm = 256
    return pl.pallas_call(
        _lnmod_kernel,
        out_shape=jax.ShapeDtypeStruct((t, D_MODEL), out_dtype),
        grid=(t // tm,),
        in_specs=[pl.BlockSpec((tm, D_MODEL), lambda i: (i, 0)),
                  pl.BlockSpec((1, D_MODEL), lambda i: (0, shift_chunk)),
                  pl.BlockSpec((1, D_MODEL), lambda i: (0, scale_chunk))],
        out_specs=pl.BlockSpec((tm, D_MODEL), lambda i: (i, 0)),
        compiler_params=_cparams("arbitrary"),
        name="ln_modulate",
    )(x, mod, mod)


def _lnmod_pair_kernel(xa_ref, xb_ref, sha_ref, sca_ref, shb_ref, scb_ref, wr_ref, br_ref,
                       o_ref, route_ref, *, na):
    i = pl.program_id(0)

    def one(x_ref, sh_ref, sc_ref):
        y = _layer_norm(x_ref[...]) * (1.0 + sc_ref[...]) + sh_ref[...]
        o_ref[...] = y
        route_ref[...] = _route(y, wr_ref, br_ref)

    @pl.when(i < na)
    def _():
        one(xa_ref, sha_ref, sca_ref)

    @pl.when(i >= na)
    def _():
        one(xb_ref, shb_ref, scb_ref)


def _lnmod_pair_call(xa, mod_a, xb, mod_b, shift_chunk, scale_chunk, router):
    tm = 256
    na = xa.shape[0] // tm
    nb = xb.shape[0] // tm
    rows = (na + nb) * tm
    vec = lambda chunk: pl.BlockSpec((1, D_MODEL), lambda i: (0, chunk))
    r_args, r_specs = _router_operands(*router)
    return pl.pallas_call(
        functools.partial(_lnmod_pair_kernel, na=na),
        out_shape=(jax.ShapeDtypeStruct((rows, D_MODEL), F32), jax.ShapeDtypeStruct((rows, LANES), F32)),
        grid=(na + nb,),
        in_specs=[pl.BlockSpec((tm, D_MODEL), lambda i: (jnp.minimum(i, na - 1), 0)),
                  pl.BlockSpec((tm, D_MODEL), lambda i: (jnp.maximum(i - na, 0), 0)),
                  vec(shift_chunk), vec(scale_chunk), vec(shift_chunk), vec(scale_chunk)] + r_specs,
        out_specs=(pl.BlockSpec((tm, D_MODEL), lambda i: (i, 0)),
                   pl.BlockSpec((tm, LANES), lambda i: (i, 0))),
        compiler_params=_cparams("arbitrary"),
        name="ln_modulate_route",
    )(xa, xb, mod_a, mod_a, mod_b, mod_b, *r_args)


def _rope_head(xh, cos, sin_signed, first_half):
    partner = jnp.where(first_half, pltpu.roll(xh, LANES - 32, 1), pltpu.roll(xh, 32, 1))
    return xh * cos + partner * sin_signed


def _proj_kernel(u_ref, uc_ref, w_ref, b_ref, cos_ref, sin_ref, o_ref, oc_ref, w_scr):
    j = pl.program_id(0)
    i = pl.program_id(1)

    @pl.when(i == 0)
    def _():
        w_scr[...] = w_ref[...].astype(BF16)
        oc_ref[...] = jnp.dot(uc_ref[...], w_scr[...], preferred_element_type=F32) + b_ref[...]

    def project():
        return jnp.dot(u_ref[...], w_scr[...], preferred_element_type=F32) + b_ref[...]

    @pl.when(j != 0)
    def _():
        o_ref[...] = project()

    @pl.when(j == 0)
    def _():
        acc = project()
        cos = cos_ref[...]
        sin_signed = sin_ref[...]
        lane = lax.broadcasted_iota(jnp.int32, cos.shape, 1)
        first_half = (lane & 63) < 32
        for h in range(N_HEADS + N_KV_HEADS):
            sl = slice(h * HEAD_DIM, (h + 1) * HEAD_DIM)
            r = _rope_head(acc[:, sl], cos, sin_signed, first_half)
            if h < N_HEADS:
                r = r * Q_SCALE
            o_ref[:, sl] = r


def _proj_call(u, u_ctx, w_all, b, cos, sin_signed, layer, *, tm):
    t = u.shape[0]
    lc = u_ctx.shape[0]
    n = w_all.shape[2]
    return pl.pallas_call(
        _proj_kernel,
        out_shape=(jax.ShapeDtypeStruct((t, n), F32), jax.ShapeDtypeStruct((lc, n), F32)),
        grid=(n // PROJ_TN, t // tm),
        in_specs=[pl.BlockSpec((tm, D_MODEL), lambda j, i: (i, 0)),
                  pl.BlockSpec((lc, D_MODEL), lambda j, i: (0, 0)),
                  pl.BlockSpec((None, D_MODEL, PROJ_TN), lambda j, i: (layer, 0, j)),
                  pl.BlockSpec((1, PROJ_TN), lambda j, i: (0, j)),
                  pl.BlockSpec((tm, HEAD_DIM), lambda j, i: (i, 0)),
                  pl.BlockSpec((tm, HEAD_DIM), lambda j, i: (i, 0))],
        out_specs=(pl.BlockSpec((tm, PROJ_TN), lambda j, i: (i, j)),
                   pl.BlockSpec((lc, PROJ_TN), lambda j, i: (0, j))),
        scratch_shapes=[pltpu.VMEM((D_MODEL, PROJ_TN), BF16)],
        compiler_params=pltpu.CompilerParams(dimension_semantics=("arbitrary", "arbitrary"),
                                             vmem_limit_bytes=BIG_VMEM_LIMIT_BYTES),
        name="in_proj_rope",
    )(u, u_ctx, w_all, b.reshape(1, n), cos, sin_signed)


def _softmax_pv(q, k_all, v_all, bias, sink_col):
    s = lax.dot_general(q, k_all, (((1,), (1,)), ((), ())), preferred_element_type=F32)
    if bias is not None:
        s = s + bias
    m = jnp.maximum(jnp.max(s, axis=-1, keepdims=True), sink_col)
    e = jnp.exp(s - m)
    den = jnp.sum(e, axis=-1, keepdims=True) + jnp.exp(sink_col - m)
    p = (e / den).astype(BF16)
    return jnp.dot(p, v_all, preferred_element_type=F32)


def _sink_column(sink_ref, kv, rows):
    row = lax.broadcasted_iota(jnp.int32, (Q_GROUPS * rows, 1), 0)
    col = jnp.full((Q_GROUPS * rows, 1), sink_ref[kv * Q_GROUPS + Q_GROUPS - 1], F32)
    for g in range(Q_GROUPS - 2, -1, -1):
        col = jnp.where(row < (g + 1) * rows, sink_ref[kv * Q_GROUPS + g], col)
    return col


ATTN_QB = 4


def _attn_kernel(sink_ref, q_ref, kp_ref, kc_ref, kn_ref, vp_ref, vc_ref, vn_ref,
                 kx_ref, vx_ref, o_ref, *, seq):
    n = pl.program_id(0)
    n_loc = 3 * BLOCK
    n_keys = n_loc + CTX_LEN
    qi = lax.broadcasted_iota(jnp.int32, (BLOCK, n_keys), 0)
    kj = lax.broadcasted_iota(jnp.int32, (BLOCK, n_keys), 1)
    rel = kj - BLOCK - qi
    sinks = [_sink_column(sink_ref, kv, BLOCK) for kv in range(N_KV_HEADS)]
    for sub in range(ATTN_QB):
        blk = n * ATTN_QB + sub
        kpos = blk * BLOCK - BLOCK + kj
        bad = jnp.where(kj < n_loc,
                        jnp.where(jnp.abs(rel) > WINDOW, 1, 0) + jnp.where(kpos < 0, 1, 0)
                        + jnp.where(kpos >= seq, 1, 0), 0)
        bias1 = jnp.where(bad > 0, -jnp.inf, 0.0).astype(F32)
        bias = jnp.concatenate([bias1] * Q_GROUPS, axis=0)
        rows = slice(sub * BLOCK, (sub + 1) * BLOCK)

        def band(prev_ref, cur_ref, next_ref, ctx_ref, ks):
            own = [cur_ref[j * BLOCK:(j + 1) * BLOCK, ks] for j in range(ATTN_QB)]
            blocks = [prev_ref[:, ks]] + own + [next_ref[:, ks]]
            return jnp.concatenate(blocks[sub:sub + 3] + [ctx_ref[:, ks]], axis=0).astype(BF16)

        for kv in range(N_KV_HEADS):
            ks = slice(kv * HEAD_DIM, (kv + 1) * HEAD_DIM)
            k_all = band(kp_ref, kc_ref, kn_ref, kx_ref, ks)
            v_all = band(vp_ref, vc_ref, vn_ref, vx_ref, ks)
            q = jnp.concatenate(
                [q_ref[rows, (kv * Q_GROUPS + g) * HEAD_DIM:(kv * Q_GROUPS + g + 1) * HEAD_DIM]
                 for g in range(Q_GROUPS)], axis=0).astype(BF16)
            o = _softmax_pv(q, k_all, v_all, bias, sinks[kv])
            for g in range(Q_GROUPS):
                h = kv * Q_GROUPS + g
                o_ref[rows, h * HEAD_DIM:(h + 1) * HEAD_DIM] = (
                    o[g * BLOCK:(g + 1) * BLOCK].astype(o_ref.dtype))


def _attn_call(p, p_ctx, sink):
    s = p.shape[0]
    nb = s // BLOCK
    step_rows = ATTN_QB * BLOCK
    kcol = OFF_K // KV_W
    vcol = OFF_V // KV_W
    prev = lambda c: pl.BlockSpec((BLOCK, KV_W), lambda n: (jnp.maximum(n * ATTN_QB - 1, 0), c))
    nxt = lambda c: pl.BlockSpec((BLOCK, KV_W), lambda n: (jnp.minimum((n + 1) * ATTN_QB, nb - 1), c))
    cur = lambda c: pl.BlockSpec((step_rows, KV_W), lambda n: (n, c))
    return pl.pallas_call(
        functools.partial(_attn_kernel, seq=s),
        out_shape=jax.ShapeDtypeStruct((s, Q_W), BF16),
        grid=(s // step_rows,),
        in_specs=[pl.BlockSpec(memory_space=pltpu.SMEM),
                  pl.BlockSpec((step_rows, Q_W), lambda n: (n, 0)),
                  prev(kcol), cur(kcol), nxt(kcol),
                  prev(vcol), cur(vcol), nxt(vcol),
                  pl.BlockSpec((CTX_LEN, KV_W), lambda n: (0, kcol)),
                  pl.BlockSpec((CTX_LEN, KV_W), lambda n: (0, vcol))],
        out_specs=pl.BlockSpec((step_rows, Q_W), lambda n: (n, 0)),
        compiler_params=_cparams("arbitrary"),
        name="window_attention",
    )(sink, p, p, p, p, p, p, p, p_ctx, p_ctx)


def _ctx_attn_kernel(sink_ref, q_ref, k_ref, v_ref, o_ref):
    rows = q_ref.shape[0]
    for kv in range(N_KV_HEADS):
        ks = slice(kv * HEAD_DIM, (kv + 1) * HEAD_DIM)
        q = jnp.concatenate(
            [q_ref[:, (kv * Q_GROUPS + g) * HEAD_DIM:(kv * Q_GROUPS + g + 1) * HEAD_DIM] * Q_SCALE
             for g in range(Q_GROUPS)], axis=0).astype(BF16)
        o = _softmax_pv(q, k_ref[:, ks].astype(BF16), v_ref[:, ks].astype(BF16), None,
                        _sink_column(sink_ref, kv, rows))
        for g in range(Q_GROUPS):
            h = kv * Q_GROUPS + g
            o_ref[:, h * HEAD_DIM:(h + 1) * HEAD_DIM] = o[g * rows:(g + 1) * rows].astype(o_ref.dtype)


def _ctx_attn_call(p_ctx, sink):
    lc = p_ctx.shape[0]
    return pl.pallas_call(
        _ctx_attn_kernel,
        out_shape=jax.ShapeDtypeStruct((lc, Q_W), BF16),
        grid=(1,),
        in_specs=[pl.BlockSpec(memory_space=pltpu.SMEM),
                  pl.BlockSpec((lc, Q_W), lambda n: (0, 0)),
                  pl.BlockSpec((lc, KV_W), lambda n: (0, OFF_K // KV_W)),
                  pl.BlockSpec((lc, KV_W), lambda n: (0, OFF_V // KV_W))],
        out_specs=pl.BlockSpec((lc, Q_W), lambda n: (0, 0)),
        compiler_params=_cparams("arbitrary"),
        name="context_attention",
    )(sink, p_ctx, p_ctx, p_ctx)


def _lru_conv(ext_scr, cur_ref, prev_ref, next_ref, is_first, is_last, cw_ref, cb_ref, tc):
    ext_scr[0:SUBLANES, :] = jnp.where(is_first, 0.0, prev_ref[...])
    ext_scr[SUBLANES:SUBLANES + tc, :] = cur_ref[...]
    ext_scr[SUBLANES + tc:, :] = jnp.where(is_last, 0.0, next_ref[...])
    out = cb_ref[...] + cw_ref[0:1, :] * ext_scr[SUBLANES - 1:SUBLANES - 1 + tc, :]
    for j in range(1, 4):
        out = out + cw_ref[j:j + 1, :] * ext_scr[SUBLANES - 1 + j:SUBLANES - 1 + j + tc, :]
    return out


def _lru_gates(xb, d, wa_ref, ba_ref, wx_ref, bx_ref, lam_ref, a_scr, u_scr):
    xb_bf = xb.astype(BF16)
    for b in range(HALF_W // LRU_BLOCK_DIM):
        sl = slice(b * LRU_BLOCK_DIM, (b + 1) * LRU_BLOCK_DIM)
        xs = xb_bf[:, sl]
        r = jax.nn.sigmoid(jnp.dot(xs, wa_ref[d, b].astype(BF16), preferred_element_type=F32)
                           + ba_ref[d:d + 1, sl])
        ig = jax.nn.sigmoid(jnp.dot(xs, wx_ref[d, b].astype(BF16), preferred_element_type=F32)
                            + bx_ref[d:d + 1, sl])
        log_a = -LRU_C * r * jax.nn.softplus(-lam_ref[d:d + 1, sl])
        a_scr[d, :, sl] = jnp.exp(log_a)
        u_scr[d, :, sl] = jnp.sqrt(1.0 - jnp.exp(2.0 * log_a)) * ig * xb[:, sl]


def _lru_kernel(fc_ref, fp_ref, fn_ref, rc_ref, rp_ref, rn_ref,
                cw_ref, cb_ref, wa_ref, ba_ref, wx_ref, bx_ref, lam_ref, h0_ref,
                hf_ref, hr_ref, ext_scr, a_scr, u_scr, cf_scr, cr_scr, *, tc, nchunks):
    i = pl.program_id(1)
    w = HALF_W

    @pl.when(i == 0)
    def _():
        cf_scr[...] = jnp.broadcast_to(h0_ref[0:1, :], (SUBLANES, w))
        cr_scr[...] = jnp.broadcast_to(h0_ref[1:2, :], (SUBLANES, w))

    xb_f = _lru_conv(ext_scr, fc_ref, fp_ref, fn_ref, i == 0, i == nchunks - 1, cw_ref, cb_ref, tc)
    _lru_gates(xb_f, 0, wa_ref, ba_ref, wx_ref, bx_ref, lam_ref, a_scr, u_scr)
    xb_r = _lru_conv(ext_scr, rc_ref, rp_ref, rn_ref, i == nchunks - 1, i == 0, cw_ref, cb_ref, tc)
    _lru_gates(xb_r, 1, wa_ref, ba_ref, wx_ref, bx_ref, lam_ref, a_scr, u_scr)

    row = lax.broadcasted_iota(jnp.int32, (SUBLANES, w), 0)
    ntile = tc // SUBLANES

    def body(j, carry):
        cf, cr = carry
        off = pl.multiple_of(j * SUBLANES, SUBLANES)
        a = a_scr[0, pl.ds(off, SUBLANES), :]
        u = u_scr[0, pl.ds(off, SUBLANES), :]
        for s in (1, 2, 4):
            m = row >= s
            u = jnp.where(m, u + a * pltpu.roll(u, s, 0), u)
            a = jnp.where(m, a * pltpu.roll(a, s, 0), a)
        h = u + a * cf
        hf_ref[pl.ds(off, SUBLANES), :] = h
        cf = jnp.broadcast_to(h[SUBLANES - 1:SUBLANES, :], (SUBLANES, w))
        offr = pl.multiple_of((ntile - 1 - j) * SUBLANES, SUBLANES)
        a = a_scr[1, pl.ds(offr, SUBLANES), :]
        u = u_scr[1, pl.ds(offr, SUBLANES), :]
        for s in (1, 2, 4):
            m = row < SUBLANES - s
            u = jnp.where(m, u + a * pltpu.roll(u, SUBLANES - s, 0), u)
            a = jnp.where(m, a * pltpu.roll(a, SUBLANES - s, 0), a)
        h = u + a * cr
        hr_ref[pl.ds(offr, SUBLANES), :] = h
        cr = jnp.broadcast_to(h[0:1, :], (SUBLANES, w))
        return cf, cr

    cf, cr = lax.fori_loop(0, ntile, body, (cf_scr[...], cr_scr[...]))
    cf_scr[...] = cf
    cr_scr[...] = cr


def _lru_call(p, h0, cw, cb, wa, ba, wx, bx, lam, *, tc):
    t = p.shape[0]
    nchunks = t // tc
    w = HALF_W
    c0 = OFF_LX // w
    r8 = tc // SUBLANES
    last8 = t // SUBLANES - 1
    fwd = lambda i: i
    rev = lambda i: nchunks - 1 - i
    cur_spec = lambda f: pl.BlockSpec((tc, w), lambda h, i: (f(i), c0 + h))
    prev_spec = lambda f: pl.BlockSpec((SUBLANES, w), lambda h, i: (jnp.maximum(f(i) * r8 - 1, 0), c0 + h))
    next_spec = lambda f: pl.BlockSpec((SUBLANES, w), lambda h, i: (jnp.minimum((f(i) + 1) * r8, last8), c0 + h))
    vec_spec = lambda rows: pl.BlockSpec((rows, w), lambda h, i: (0, h))
    gate_w_spec = pl.BlockSpec((2, w // LRU_BLOCK_DIM, LRU_BLOCK_DIM, LRU_BLOCK_DIM),
                               lambda h, i: (0, h, 0, 0))
    kern = functools.partial(_lru_kernel, tc=tc, nchunks=nchunks)
    return pl.pallas_call(
        kern,
        out_shape=(jax.ShapeDtypeStruct((t, LRU_WIDTH), F32), jax.ShapeDtypeStruct((t, LRU_WIDTH), F32)),
        grid=(LRU_WIDTH // w, nchunks),
        in_specs=[cur_spec(fwd), prev_spec(fwd), next_spec(fwd),
                  cur_spec(rev), prev_spec(rev), next_spec(rev),
                  vec_spec(4), vec_spec(1), gate_w_spec, vec_spec(2), gate_w_spec, vec_spec(2),
                  vec_spec(2), vec_spec(2)],
        out_specs=(pl.BlockSpec((tc, w), lambda h, i: (i, h)),
                   pl.BlockSpec((tc, w), lambda h, i: (nchunks - 1 - i, h))),
        scratch_shapes=[pltpu.VMEM((tc + 2 * SUBLANES, w), F32),
                        pltpu.VMEM((2, tc, w), F32), pltpu.VMEM((2, tc, w), F32),
                        pltpu.VMEM((SUBLANES, w), F32), pltpu.VMEM((SUBLANES, w), F32)],
        compiler_params=_cparams("arbitrary", "arbitrary"),
        name="rg_lru",
    )(p, p, p, p, p, p, cw, cb.reshape(1, LRU_WIDTH), wa, ba.reshape(2, LRU_WIDTH),
      wx, bx.reshape(2, LRU_WIDTH), lam, h0)


def _prep_kernel(lg_ref, hf_ref, hr_ref, sb_ref, scc_ref, scp_ref, scn_ref,
                 sxc_ref, sxp_ref, sxn_ref, w3_ref, ylru_ref, ysc_ref, ext_scr, *, tm, ntiles):
    i = pl.program_id(0)
    ylru_ref[...] = ((hf_ref[...] + hr_ref[...]) * jax.nn.gelu(lg_ref[...])).astype(ylru_ref.dtype)
    ext_scr[0:SUBLANES, :] = jnp.where(i == 0, 0.0, scp_ref[...] * sxp_ref[...])
    ext_scr[SUBLANES:SUBLANES + tm, :] = scc_ref[...] * sxc_ref[...]
    ext_scr[SUBLANES + tm:, :] = jnp.where(i == ntiles - 1, 0.0, scn_ref[...] * sxn_ref[...])
    conv = w3_ref[0:1, :] * ext_scr[SUBLANES - 1:SUBLANES - 1 + tm, :]
    for j in range(1, 3):
        conv = conv + w3_ref[j:j + 1, :] * ext_scr[SUBLANES - 1 + j:SUBLANES - 1 + j + tm, :]
    ysc_ref[...] = (sb_ref[...] * conv).astype(ysc_ref.dtype)


def _prep_call(p, hf, hr, w3, *, tm):
    t = p.shape[0]
    ntiles = t // tm
    w = HALF_W
    r8 = tm // SUBLANES
    last8 = t // SUBLANES - 1
    col = lambda off: pl.BlockSpec((tm, w), lambda i, h: (i, off // w + h))
    colp = lambda off: pl.BlockSpec((SUBLANES, w), lambda i, h: (jnp.maximum(i * r8 - 1, 0), off // w + h))
    coln = lambda off: pl.BlockSpec((SUBLANES, w), lambda i, h: (jnp.minimum((i + 1) * r8, last8), off // w + h))
    own = pl.BlockSpec((tm, w), lambda i, h: (i, h))
    kern = functools.partial(_prep_kernel, tm=tm, ntiles=ntiles)
    return pl.pallas_call(
        kern,
        out_shape=(jax.ShapeDtypeStruct((t, LRU_WIDTH), BF16), jax.ShapeDtypeStruct((t, SC_WIDTH), BF16)),
        grid=(ntiles, LRU_WIDTH // w),
        in_specs=[col(OFF_LG), own, own, col(OFF_SB),
                  col(OFF_SC), colp(OFF_SC), coln(OFF_SC),
                  col(OFF_SX), colp(OFF_SX), coln(OFF_SX),
                  pl.BlockSpec((3, w), lambda i, h: (0, h))],
        out_specs=(own, own),
        scratch_shapes=[pltpu.VMEM((tm + 2 * SUBLANES, w), F32)],
        compiler_params=_cparams("arbitrary", "arbitrary"),
        name="branch_prep",
    )(p, hf, hr, p, p, p, p, p, p, p, w3)


def _mergez_kernel(b0_ref, b1_ref, b2_ref, wb_ref, g0_ref, g1_ref, g2_ref, z_ref):
    acc = None
    for j, (b_ref, g_ref) in enumerate(((b0_ref, g0_ref), (b1_ref, g1_ref), (b2_ref, g2_ref))):
        t = jax.nn.sigmoid(g_ref[...]) * jnp.dot(b_ref[...], wb_ref[j].astype(BF16),
                                                  preferred_element_type=F32)
        acc = t if acc is None else acc + t
    z_ref[...] = acc.astype(z_ref.dtype)


def _mergez_call(y_att, y_lru, y_sc, wb_all, p, layer, *, tm):
    t = p.shape[0]
    tn = 256
    br = pl.BlockSpec((tm, BRANCH_W), lambda i, j: (i, 0))
    gate = lambda k: pl.BlockSpec((tm, tn), lambda i, j: (i, (OFF_GATE + k * D_MODEL) // tn + j))
    return pl.pallas_call(
        _mergez_kernel,
        out_shape=jax.ShapeDtypeStruct((t, D_MODEL), BF16),
        grid=(t // tm, D_MODEL // tn),
        in_specs=[br, br, br,
                  pl.BlockSpec((None, N_BRANCH, BRANCH_W, tn), lambda i, j: (layer, 0, 0, j)),
                  gate(0), gate(1), gate(2)],
        out_specs=pl.BlockSpec((tm, tn), lambda i, j: (i, j)),
        compiler_params=pltpu.CompilerParams(dimension_semantics=("arbitrary", "arbitrary"),
                                             vmem_limit_bytes=BIG_VMEM_LIMIT_BYTES),
        name="branch_merge",
    )(y_att, y_lru, y_sc, wb_all, p, p, p)


def _post_norm_store(r, g_ref, b_ref, nsh_ref, nsc_ref, o_ref, on_ref):
    o = _layer_norm(r) * g_ref[...] + b_ref[...]
    o_ref[...] = o
    if on_ref is None:
        return None
    on = _layer_norm(o) * (1.0 + nsc_ref[...]) + nsh_ref[...]
    on_ref[...] = on.astype(on_ref.dtype)
    return on


def _outnorm_kernel(z_ref, w_ref, x_ref, gate_ref, g_ref, b_ref, *rest, fused, routed):
    nsh_ref = nsc_ref = on_ref = wr_ref = br_ref = route_ref = None
    if routed:
        nsh_ref, nsc_ref, wr_ref, br_ref, o_ref, on_ref, route_ref, w_scr = rest
    elif fused:
        nsh_ref, nsc_ref, o_ref, on_ref, w_scr = rest
    else:
        o_ref, w_scr = rest

    @pl.when(pl.program_id(0) == 0)
    def _():
        w_scr[...] = w_ref[...].astype(BF16)

    y = jnp.dot(z_ref[...], w_scr[...], preferred_element_type=F32)
    r = ALPHA * x_ref[...] + gate_ref[...] * y
    on = _post_norm_store(r, g_ref, b_ref, nsh_ref, nsc_ref, o_ref, on_ref)
    if routed:
        route_ref[...] = _route(on, wr_ref, br_ref)


def _outnorm_call(z, w_all, layer, x, mod, gate_chunk, g, b, next_mod=None, next_dtype=None,
                  router=None):
    t = z.shape[0]
    tm = 256
    row = pl.BlockSpec((tm, D_MODEL), lambda i: (i, 0))
    vec = pl.BlockSpec((1, D_MODEL), lambda i: (0, 0))
    chunk = lambda c: pl.BlockSpec((1, D_MODEL), lambda i: (0, c))
    in_specs = [row,
                pl.BlockSpec((None, D_MODEL, D_MODEL), lambda i: (layer, 0, 0),
                             pipeline_mode=pl.Buffered(1)),
                row, chunk(gate_chunk), vec, vec]
    args = [z, w_all, x, mod, g.reshape(1, D_MODEL), b.reshape(1, D_MODEL)]
    out_shape = jax.ShapeDtypeStruct((t, D_MODEL), F32)
    out_specs = row
    if next_mod is not None:
        nmod, sh, sc = next_mod
        in_specs += [chunk(sh), chunk(sc)]
        args += [nmod, nmod]
        out_shape = (out_shape, jax.ShapeDtypeStruct((t, D_MODEL), next_dtype))
        out_specs = (row, row)
    if router is not None:
        r_args, r_specs = _router_operands(*router)
        in_specs += r_specs
        args += r_args
        out_shape = out_shape + (jax.ShapeDtypeStruct((t, LANES), F32),)
        out_specs = out_specs + (pl.BlockSpec((tm, LANES), lambda i: (i, 0)),)
    return pl.pallas_call(
        functools.partial(_outnorm_kernel, fused=next_mod is not None, routed=router is not None),
        out_shape=out_shape,
        grid=(t // tm,),
        in_specs=in_specs,
        out_specs=out_specs,
        scratch_shapes=[pltpu.VMEM((D_MODEL, D_MODEL), BF16)],
        compiler_params=_cparams("arbitrary"),
        name="out_proj_norm",
    )(*args)


def _route(xf, w_ref, b_ref):
    logits = jnp.dot(xf.astype(BF16), w_ref[...], preferred_element_type=F32) + b_ref[...]
    lane = lax.broadcasted_iota(jnp.int32, logits.shape, 1)
    neg = -jnp.inf
    is_grp = lane < N_GROUPS
    gl = jnp.where(is_grp, logits, neg)
    gmax = jnp.max(gl, axis=-1, keepdims=True)
    g_idx = jnp.min(jnp.where(gl == gmax, lane, LANES), axis=-1, keepdims=True)
    gsum = jnp.sum(jnp.where(is_grp, jnp.exp(gl - gmax), 0.0), axis=-1, keepdims=True)
    g_w = 1.0 / gsum
    lo = N_GROUPS + g_idx * EXPERTS_PER_GROUP
    in_grp = jnp.where(lane >= lo, 1, 0) * jnp.where(lane < lo + EXPERTS_PER_GROUP, 1, 0)
    sel = jnp.where(in_grp > 0, logits, neg)
    v0 = jnp.max(sel, axis=-1, keepdims=True)
    i0 = jnp.min(jnp.where(sel == v0, lane, LANES), axis=-1, keepdims=True)
    sel1 = jnp.where(lane == i0, neg, sel)
    v1 = jnp.max(sel1, axis=-1, keepdims=True)
    i1 = jnp.min(jnp.where(sel1 == v1, lane, LANES), axis=-1, keepdims=True)
    e1 = jnp.exp(v1 - v0)
    den = 1.0 + e1
    w0 = (1.0 / den) * g_w
    w1 = (e1 / den) * g_w
    return jnp.where(lane == 0, (i0 - N_GROUPS).astype(F32),
                     jnp.where(lane == 1, (i1 - N_GROUPS).astype(F32),
                               jnp.where(lane == 2, w0, jnp.where(lane == 3, w1, 0.0))))


def _router_operands(w_grp, b_grp, w_exp, b_exp):
    pad = LANES - N_GROUPS - N_EXPERTS
    wr = jnp.concatenate([w_grp, w_exp, jnp.zeros((D_MODEL, pad), F32)], axis=1).astype(BF16)
    br = jnp.concatenate([b_grp, b_exp, jnp.zeros((pad,), F32)]).reshape(1, LANES)
    specs = [pl.BlockSpec((D_MODEL, LANES), lambda i: (0, 0)), pl.BlockSpec((1, LANES), lambda i: (0, 0))]
    return [wr, br], specs


MOE_UNROLL = 8
WEIGHT_DMA_PRIORITY = 1


def _tile_rows(q0_ref, cnt_ref, order_ref, tile, fn, static_rows):
    base = q0_ref[tile]
    cnt = cnt_ref[tile]
    if static_rows:
        for r in range(MOE_TM):
            fn(r, order_ref[base + r], r < cnt)
        return

    def body(blk, c):
        for k in range(MOE_UNROLL):
            r = blk * MOE_UNROLL + k
            fn(r, order_ref[base + r], r < cnt)
        return c
    lax.fori_loop(0, MOE_TM // MOE_UNROLL, body, 0)


def _is_new_expert(te_ref, i):
    return jnp.logical_or(i == 0, te_ref[i] != te_ref[jnp.maximum(i - 1, 0)])


def _moe_kernel(te_ref, nx_ref, q0_ref, cnt_ref, nt_ref, order_ref, tok_ref,
                x_hbm, wg_hbm, wu_hbm, wd_hbm, y_hbm,
                xbuf, ybuf, wg_stage, wu_stage, wd_stage, wg_scr, wu_scr, wd_scr,
                gsem, ssem, wsem, *, layer, ntok):
    i = pl.program_id(0)
    nt = nt_ref[0]

    weights = ((wg_hbm, wg_stage, wg_scr), (wu_hbm, wu_stage, wu_scr), (wd_hbm, wd_stage, wd_scr))

    def weight_copy(e, m):
        return pltpu.make_async_copy(weights[m][0].at[layer, e], weights[m][1], wsem.at[m])

    def start_gather(tile, s, static_rows):
        def issue(r, tok, valid):
            pltpu.make_async_copy(x_hbm.at[pl.ds(tok, 1), :], xbuf.at[s, pl.ds(r, 1), :],
                                  gsem.at[s]).start()
        _tile_rows(q0_ref, cnt_ref, tok_ref, tile, issue, static_rows)

    def wait_gather(s):
        pltpu.make_async_copy(x_hbm.at[pl.ds(0, MOE_TM), :], xbuf.at[s], gsem.at[s]).wait()

    def wait_scatter(s):
        pltpu.make_async_copy(ybuf.at[s], y_hbm.at[pl.ds(0, MOE_TM), pl.ds(0, D_MODEL)], ssem.at[s]).wait()

    def start_scatter(tile, s):
        def issue(r, pair, valid):
            dst = jnp.where(valid, pair, 2 * ntok + s * MOE_TM + r)
            tok = lax.shift_right_logical(dst, 1)
            col = pl.multiple_of((dst & 1) * D_MODEL, D_MODEL)
            pltpu.make_async_copy(ybuf.at[s, pl.ds(r, 1), :],
                                  y_hbm.at[pl.ds(tok, 1), pl.ds(col, D_MODEL)],
                                  ssem.at[s]).start(priority=r % 2)
        _tile_rows(q0_ref, cnt_ref, order_ref, tile, issue, True)

    @pl.when(i == 0)
    def _():
        start_gather(0, 0, False)
        for m in range(len(weights)):
            weight_copy(te_ref[0], m).start(priority=WEIGHT_DMA_PRIORITY)
        ybuf[0] = jnp.zeros(ybuf.shape[1:], ybuf.dtype)
        for half in range(2):
            spare = pltpu.make_async_copy(
                ybuf.at[0], y_hbm.at[pl.ds(ntok, MOE_TM), pl.ds(half * D_MODEL, D_MODEL)], ssem.at[0])
            spare.start()
            spare.wait()

    @pl.when(_is_new_expert(te_ref, i))
    def _():
        for m, (_, stage, scr) in enumerate(weights):
            weight_copy(te_ref[i], m).wait()
            scr[...] = stage[...].astype(BF16)

            @pl.when(nx_ref[i] >= 0)
            def _(m=m):
                weight_copy(nx_ref[i], m).start(priority=WEIGHT_DMA_PRIORITY)

    for s in range(2):
        @pl.when(jnp.logical_and(i < nt, i % 2 == s))
        def _(s=s):
            wait_gather(s)

            @pl.when(i >= 2)
            def _():
                wait_scatter(s)

            start_gather(jnp.minimum(i + 1, nt - 1), 1 - s, True)
            x = xbuf[s].astype(BF16)
            hg = jnp.dot(x, wg_scr[...], preferred_element_type=F32)
            hu = jnp.dot(x, wu_scr[...], preferred_element_type=F32)
            h = (jax.nn.silu(hg) * hu).astype(BF16)
            ybuf[s] = jnp.dot(h, wd_scr[...], preferred_element_type=F32)
            start_scatter(i, s)

            @pl.when(i == nt - 1)
            def _():
                wait_gather(1 - s)

                @pl.when(i >= 1)
                def _():
                    wait_scatter(1 - s)
                wait_scatter(s)


def _moe_experts(plan, xf, w_gate, w_up, w_down, layer):
    te, nxt, q0, cnt, nt, order, tok = plan
    n = xf.shape[0]
    ntiles = te.shape[0]
    hbm = pl.BlockSpec(memory_space=pl.ANY)
    return pl.pallas_call(
        functools.partial(_moe_kernel, layer=layer, ntok=n),
        out_shape=jax.ShapeDtypeStruct((n + MOE_TM, 2 * D_MODEL), F32),
        grid_spec=pltpu.PrefetchScalarGridSpec(
            num_scalar_prefetch=7,
            grid=(ntiles,),
            in_specs=[hbm, hbm, hbm, hbm],
            out_specs=hbm,
            scratch_shapes=[pltpu.VMEM((2, MOE_TM, D_MODEL), F32),
                            pltpu.VMEM((2, MOE_TM, D_MODEL), F32),
                            pltpu.VMEM((D_MODEL, D_EXPERT), F32),
                            pltpu.VMEM((D_MODEL, D_EXPERT), F32),
                            pltpu.VMEM((D_EXPERT, D_MODEL), F32),
                            pltpu.VMEM((D_MODEL, D_EXPERT), BF16),
                            pltpu.VMEM((D_MODEL, D_EXPERT), BF16),
                            pltpu.VMEM((D_EXPERT, D_MODEL), BF16),
                            pltpu.SemaphoreType.DMA((2,)),
                            pltpu.SemaphoreType.DMA((2,)),
                            pltpu.SemaphoreType.DMA((3,))]),
        compiler_params=pltpu.CompilerParams(dimension_semantics=("arbitrary",),
                                             vmem_limit_bytes=BIG_VMEM_LIMIT_BYTES),
        name="moe_experts",
    )(te, nxt, q0, cnt, nt, order, tok, xf, w_gate, w_up, w_down)


def _lookup(table, idx):
    onehot = idx[:, None] == jnp.arange(table.shape[0], dtype=jnp.int32)[None, :]
    return jnp.sum(jnp.where(onehot, table[None, :], 0), axis=1)


def _moe_plan(eid, tm):
    npair = eid.size
    ntiles = npair // tm + N_EXPERTS
    flat = eid.reshape(-1)
    order = jnp.argsort(flat, stable=True).astype(jnp.int32)
    experts = jnp.arange(N_EXPERTS, dtype=jnp.int32)
    counts = jnp.sum((flat[:, None] == experts[None, :]).astype(jnp.int32), axis=0)
    padded = ((counts + tm - 1) // tm) * tm
    pend = jnp.cumsum(padded)
    pstart = pend - padded
    start = jnp.cumsum(counts) - counts
    nt = pend[-1] // tm
    tiles = jnp.arange(ntiles, dtype=jnp.int32)
    tile_start = tiles * tm
    te = jnp.minimum(jnp.sum((tile_start[:, None] >= pend[None, :]).astype(jnp.int32), axis=1), N_EXPERTS - 1)
    in_use = tiles < nt
    off = tile_start - _lookup(pstart, te)
    q0 = jnp.where(in_use, _lookup(start, te) + off, 0)
    cnt = jnp.where(in_use, jnp.clip(_lookup(counts, te) - off, 0, tm), 0)
    te_last = jnp.sum(jnp.where(tiles == nt - 1, te, 0))
    te = jnp.where(in_use, te, te_last)
    later = (experts[None, :] > experts[:, None]) & (counts[None, :] > 0)
    next_expert = jnp.min(jnp.where(later, experts[None, :], N_EXPERTS), axis=1)
    next_expert = jnp.where(next_expert < N_EXPERTS, next_expert, -1)
    nxt = _lookup(next_expert, te)
    i32 = lambda a: a.astype(jnp.int32)
    order = jnp.concatenate([order, jnp.zeros((tm,), jnp.int32)])
    return i32(te), i32(nxt), i32(q0), i32(cnt), i32(nt).reshape(1), order, order // 2


def _combine_kernel(yp_ref, r_ref, x_ref, gate_ref, g_ref, b_ref, *rest, fused):
    if fused:
        nsh_ref, nsc_ref, o_ref, on_ref = rest
    else:
        (o_ref,), nsh_ref, nsc_ref, on_ref = rest, None, None, None
    w0 = r_ref[:, 2:3]
    w1 = r_ref[:, 3:4]
    m = w0 * yp_ref[:, 0:D_MODEL] + w1 * yp_ref[:, D_MODEL:2 * D_MODEL]
    r = ALPHA * x_ref[...] + gate_ref[...] * m
    _post_norm_store(r, g_ref, b_ref, nsh_ref, nsc_ref, o_ref, on_ref)


def _combine_call(ypair2, route, x, mod, gate_chunk, g, b, row0, next_mod=None, next_dtype=None):
    t = x.shape[0]
    tm = min(512, t)
    row_block0 = row0 // tm
    row = pl.BlockSpec((tm, D_MODEL), lambda i: (i, 0))
    vec = pl.BlockSpec((1, D_MODEL), lambda i: (0, 0))
    chunk = lambda c: pl.BlockSpec((1, D_MODEL), lambda i: (0, c))
    in_specs = [pl.BlockSpec((tm, 2 * D_MODEL), lambda i: (i + row_block0, 0)),
                pl.BlockSpec((tm, LANES), lambda i: (i + row_block0, 0)),
                row, chunk(gate_chunk), vec, vec]
    args = [ypair2, route, x, mod, g.reshape(1, D_MODEL), b.reshape(1, D_MODEL)]
    out_shape = jax.ShapeDtypeStruct((t, D_MODEL), F32)
    out_specs = row
    if next_mod is not None:
        nmod, sh, sc = next_mod
        in_specs += [chunk(sh), chunk(sc)]
        args += [nmod, nmod]
        out_shape = (out_shape, jax.ShapeDtypeStruct((t, D_MODEL), next_dtype))
        out_specs = (row, row)
    return pl.pallas_call(
        functools.partial(_combine_kernel, fused=next_mod is not None),
        out_shape=out_shape,
        grid=(t // tm,),
        in_specs=in_specs,
        out_specs=out_specs,
        compiler_params=_cparams("arbitrary"),
        name="moe_combine_norm",
    )(*args)


def _rope_tables(s):
    t = np.arange(s)
    rows = (t // GRID_W).astype(np.float32)
    cols = (t % GRID_W).astype(np.float32)
    half = HEAD_DIM // 2
    inv = np.float32(ROPE_THETA) ** (-np.arange(0, half, 2, dtype=np.float32) / np.float32(half))
    ar = (rows[:, None] * inv[None, :]).astype(np.float32)
    ac = (cols[:, None] * inv[None, :]).astype(np.float32)
    cos = np.concatenate([np.cos(ar), np.cos(ar), np.cos(ac), np.cos(ac)], axis=-1)
    sin_signed = np.concatenate([-np.sin(ar), np.sin(ar), -np.sin(ac), np.sin(ac)], axis=-1)
    return jnp.asarray(cos, F32), jnp.asarray(sin_signed, F32)


def _moe(xf, route, w_gate, w_up, w_down, layer):
    eid = route[:, 0:2].astype(jnp.int32)
    return _moe_experts(_moe_plan(eid, MOE_TM), xf, w_gate, w_up, w_down, layer)


def kernel(x, c, ctx, c_ctx, w_mod, b_mod, w_in, b_in, attn_sink, lru_conv_w, lru_conv_b,
           lru_w_a, lru_b_a, lru_w_x, lru_b_x, lru_lam, sc_conv_w, w_branch, w_out,
           ln1_g, ln1_b, ln2_g, ln2_b, w_grp, b_grp, w_exp, b_exp, w_gate, w_up, w_down):
    assert x.shape == (1, SEQ, D_MODEL) and ctx.shape == (1, CTX_LEN, D_MODEL)
    xs = x[0]
    xc = ctx[0]
    cos, sin_signed = _rope_tables(SEQ)
    c8 = jnp.concatenate([c, c_ctx[None, :], jnp.zeros((SUBLANES - 2, D_MODEL), F32)], axis=0)
    zero_h0 = jnp.zeros((2, LRU_WIDTH), F32)

    mods = [_mod_call(c8, w_mod, b_mod[l], l) for l in range(DEPTH)]
    u = u_ctx = None
    for l in range(DEPTH):
        last = l == DEPTH - 1
        m_lat = mods[l][0:1]
        m_ctx = mods[l][1:2]

        if l == 0:
            u = _lnmod_call(xs, m_lat, 0, 1, BF16)
            u_ctx = _lnmod_call(xc, m_ctx, 0, 1, BF16)
        p, p_ctx = _proj_call(u, u_ctx, w_in, b_in[l], cos, sin_signed, l, tm=1024)

        y_att = _attn_call(p, p_ctx, attn_sink[l])
        lru_args = (lru_conv_w[l], lru_conv_b[l], lru_w_a[l], lru_b_a[l], lru_w_x[l], lru_b_x[l], lru_lam[l])
        hcf, hcr = _lru_call(p_ctx, zero_h0, *lru_args, tc=CTX_LEN)
        h0 = jnp.concatenate([hcf[CTX_LEN - 1:CTX_LEN], hcr[0:1]], axis=0)
        hf, hr = _lru_call(p, h0, *lru_args, tc=1024)
        y_lru, y_sc = _prep_call(p, hf, hr, sc_conv_w[l], tm=512)
        z = _mergez_call(y_att, y_lru, y_sc, w_branch, p, l, tm=2048)

        router = (w_grp[l], b_grp[l], w_exp[l], b_exp[l])
        if last:
            xs, xf, route = _outnorm_call(z, w_out, l, xs, m_lat, 2, ln1_g[l], ln1_b[l],
                                          next_mod=(m_lat, 3, 4), next_dtype=F32, router=router)
            ypair2 = _moe(xf, route, w_gate, w_up, w_down, l)
            xs = _combine_call(ypair2, route, xs, m_lat, 5, ln2_g[l], ln2_b[l], 0)
        else:
            xs = _outnorm_call(z, w_out, l, xs, m_lat, 2, ln1_g[l], ln1_b[l])
            yc_att = _ctx_attn_call(p_ctx, attn_sink[l])
            yc_lru, yc_sc = _prep_call(p_ctx, hcf, hcr, sc_conv_w[l], tm=CTX_LEN)
            zc = _mergez_call(yc_att, yc_lru, yc_sc, w_branch, p_ctx, l, tm=CTX_LEN)
            xc = _outnorm_call(zc, w_out, l, xc, m_ctx, 2, ln1_g[l], ln1_b[l])
            xf, route = _lnmod_pair_call(xs, m_lat, xc, m_ctx, 3, 4, router)
            ypair2 = _moe(xf, route, w_gate, w_up, w_down, l)
            n_lat = mods[l + 1][0:1]
            n_ctx = mods[l + 1][1:2]
            xs, u = _combine_call(ypair2, route, xs, m_lat, 5, ln2_g[l], ln2_b[l], 0,
                                  next_mod=(n_lat, 0, 1), next_dtype=BF16)
            xc, u_ctx = _combine_call(ypair2, route, xc, m_ctx, 5, ln2_g[l], ln2_b[l], SEQ,
                                      next_mod=(n_ctx, 0, 1), next_dtype=BF16)
    return xs[None]
```
